```python
import math
import jax, jax.numpy as jnp
from jax import lax
import numpy as np

D_MODEL = 1024
BATCH = 32
SEQ = 256
DEPTH = 4
DEC_BATCH = 8
DEC_SEQ = 4096
PAST_LEN = 512

GRID_W = 64
N_MIXERS = 2
N_SSM_LAYERS = (DEPTH + 1) // 2
N_ATTN_LAYERS = DEPTH // 2
SSM_GROUP = 16
SSM_GROUPS = D_MODEL // SSM_GROUP
SSM_STATE = 64
DT_MIN = 1e-3
DT_MAX = 1e-1
HEAD_DIM = 64
N_HEADS = D_MODEL // HEAD_DIM
N_KV_HEADS = 4
KV_REP = N_HEADS // N_KV_HEADS
WINDOW = 128
ATTN_BLOCK = 128
ROPE_BASE = 10000.0
ROPE_AXIS_DIM = HEAD_DIM // 2
QKV_DIM = (N_HEADS + 2 * N_KV_HEADS) * HEAD_DIM
PEER_HEADS = 8
PEER_NKEYS = 128
PEER_EXPERTS = PEER_NKEYS * PEER_NKEYS
PEER_TOPK = 16
PEER_KEY_DIM = 256
PEER_HALF = PEER_KEY_DIM // 2
PEER_CHUNK = 128
DN_ALPHA = (2 * DEPTH) ** 0.25
DN_BETA = (8 * DEPTH) ** -0.25
LN_EPS = 1e-5
NEG_INF = -1e30

kernel_name = 'hybrid_s5_swa_peer_diffusion_step'


def _layer_norm(x, g, b):
    xf = x.astype(jnp.float32)
    mu = jnp.mean(xf, axis=-1, keepdims=True)
    var = jnp.mean(jnp.square(xf - mu), axis=-1, keepdims=True)
    y = (xf - mu) * lax.rsqrt(var + LN_EPS) * g.astype(jnp.float32) + b.astype(jnp.float32)
    return y.astype(x.dtype)


def _modulation(cond, w_mod, b_mod):
    m = jax.nn.silu(cond.astype(jnp.float32)).astype(w_mod.dtype) @ w_mod + b_mod
    return [t[:, None, :] for t in jnp.split(m, 6, axis=-1)]


def _modulate(x, shift, scale):
    return x * (1.0 + scale) + shift


def _post_norm(x, out, gate, g, b):
    return _layer_norm(DN_ALPHA * x + (1.0 + gate) * out, g, b)


def _rope_2d(x):
    L = x.shape[1]
    rows = L // GRID_W
    row = jnp.repeat(jnp.arange(rows, dtype=jnp.float32), GRID_W)
    col = jnp.tile(jnp.arange(GRID_W, dtype=jnp.float32), rows)
    n_freq = ROPE_AXIS_DIM // 2
    inv = ROPE_BASE ** (-jnp.arange(n_freq, dtype=jnp.float32) / n_freq)

    def rot(seg, pos):
        ang = pos[:, None] * inv[None, :]
        cos = jnp.cos(ang)[None, :, None, :]
        sin = jnp.sin(ang)[None, :, None, :]
        s1, s2 = seg[..., :n_freq], seg[..., n_freq:]
        return jnp.concatenate([s1 * cos - s2 * sin, s1 * sin + s2 * cos], axis=-1)

    xf = x.astype(jnp.float32)
    y = jnp.concatenate([rot(xf[..., :ROPE_AXIS_DIM], row), rot(xf[..., ROPE_AXIS_DIM:], col)], axis=-1)
    return y.astype(x.dtype)


def _s5_scan_dir(u_c, a_re, a_im, log_dt, b_re, b_im, c_re, c_im, h0):
    A = lax.complex(a_re.astype(jnp.float32), a_im.astype(jnp.float32))
    dt = jnp.exp(log_dt.astype(jnp.float32))[:, None]
    A_bar = jnp.exp(A * dt)
    Bm = lax.complex(b_re.astype(jnp.float32), b_im.astype(jnp.float32))
    B_bar = ((A_bar - 1.0) / A)[..., None] * Bm
    Cm = lax.complex(c_re.astype(jnp.float32), c_im.astype(jnp.float32))
    bu = jnp.einsum('blgc,gpc->blgp', u_c, B_bar)
    bu = bu.at[:, 0].add(A_bar[None] * h0)
    a = jnp.broadcast_to(A_bar, (1,) + bu.shape[1:])

    def combine(e1, e2):
        a1, b1 = e1
        a2, b2 = e2
        return a1 * a2, a2 * b1 + b2

    _, hs = lax.associative_scan(combine, (a, bu), axis=1)
    y = jnp.real(jnp.einsum('blgp,gcp->blgc', hs, Cm))
    return y, hs[:, -1]


def _s5_mixer(h, h0_re, h0_im, w_in, a_re, a_im, log_dt, b_re, b_im, c_re, c_im, d_skip, w_glu, w_out):
    Bsz, L, _ = h.shape
    u = (h @ w_in).astype(jnp.float32)
    u_c = u.astype(jnp.complex64).reshape(Bsz, L, SSM_GROUPS, SSM_GROUP)
    h0 = lax.complex(h0_re.astype(jnp.float32), h0_im.astype(jnp.float32))
    y_f, s_f = _s5_scan_dir(u_c, a_re[0], a_im[0], log_dt[0], b_re[0], b_im[0], c_re[0], c_im[0], h0[:, 0])
    y_b, s_b = _s5_scan_dir(jnp.flip(u_c, axis=1), a_re[1], a_im[1], log_dt[1], b_re[1], b_im[1],
                            c_re[1], c_im[1], h0[:, 1])
    y = (y_f + jnp.flip(y_b, axis=1)).reshape(Bsz, L, D_MODEL) + d_skip.astype(jnp.float32) * u
    y = jax.nn.gelu(y).astype(h.dtype)
    val, gate = jnp.split(y @ w_glu, 2, axis=-1)
    out = (val * jax.nn.sigmoid(gate)) @ w_out
    s = jnp.stack([s_f, s_b], axis=1)
    return out, jnp.real(s), jnp.imag(s)


def _project_qkv(h, w_qkv):
    Bsz, L, _ = h.shape
    qkv = h @ w_qkv
    nq = N_HEADS * HEAD_DIM
    nk = N_KV_HEADS * HEAD_DIM
    q = qkv[..., :nq].reshape(Bsz, L, N_HEADS, HEAD_DIM)
    k = qkv[..., nq:nq + nk].reshape(Bsz, L, N_KV_HEADS, HEAD_DIM)
    v = qkv[..., nq + nk:].reshape(Bsz, L, N_KV_HEADS, HEAD_DIM)
    return q, k, v


def _block_attention(q, k_ctx, v_ctx, sink, k_lat=None, v_lat=None):
    Bsz, T = q.shape[0], q.shape[1]
    Lc = k_ctx.shape[1]
    nb = T // ATTN_BLOCK
    scale = HEAD_DIM ** -0.5
    sink_l = sink.astype(jnp.float32).reshape(1, N_KV_HEADS, KV_REP, 1, 1)
    span = ATTN_BLOCK + 2 * WINDOW
    if k_lat is not None:
        pad = ((0, 0), (WINDOW, WINDOW), (0, 0), (0, 0))
        k_pad = jnp.pad(k_lat, pad)
        v_pad = jnp.pad(v_lat, pad)
        qi = jnp.arange(ATTN_BLOCK)[:, None]
        kj = jnp.arange(span)[None, :]
        rel_ok = (kj - qi >= 0) & (kj - qi <= 2 * WINDOW)

    def one_block(b):
        q_b = lax.dynamic_slice_in_dim(q, b * ATTN_BLOCK, ATTN_BLOCK, axis=1)
        q_b = q_b.reshape(Bsz, ATTN_BLOCK, N_KV_HEADS, KV_REP, HEAD_DIM)
        logits = [jnp.broadcast_to(sink_l, (Bsz, N_KV_HEADS, KV_REP, ATTN_BLOCK, 1)),
                  jnp.einsum('bqgrd,bkgd->bgrqk', q_b, k_ctx).astype(jnp.float32) * scale]
        if k_lat is not None:
            k_b = lax.dynamic_slice_in_dim(k_pad, b * ATTN_BLOCK, span, axis=1)
            v_b = lax.dynamic_slice_in_dim(v_pad, b * ATTN_BLOCK, span, axis=1)
            kpos = b * ATTN_BLOCK - WINDOW + kj
            ok = rel_ok & (kpos >= 0) & (kpos < T)
            s_loc = jnp.einsum('bqgrd,bkgd->bgrqk', q_b, k_b).astype(jnp.float32) * scale
            logits.append(jnp.where(ok, s_loc, NEG_INF))
        p = jax.nn.softmax(jnp.concatenate(logits, axis=-1), axis=-1)
        out = jnp.einsum('bgrqk,bkgd->bqgrd', p[..., 1:1 + Lc].astype(v_ctx.dtype), v_ctx)
        if k_lat is not None:
            out = out + jnp.einsum('bgrqk,bkgd->bqgrd', p[..., 1 + Lc:].astype(v_b.dtype), v_b)
        return out.reshape(Bsz, ATTN_BLOCK, N_HEADS * HEAD_DIM)

    out = lax.map(one_block, jnp.arange(nb))
    return jnp.transpose(out, (1, 0, 2, 3)).reshape(Bsz, T, N_HEADS * HEAD_DIM)


def _attn_context(h, w_qkv, sink, w_out):
    q, k, v = _project_qkv(h, w_qkv)
    out = _block_attention(q, k, v, sink)
    return out.astype(h.dtype) @ w_out, k, v


def _attn_latent(h, k_ctx, v_ctx, w_qkv, sink, w_out):
    q, k, v = _project_qkv(h, w_qkv)
    q = _rope_2d(q)
    k = _rope_2d(k)
    out = _block_attention(q, k_ctx, v_ctx, sink, k, v)
    return out.astype(h.dtype) @ w_out


def _peer(h, w_q, keys, u_tab, v_tab):
    Bsz, L, _ = h.shape
    T = Bsz * L
    x = h.reshape(T, D_MODEL)
    q = (x @ w_q).reshape(T, PEER_HEADS, 2, PEER_HALF)
    s1 = jnp.einsum('thd,hnd->thn', q[:, :, 0], keys[0])
    s2 = jnp.einsum('thd,hnd->thn', q[:, :, 1], keys[1])
    v1, i1 = lax.top_k(s1, PEER_TOPK)
    v2, i2 = lax.top_k(s2, PEER_TOPK)
    cand = (v1[..., :, None] + v2[..., None, :]).reshape(T, PEER_HEADS, PEER_TOPK * PEER_TOPK)
    cand_idx = (i1[..., :, None] * PEER_NKEYS + i2[..., None, :]).reshape(T, PEER_HEADS, PEER_TOPK * PEER_TOPK)
    top, pos = lax.top_k(cand, PEER_TOPK)
    idx = jnp.take_along_axis(cand_idx, pos, axis=-1)
    g = jax.nn.softmax(top.astype(jnp.float32), axis=-1).astype(h.dtype)
    nc = T // PEER_CHUNK

    def chunk(args):
        xc, ic, gc = args
        u = u_tab[ic]
        act = jax.nn.gelu(jnp.einsum('chkd,cd->chk', u, xc)) * gc
        return jnp.einsum('chk,chkd->cd', act, v_tab[ic])

    out = lax.map(chunk, (x.reshape(nc, PEER_CHUNK, D_MODEL),
                          idx.reshape(nc, PEER_CHUNK, PEER_HEADS, PEER_TOPK),
                          g.reshape(nc, PEER_CHUNK, PEER_HEADS, PEER_TOPK)))
    return out.reshape(Bsz, L, D_MODEL)


def setup_inputs(seed: int = 0) -> dict:
    key = jax.random.key(seed)
    ks = jax.random.split(key, 32)
    f32 = jnp.float32
    inv_d = D_MODEL ** -0.5

    def nrm(k, shape, s=1.0):
        return jax.random.normal(k, shape, f32) * s

    ssm_shape = (N_SSM_LAYERS, 2, SSM_GROUPS, SSM_STATE)
    a_im_init = math.pi * jnp.arange(SSM_STATE, dtype=f32)
    return {
        'x_prompt': nrm(ks[0], (BATCH, SEQ, D_MODEL)),
        'x_sample': nrm(ks[1], (DEC_BATCH, DEC_SEQ, D_MODEL)),
        'state_ssm_re': nrm(ks[2], (DEC_BATCH, N_SSM_LAYERS, 2, SSM_GROUPS, SSM_STATE), 0.3),
        'state_ssm_im': nrm(ks[3], (DEC_BATCH, N_SSM_LAYERS, 2, SSM_GROUPS, SSM_STATE), 0.3),
        'cache_k': nrm(ks[4], (DEC_BATCH, N_ATTN_LAYERS, PAST_LEN, N_KV_HEADS, HEAD_DIM)),
        'cache_v': nrm(ks[5], (DEC_BATCH, N_ATTN_LAYERS, PAST_LEN, N_KV_HEADS, HEAD_DIM)),
        'c': nrm(ks[6], (DEC_BATCH, D_MODEL)),
        'c_ctx': nrm(ks[7], (D_MODEL,)),
        'w_mod': nrm(ks[8], (DEPTH, D_MODEL, 6 * D_MODEL), 0.5 * inv_d),
        'b_mod': nrm(ks[9], (DEPTH, 6 * D_MODEL), 0.01),
        'ln_g': 1.0 + nrm(ks[10], (DEPTH, 2, D_MODEL), 0.01),
        'ln_b': nrm(ks[11], (DEPTH, 2, D_MODEL), 0.01),
        'ssm_w_in': nrm(ks[12], (N_SSM_LAYERS, D_MODEL, D_MODEL), inv_d),
        'ssm_a_re': -0.5 + nrm(ks[13], ssm_shape, 0.01),
        'ssm_a_im': a_im_init + nrm(ks[14], ssm_shape, 0.01),
        'ssm_log_dt': jax.random.uniform(ks[15], (N_SSM_LAYERS, 2, SSM_GROUPS), f32,
                                         math.log(DT_MIN), math.log(DT_MAX)),
        'ssm_b_re': nrm(ks[16], ssm_shape + (SSM_GROUP,), (2 * SSM_GROUP) ** -0.5),
        'ssm_b_im': nrm(ks[17], ssm_shape + (SSM_GROUP,), (2 * SSM_GROUP) ** -0.5),
        'ssm_c_re': nrm(ks[18], (N_SSM_LAYERS, 2, SSM_GROUPS, SSM_GROUP, SSM_STATE), SSM_STATE ** -0.5),
        'ssm_c_im': nrm(ks[19], (N_SSM_LAYERS, 2, SSM_GROUPS, SSM_GROUP, SSM_STATE), SSM_STATE ** -0.5),
        'ssm_d': nrm(ks[20], (N_SSM_LAYERS, D_MODEL)),
        'ssm_w_glu': nrm(ks[21], (N_SSM_LAYERS, D_MODEL, 2 * D_MODEL), inv_d),
        'ssm_w_out': nrm(ks[22], (N_SSM_LAYERS, D_MODEL, D_MODEL), inv_d * DN_BETA),
        'attn_w_qkv': nrm(ks[23], (N_ATTN_LAYERS, D_MODEL, QKV_DIM), inv_d),
        'attn_sink': nrm(ks[24], (N_ATTN_LAYERS, N_HEADS)),
        'attn_w_out': nrm(ks[25], (N_ATTN_LAYERS, D_MODEL, D_MODEL), inv_d * DN_BETA),
        'peer_w_q': nrm(ks[26], (DEPTH, D_MODEL, PEER_HEADS * PEER_KEY_DIM), inv_d),
        'peer_keys': nrm(ks[27], (DEPTH, 2, PEER_HEADS, PEER_NKEYS, PEER_HALF), PEER_HALF ** -0.5),
        'peer_u': nrm(ks[28], (DEPTH, PEER_EXPERTS, D_MODEL), inv_d),
        'peer_v': nrm(ks[29], (DEPTH, PEER_EXPERTS, D_MODEL), DN_BETA * PEER_HEADS ** -0.5),
    }


def reference(x_prompt, x_sample, state_ssm_re, state_ssm_im, cache_k, cache_v, c, c_ctx,
              w_mod, b_mod, ln_g, ln_b,
              ssm_w_in, ssm_a_re, ssm_a_im, ssm_log_dt, ssm_b_re, ssm_b_im, ssm_c_re, ssm_c_im,
              ssm_d, ssm_w_glu, ssm_w_out,
              attn_w_qkv, attn_sink, attn_w_out,
              peer_w_q, peer_keys, peer_u, peer_v):
    xc = x_prompt
    xs = x_sample
    n_ctx_batch = x_prompt.shape[0]
    st_re, st_im, st_k, st_v = [], [], [], []
    for i in range(DEPTH):
        j = i // N_MIXERS
        mc = _modulation(c_ctx[None, :], w_mod[i], b_mod[i])
        ms = _modulation(c, w_mod[i], b_mod[i])
        hc = _modulate(xc, mc[0], mc[1])
        hs = _modulate(xs, ms[0], ms[1])
        if i % N_MIXERS == 0:
            p = (ssm_w_in[j], ssm_a_re[j], ssm_a_im[j], ssm_log_dt[j], ssm_b_re[j], ssm_b_im[j],
                 ssm_c_re[j], ssm_c_im[j], ssm_d[j], ssm_w_glu[j], ssm_w_out[j])
            zeros = jnp.zeros((n_ctx_batch, 2, SSM_GROUPS, SSM_STATE), jnp.float32)
            oc, s_re, s_im = _s5_mixer(hc, zeros, zeros, *p)
            os_, _, _ = _s5_mixer(hs, state_ssm_re[:, j], state_ssm_im[:, j], *p)
            st_re.append(s_re)
            st_im.append(s_im)
        else:
            oc, k_c, v_c = _attn_context(hc, attn_w_qkv[j], attn_sink[j], attn_w_out[j])
            os_ = _attn_latent(hs, cache_k[:, j], cache_v[:, j], attn_w_qkv[j], attn_sink[j], attn_w_out[j])
            st_k.append(k_c)
            st_v.append(v_c)
        xc = _post_norm(xc, oc, mc[2], ln_g[i, 0], ln_b[i, 0])
        xs = _post_norm(xs, os_, ms[2], ln_g[i, 0], ln_b[i, 0])
        hc = _modulate(xc, mc[3], mc[4])
        hs = _modulate(xs, ms[3], ms[4])
        oc = _peer(hc, peer_w_q[i], peer_keys[i], peer_u[i], peer_v[i])
        os_ = _peer(hs, peer_w_q[i], peer_keys[i], peer_u[i], peer_v[i])
        xc = _post_norm(xc, oc, mc[5], ln_g[i, 1], ln_b[i, 1])
        xs = _post_norm(xs, os_, ms[5], ln_g[i, 1], ln_b[i, 1])
    new_state_ssm_re = jnp.stack(st_re, axis=1)
    new_state_ssm_im = jnp.stack(st_im, axis=1)
    new_cache_k = jnp.stack(st_k, axis=1)
    new_cache_v = jnp.stack(st_v, axis=1)
    return (xc, xs, new_state_ssm_re, new_state_ssm_im, new_cache_k, new_cache_v)
```

```python
import functools
import math

import jax
import jax.numpy as jnp
from jax import lax
from jax.experimental import pallas as pl
from jax.experimental.pallas import tpu as pltpu

F32 = jnp.float32
BF16 = jnp.bfloat16

D_MODEL = 1024
GRID_W = 64
SSM_GROUP = 16
SSM_GROUPS = D_MODEL // SSM_GROUP
SSM_STATE = 64
SSM_CHUNK = 16
SSM_ROW = SSM_CHUNK * SSM_GROUP
HEAD_DIM = 64
N_HEADS = D_MODEL // HEAD_DIM
N_KV_HEADS = 4
KV_REP = N_HEADS // N_KV_HEADS
KV_DIM = N_KV_HEADS * HEAD_DIM
WINDOW = 128
ATTN_BLOCK = 128
ROPE_BASE = 10000.0
ROPE_AXIS_DIM = HEAD_DIM // 2
NEG_INF = -1e30
PEER_HEADS = 8
PEER_NKEYS = 128
PEER_EXPERTS = PEER_NKEYS * PEER_NKEYS
PEER_TOPK = 16
PEER_KEY_DIM = 256
PEER_HALF = PEER_KEY_DIM // 2
LN_EPS = 1e-5

TOKEN_TILE = 512
EXPERT_BLOCK = 512
MOD_ROWS = 16
VMEM_LIMIT = 56 * 1024 * 1024

_NT = (((1,), (1,)), ((), ()))


def _cparams(*sem):
    return pltpu.CompilerParams(dimension_semantics=sem, vmem_limit_bytes=VMEM_LIMIT)


def _gelu(x):
    return 0.5 * x * (1.0 + jnp.tanh(0.7978845608028654 * (x + 0.044715 * (x * x * x))))


def _layer_norm_rows(z, g, b):
    mu = jnp.mean(z, axis=-1, keepdims=True)
    zc = z - mu
    var = jnp.mean(zc * zc, axis=-1, keepdims=True)
    return zc * lax.rsqrt(var + LN_EPS) * g + b


def _mod_kernel(c_ref, w_ref, b_ref, o_ref):
    c = c_ref[...]
    s = c * jax.nn.sigmoid(c)
    o_ref[...] = jnp.dot(s.astype(BF16), w_ref[...].astype(BF16),
                         preferred_element_type=F32) + b_ref[...]


def _modulation_all(cond, w_mod, b_mod):
    depth = w_mod.shape[0]
    n_out = w_mod.shape[2]
    nb = n_out // D_MODEL
    out = pl.pallas_call(
        _mod_kernel,
        grid=(depth, nb),
        in_specs=[
            pl.BlockSpec((MOD_ROWS, D_MODEL), lambda i, j: (0, 0)),
            pl.BlockSpec((None, D_MODEL, D_MODEL), lambda i, j: (i, 0, j)),
            pl.BlockSpec((None, 1, D_MODEL), lambda i, j: (i, 0, j)),
        ],
        out_specs=pl.BlockSpec((None, MOD_ROWS, D_MODEL), lambda i, j: (i, 0, j)),
        out_shape=jax.ShapeDtypeStruct((depth, MOD_ROWS, n_out), F32),
        compiler_params=_cparams("arbitrary", "arbitrary"),
        name="modulation",
    )(cond, w_mod, b_mod.reshape(depth, 1, n_out))
    return out.reshape(depth, MOD_ROWS, nb, D_MODEL)


class _Tokens:
    def __init__(self, n_ctx_tokens, dec_batch, dec_seq):
        self.n_ctx = n_ctx_tokens
        self.dec_batch = dec_batch
        self.dec_seq = dec_seq
        self.total = n_ctx_tokens + dec_batch * dec_seq
        assert n_ctx_tokens % TOKEN_TILE == 0 and dec_seq % TOKEN_TILE == 0
        self.ctx_tiles = n_ctx_tokens // TOKEN_TILE
        self.tiles = self.total // TOKEN_TILE
        self.tiles_per_seq = dec_seq // TOKEN_TILE

    def mod_row(self, i):
        return jnp.where(i < self.ctx_tiles, self.dec_batch, (i - self.ctx_tiles) // self.tiles_per_seq)


def _mod_spec(tok):
    return pl.BlockSpec((1, 6, D_MODEL), lambda i: (tok.mod_row(i), 0, 0))


def _row_spec(width):
    return pl.BlockSpec((TOKEN_TILE, width), lambda i: (i, 0))


def _full_spec(shape):
    nd = len(shape)
    return pl.BlockSpec(shape, lambda i: (0,) * nd)


def _linear_kernel(x_ref, m_ref, w_ref, o_ref, *, shift_idx):
    m = m_ref[0]
    h = x_ref[...] * (1.0 + m[shift_idx + 1:shift_idx + 2]) + m[shift_idx:shift_idx + 1]
    o_ref[...] = jnp.dot(h.astype(BF16), w_ref[...], preferred_element_type=F32)


def _mod_linear(tok, x, mods, w, shift_idx, name):
    n = w.shape[1]
    return pl.pallas_call(
        functools.partial(_linear_kernel, shift_idx=shift_idx),
        grid=(tok.tiles,),
        in_specs=[_row_spec(D_MODEL), _mod_spec(tok), _full_spec((D_MODEL, n))],
        out_specs=_row_spec(n),
        out_shape=jax.ShapeDtypeStruct((tok.total, n), F32),
        compiler_params=_cparams("parallel"),
        name=name,
    )(x, mods, w)


def _post_norm_rows(x, out, m, gate_idx, g_ref, b_ref, alpha):
    z = alpha * x + (1.0 + m[gate_idx:gate_idx + 1]) * out
    return _layer_norm_rows(z, g_ref[...], b_ref[...])


def _mm_postnorm_kernel(a_ref, x_ref, m_ref, w_ref, g_ref, b_ref, o_ref, *, gate_idx, alpha):
    y = jnp.dot(a_ref[...].astype(BF16), w_ref[...], preferred_element_type=F32)
    o_ref[...] = _post_norm_rows(x_ref[...], y, m_ref[0], gate_idx, g_ref, b_ref, alpha)


def _mm_postnorm(tok, a, x, mods, w, g, b, gate_idx, alpha, name):
    return pl.pallas_call(
        functools.partial(_mm_postnorm_kernel, gate_idx=gate_idx, alpha=alpha),
        grid=(tok.tiles,),
        in_specs=[_row_spec(D_MODEL), _row_spec(D_MODEL), _mod_spec(tok),
                  _full_spec((D_MODEL, D_MODEL)), _full_spec((1, D_MODEL)), _full_spec((1, D_MODEL))],
        out_specs=_row_spec(D_MODEL),
        out_shape=jax.ShapeDtypeStruct((tok.total, D_MODEL), F32),
        compiler_params=_cparams("parallel"),
        name=name,
    )(a, x, mods, w, g.reshape(1, D_MODEL), b.reshape(1, D_MODEL))


def _postnorm_kernel(o_in_ref, x_ref, m_ref, g_ref, b_ref, o_ref, *, gate_idx, alpha):
    o_ref[...] = _post_norm_rows(x_ref[...], o_in_ref[...], m_ref[0], gate_idx, g_ref, b_ref, alpha)


def _postnorm(tok, out, x, mods, g, b, gate_idx, alpha, name):
    return pl.pallas_call(
        functools.partial(_postnorm_kernel, gate_idx=gate_idx, alpha=alpha),
        grid=(tok.tiles,),
        in_specs=[_row_spec(D_MODEL), _row_spec(D_MODEL), _mod_spec(tok),
                  _full_spec((1, D_MODEL)), _full_spec((1, D_MODEL))],
        out_specs=_row_spec(D_MODEL),
        out_shape=jax.ShapeDtypeStruct((tok.total, D_MODEL), F32),
        compiler_params=_cparams("parallel"),
        name=name,
    )(out, x, mods, g.reshape(1, D_MODEL), b.reshape(1, D_MODEL))


def _s5_operators(a_re, a_im, log_dt, b_re, b_im, c_re, c_im):
    hi = lax.Precision.HIGHEST
    dt = jnp.exp(log_dt)[..., None]
    ar, ai = a_re * dt, a_im * dt
    k = jnp.arange(SSM_CHUNK + 1, dtype=F32)[:, None, None, None]
    mag = jnp.exp(k * ar[None])
    pw_re, pw_im = mag * jnp.cos(k * ai[None]), mag * jnp.sin(k * ai[None])
    l_re, l_im = pw_re[1], pw_im[1]
    den = a_re * a_re + a_im * a_im
    q_re = ((l_re - 1.0) * a_re + l_im * a_im) / den
    q_im = (l_im * a_re - (l_re - 1.0) * a_im) / den
    bt_re = q_re[..., None] * b_re - q_im[..., None] * b_im
    bt_im = q_re[..., None] * b_im + q_im[..., None] * b_re
    d_re = pw_re[..., None] * bt_re[None] - pw_im[..., None] * bt_im[None]
    d_im = pw_re[..., None] * bt_im[None] + pw_im[..., None] * bt_re[None]
    pw_re_c, pw_im_c = pw_re[:, :, :, None, :], pw_im[:, :, :, None, :]
    e_re = c_re[None] * pw_re_c - c_im[None] * pw_im_c
    e_im = c_re[None] * pw_im_c + c_im[None] * pw_re_c
    kern = (jnp.einsum('dgip,kdgpj->kdgij', c_re, d_re, precision=hi)
            - jnp.einsum('dgip,kdgpj->kdgij', c_im, d_im, precision=hi))
    s = jnp.arange(SSM_CHUNK)
    lag = s[None, :] - s[:, None]
    kf = kern[jnp.clip(lag, 0, SSM_CHUNK), 0]
    kb = kern[jnp.clip(-lag, 0, SSM_CHUNK), 1]
    mix = (jnp.where((lag >= 0)[:, :, None, None, None], kf, 0.0)
           + jnp.where((lag <= 0)[:, :, None, None, None], kb, 0.0))
    g = SSM_GROUPS
    m_op = jnp.transpose(mix, (2, 0, 4, 1, 3)).reshape(g, SSM_ROW, SSM_ROW)
    pf = (SSM_CHUNK - 1) - s
    win = jnp.stack([d_re[pf, 0], d_re[s, 1], d_im[pf, 0], d_im[s, 1]], axis=0)
    win_op = jnp.transpose(win, (2, 1, 4, 0, 3)).reshape(g, SSM_ROW, 4 * SSM_STATE)
    wout = jnp.stack([e_re[s + 1, 0], e_re[SSM_CHUNK - s, 1], -e_im[s + 1, 0], -e_im[SSM_CHUNK - s, 1]],
                     axis=0)
    wout_op = jnp.transpose(wout, (2, 0, 4, 1, 3)).reshape(g, 4 * SSM_STATE, SSM_ROW)
    lam = jnp.stack([jnp.concatenate([pw_re[SSM_CHUNK, 0], pw_re[SSM_CHUNK, 1]], axis=-1),
                     jnp.concatenate([pw_im[SSM_CHUNK, 0], pw_im[SSM_CHUNK, 1]], axis=-1)], axis=1)
    return m_op, win_op, wout_op, lam


def _s5_core_kernel(u_ref, h0_ref, m_ref, win_ref, wout_ref, lam_ref, y_ref, s_ref, z_scr, p_scr,
                    *, n_chunks, batch):
    hi = lax.Precision.HIGHEST
    p2 = 2 * SSM_STATE
    u = u_ref[...]
    z_scr[...] = jnp.dot(u, win_ref[...], preferred_element_type=F32, precision=hi)
    l_re = lam_ref[0:1, :]
    l_im = lam_ref[1:2, :]
    is_fwd = lax.broadcasted_iota(jnp.int32, (batch, p2), 1) < SSM_STATE

    def step(k, carry):
        s_re, s_im = carry
        rf = pl.multiple_of(k * batch, batch)
        rb = pl.multiple_of((n_chunks - 1 - k) * batch, batch)
        zf = z_scr[pl.ds(rf, batch), :]
        zb = z_scr[pl.ds(rb, batch), :]
        z_re = jnp.where(is_fwd, zf[:, :p2], zb[:, :p2])
        z_im = jnp.where(is_fwd, zf[:, p2:], zb[:, p2:])
        p_scr[pl.ds(rf, batch), 0:SSM_STATE] = s_re[:, :SSM_STATE]
        p_scr[pl.ds(rf, batch), p2:p2 + SSM_STATE] = s_im[:, :SSM_STATE]
        p_scr[pl.ds(rb, batch), SSM_STATE:p2] = s_re[:, SSM_STATE:]
        p_scr[pl.ds(rb, batch), p2 + SSM_STATE:] = s_im[:, SSM_STATE:]
        n_re = l_re * s_re - l_im * s_im + z_re
        n_im = l_re * s_im + l_im * s_re + z_im
        return n_re, n_im

    h0 = h0_ref[...]
    s_re, s_im = lax.fori_loop(0, n_chunks, step, (h0[:, :p2], h0[:, p2:]))
    s_ref[:, :p2] = s_re
    s_ref[:, p2:] = s_im
    y_ref[...] = (jnp.dot(u, m_ref[...], preferred_element_type=F32, precision=hi)
                  + jnp.dot(p_scr[...], wout_ref[...], preferred_element_type=F32, precision=hi))


def _s5_core(ug, h0, ops, n_chunks, batch, name):
    m_op, win_op, wout_op, lam = ops
    g, rows, _ = ug.shape
    st = 4 * SSM_STATE
    return pl.pallas_call(
        functools.partial(_s5_core_kernel, n_chunks=n_chunks, batch=batch),
        grid=(g,),
        in_specs=[
            pl.BlockSpec((None, rows, SSM_ROW), lambda i: (i, 0, 0)),
            pl.BlockSpec((None, batch, st), lambda i: (i, 0, 0)),
            pl.BlockSpec((None, SSM_ROW, SSM_ROW), lambda i: (i, 0, 0)),
            pl.BlockSpec((None, SSM_ROW, st), lambda i: (i, 0, 0)),
            pl.BlockSpec((None, st, SSM_ROW), lambda i: (i, 0, 0)),
            pl.BlockSpec((None, 2, 2 * SSM_STATE), lambda i: (i, 0, 0)),
        ],
        out_specs=[
            pl.BlockSpec((None, rows, SSM_ROW), lambda i: (i, 0, 0)),
            pl.BlockSpec((None, batch, st), lambda i: (i, 0, 0)),
        ],
        out_shape=[jax.ShapeDtypeStruct((g, rows, SSM_ROW), F32),
                   jax.ShapeDtypeStruct((g, batch, st), F32)],
        scratch_shapes=[pltpu.VMEM((rows, st), F32), pltpu.VMEM((rows, st), F32)],
        compiler_params=_cparams("parallel"),
        name=name,
    )(ug, h0, m_op, win_op, wout_op, lam)


def _to_chunks(u, batch, seq):
    n_chunks = seq // SSM_CHUNK
    t = u.reshape(batch, n_chunks, SSM_CHUNK, SSM_GROUPS, SSM_GROUP)
    return jnp.transpose(t, (3, 1, 0, 2, 4)).reshape(SSM_GROUPS, n_chunks * batch, SSM_ROW)


def _from_chunks(y, batch, seq):
    n_chunks = seq // SSM_CHUNK
    t = y.reshape(SSM_GROUPS, n_chunks, batch, SSM_CHUNK, SSM_GROUP)
    return jnp.transpose(t, (2, 1, 3, 0, 4)).reshape(batch * seq, D_MODEL)


def _state_to_cols(h_re, h_im):
    cols = jnp.concatenate([h_re[:, 0], h_re[:, 1], h_im[:, 0], h_im[:, 1]], axis=-1)
    return jnp.transpose(cols, (1, 0, 2))


def _cols_to_state(s):
    t = jnp.transpose(s, (1, 0, 2)).reshape(s.shape[1], SSM_GROUPS, 2, 2, SSM_STATE)
    t = jnp.transpose(t, (2, 0, 3, 1, 4))
    return t[0], t[1]


def _s5_tail_kernel(y_ref, u_ref, x_ref, m_ref, d_ref, wg_ref, wo_ref, g_ref, b_ref, o_ref, *, alpha):
    v = _gelu(y_ref[...] + d_ref[...] * u_ref[...])
    gl = jnp.dot(v.astype(BF16), wg_ref[...], preferred_element_type=F32)
    a = gl[:, :D_MODEL] * jax.nn.sigmoid(gl[:, D_MODEL:])
    out = jnp.dot(a.astype(BF16), wo_ref[...], preferred_element_type=F32)
    o_ref[...] = _post_norm_rows(x_ref[...], out, m_ref[0], 2, g_ref, b_ref, alpha)


def _s5_tail(tok, y, u, x, mods, d, w_glu, w_out, g, b, alpha):
    return pl.pallas_call(
        functools.partial(_s5_tail_kernel, alpha=alpha),
        grid=(tok.tiles,),
        in_specs=[_row_spec(D_MODEL), _row_spec(D_MODEL), _row_spec(D_MODEL), _mod_spec(tok),
                  _full_spec((1, D_MODEL)), _full_spec((D_MODEL, 2 * D_MODEL)),
                  _full_spec((D_MODEL, D_MODEL)), _full_spec((1, D_MODEL)), _full_spec((1, D_MODEL))],
        out_specs=_row_spec(D_MODEL),
        out_shape=jax.ShapeDtypeStruct((tok.total, D_MODEL), F32),
        compiler_params=_cparams("parallel"),
        name="s5_tail",
    )(y, u, x, mods, d.reshape(1, D_MODEL), w_glu, w_out, g.reshape(1, D_MODEL), b.reshape(1, D_MODEL))


QKV_COLS = 3 * D_MODEL


def _attn_weights(w_qkv):
    wq, wk, wv = w_qkv[:, :D_MODEL], w_qkv[:, D_MODEL:D_MODEL + KV_DIM], w_qkv[:, D_MODEL + KV_DIM:]
    half = ROPE_AXIS_DIM // 2

    def partner(w):
        n = w.shape[1]
        t = w.reshape(D_MODEL, n // ROPE_AXIS_DIM, 2, half)
        return jnp.stack([-t[:, :, 1], t[:, :, 0]], axis=2).reshape(D_MODEL, n)

    pad = jnp.zeros((D_MODEL, QKV_COLS - 2 * D_MODEL - 3 * KV_DIM), w_qkv.dtype)
    return jnp.concatenate([wq, partner(wq), wk, wv, partner(wk), pad], axis=1).astype(BF16)


def _rope_tables(seq):
    half = ROPE_AXIS_DIM // 2
    rows = seq // GRID_W
    row = jnp.repeat(jnp.arange(rows, dtype=F32), GRID_W)
    col = jnp.tile(jnp.arange(GRID_W, dtype=F32), rows)
    inv = ROPE_BASE ** (-jnp.arange(half, dtype=F32) / half)
    ang_r = row[:, None] * inv[None, :]
    ang_c = col[:, None] * inv[None, :]
    ang = jnp.concatenate([ang_r, ang_r, ang_c, ang_c], axis=-1)
    cos = jnp.tile(jnp.cos(ang), (1, N_KV_HEADS))
    sin = jnp.tile(jnp.sin(ang), (1, N_KV_HEADS))
    return jnp.concatenate([cos, sin], axis=-1)


def _softmax_heads(q, k_list, v_list, mask_list, sink_ref, o_ref):
    scale = HEAD_DIM ** -0.5
    for h in range(N_HEADS):
        g = h // KV_REP
        qh = q[:, h * HEAD_DIM:(h + 1) * HEAD_DIM].astype(BF16)
        sink = sink_ref[h]
        logits = []
        for k, mask in zip(k_list, mask_list):
            s = lax.dot_general(qh, k[:, g * HEAD_DIM:(g + 1) * HEAD_DIM], _NT,
                                preferred_element_type=F32) * scale
            if mask is not None:
                s = jnp.where(mask, s, NEG_INF)
            logits.append(s)
        m = jnp.full((q.shape[0], 1), sink, F32)
        for s in logits:
            m = jnp.maximum(m, jnp.max(s, axis=-1, keepdims=True))
        denom = jnp.exp(sink - m)
        acc = jnp.zeros((q.shape[0], HEAD_DIM), F32)
        for s, v in zip(logits, v_list):
            p = jnp.exp(s - m)
            denom = denom + jnp.sum(p, axis=-1, keepdims=True)
            acc = acc + jnp.dot(p.astype(BF16), v[:, g * HEAD_DIM:(g + 1) * HEAD_DIM],
                                preferred_element_type=F32)
        o_ref[:, h * HEAD_DIM:(h + 1) * HEAD_DIM] = acc / denom


def _attn_ctx_kernel(sink_ref, q_ref, kv_ref, o_ref):
    kv = kv_ref[...]
    k = kv[:, :KV_DIM].astype(BF16)
    v = kv[:, KV_DIM:2 * KV_DIM].astype(BF16)
    _softmax_heads(q_ref[...], [k], [v], [None], sink_ref, o_ref)


def _attn_context(qkv, sink, n_batch, seq):
    return pl.pallas_call(
        _attn_ctx_kernel,
        grid=(n_batch,),
        in_specs=[
            pl.BlockSpec(memory_space=pltpu.SMEM),
            pl.BlockSpec((seq, D_MODEL), lambda b: (b, 0)),
            pl.BlockSpec((seq, D_MODEL), lambda b: (b, 2)),
        ],
        out_specs=pl.BlockSpec((seq, D_MODEL), lambda b: (b, 0)),
        out_shape=jax.ShapeDtypeStruct((n_batch * seq, D_MODEL), F32),
        compiler_params=_cparams("parallel"),
        name="attn_context",
    )(sink, qkv, qkv)


def _attn_lat_kernel(sink_ref, q_ref, qp_ref, kv0_ref, kv1_ref, kv2_ref, t0_ref, t1_ref, t2_ref,
                     kc_ref, vc_ref, o_ref, *, n_qblocks):
    qb = pl.program_id(1)
    t1 = t1_ref[...]
    cos_q = jnp.concatenate([t1[:, :KV_DIM]] * KV_REP, axis=-1)
    sin_q = jnp.concatenate([t1[:, KV_DIM:]] * KV_REP, axis=-1)
    q = q_ref[...] * cos_q + qp_ref[...] * sin_q

    def rope_k(kv_ref, t_ref):
        kv = kv_ref[...]
        t = t_ref[...]
        k = kv[:, :KV_DIM] * t[:, :KV_DIM] + kv[:, 2 * KV_DIM:3 * KV_DIM] * t[:, KV_DIM:]
        return k.astype(BF16), kv[:, KV_DIM:2 * KV_DIM].astype(BF16)

    k0, v0 = rope_k(kv0_ref, t0_ref)
    k1, v1 = rope_k(kv1_ref, t1_ref)
    k2, v2 = rope_k(kv2_ref, t2_ref)
    k_loc = jnp.concatenate([k0, k1, k2], axis=0)
    v_loc = jnp.concatenate([v0, v1, v2], axis=0)
    span = 3 * ATTN_BLOCK
    qi = lax.broadcasted_iota(jnp.int32, (ATTN_BLOCK, span), 0)
    kj = lax.broadcasted_iota(jnp.int32, (ATTN_BLOCK, span), 1) - ATTN_BLOCK
    ok = (jnp.abs(kj - qi) <= WINDOW)
    ok = ok & ((kj >= 0) | (qb > 0)) & ((kj < ATTN_BLOCK) | (qb < n_qblocks - 1))
    kc = kc_ref[0].astype(BF16)
    vc = vc_ref[0].astype(BF16)
    _softmax_heads(q, [kc, k_loc], [vc, v_loc], [None, ok], sink_ref, o_ref)


def _attn_latent(qkv, tables, k_ctx, v_ctx, sink, tok):
    nqb = tok.dec_seq // ATTN_BLOCK
    base = tok.n_ctx // ATTN_BLOCK
    past = k_ctx.shape[1]

    def rows(b, i, off):
        return base + b * nqb + jnp.clip(i + off, 0, nqb - 1)

    def kv_spec(off):
        return pl.BlockSpec((ATTN_BLOCK, D_MODEL), lambda b, i: (rows(b, i, off), 2))

    def tab_spec(off):
        return pl.BlockSpec((ATTN_BLOCK, 2 * KV_DIM), lambda b, i: (jnp.clip(i + off, 0, nqb - 1), 0))

    return pl.pallas_call(
        functools.partial(_attn_lat_kernel, n_qblocks=nqb),
        grid=(tok.dec_batch, nqb),
        in_specs=[
            pl.BlockSpec(memory_space=pltpu.SMEM),
            pl.BlockSpec((ATTN_BLOCK, D_MODEL), lambda b, i: (rows(b, i, 0), 0)),
            pl.BlockSpec((ATTN_BLOCK, D_MODEL), lambda b, i: (rows(b, i, 0), 1)),
            kv_spec(-1), kv_spec(0), kv_spec(1),
            tab_spec(-1), tab_spec(0), tab_spec(1),
            pl.BlockSpec((1, past, KV_DIM), lambda b, i: (b, 0, 0)),
            pl.BlockSpec((1, past, KV_DIM), lambda b, i: (b, 0, 0)),
        ],
        out_specs=pl.BlockSpec((ATTN_BLOCK, D_MODEL), lambda b, i: (b * nqb + i, 0)),
        out_shape=jax.ShapeDtypeStruct((tok.dec_batch * tok.dec_seq, D_MODEL), F32),
        compiler_params=_cparams("parallel", "arbitrary"),
        name="attn_latent",
    )(sink, qkv, qkv, qkv, qkv, qkv, tables, tables, tables, k_ctx, v_ctx)


ROUTE_TILE = 256


def _top16_ranks(s):
    n = s.shape[0]
    iota = lax.broadcasted_iota(jnp.int32, s.shape, 0).astype(F32)
    rank = jnp.full(s.shape, float(PEER_TOPK), F32)
    vals = []
    for k in range(PEER_TOPK):
        m = jnp.max(s, axis=0, keepdims=True)
        idx = jnp.min(jnp.where(s == m, iota, float(n)), axis=0, keepdims=True)
        hit = iota == idx
        rank = jnp.where(hit, float(k), rank)
        s = jnp.where(hit, -jnp.inf, s)
        vals.append(m)
    return rank, vals


def _peer_route_kernel(q_ref, keys_ref, r2_ref, w2_ref, n1_ref, w1_ref):
    q = q_ref[...].astype(BF16)
    k1 = keys_ref[0, 0].astype(BF16)
    k2 = keys_ref[1, 0].astype(BF16)
    s1 = lax.dot_general(k1, q[:, :PEER_HALF], _NT, preferred_element_type=F32)
    s2 = lax.dot_general(k2, q[:, PEER_HALF:], _NT, preferred_element_type=F32)
    r1, v1 = _top16_ranks(s1)
    r2, v2 = _top16_ranks(s2)
    v2_all = jnp.concatenate(v2, axis=0)
    cand = jnp.concatenate([v1[a] + v2_all for a in range(PEER_TOPK)], axis=0)
    rc, top = _top16_ranks(cand)
    picked = jnp.where(rc < float(PEER_TOPK), 1.0, 0.0)
    z = jnp.zeros_like(top[0])
    for k in range(PEER_TOPK):
        z = z + jnp.exp(top[k] - top[0])
    n1 = jnp.zeros(s1.shape, F32)
    for a in range(PEER_TOPK):
        cnt = jnp.sum(picked[a * PEER_TOPK:(a + 1) * PEER_TOPK], axis=0, keepdims=True)
        n1 = jnp.where(r1 == float(a), cnt, n1)
    r2_ref[0] = r2
    w2_ref[0] = jnp.exp(s2 - v2[0])
    n1_ref[0] = n1
    w1_ref[0] = jnp.exp(s1 - v1[0]) / z


def _peer_route(q, keys, n_tokens):
    shape_f = jax.ShapeDtypeStruct((PEER_HEADS, PEER_NKEYS, n_tokens), F32)
    out_spec = pl.BlockSpec((1, PEER_NKEYS, ROUTE_TILE), lambda i, h: (h, 0, i))
    return pl.pallas_call(
        _peer_route_kernel,
        grid=(n_tokens // ROUTE_TILE, PEER_HEADS),
        in_specs=[
            pl.BlockSpec((ROUTE_TILE, PEER_KEY_DIM), lambda i, h: (i, h)),
            pl.BlockSpec((2, 1, PEER_NKEYS, PEER_HALF), lambda i, h: (0, h, 0, 0)),
        ],
        out_specs=[out_spec, out_spec, out_spec, out_spec],
        out_shape=[shape_f, shape_f, shape_f, shape_f],
        compiler_params=_cparams("parallel", "arbitrary"),
        name="peer_route",
    )(q, keys)


def _peer_dense_kernel(x_ref, m_ref, u_ref, vt_ref, r2_ref, w2_ref, n1_ref, w1_ref, o_ref, h_scr, acc_scr):
    c = pl.program_id(1)

    @pl.when(c == 0)
    def _():
        m = m_ref[0]
        h_scr[...] = (x_ref[...] * (1.0 + m[4:5]) + m[3:4]).astype(BF16)
        acc_scr[...] = jnp.zeros_like(acc_scr)

    s = lax.dot_general(u_ref[...], h_scr[...], _NT, preferred_element_type=F32)
    act = _gelu(s)
    parts = []
    for cc in range(EXPERT_BLOCK // PEER_NKEYS):
        key1 = c * (EXPERT_BLOCK // PEER_NKEYS) + cc
        gate = jnp.zeros((PEER_NKEYS, TOKEN_TILE), F32)
        for h in range(PEER_HEADS):
            n1 = n1_ref[h, pl.ds(key1, 1), :]
            w1 = w1_ref[h, pl.ds(key1, 1), :]
            gate = gate + jnp.where(r2_ref[h] < n1, w2_ref[h], 0.0) * w1
        parts.append((act[cc * PEER_NKEYS:(cc + 1) * PEER_NKEYS] * gate).astype(BF16))
    a = jnp.concatenate(parts, axis=0)
    acc_scr[...] += jnp.dot(vt_ref[...], a, preferred_element_type=F32)

    @pl.when(c == pl.num_programs(1) - 1)
    def _():
        o_ref[...] = acc_scr[...].T


def _peer_dense(tok, x, mods, u_tab, vt_tab, route):
    r2, w2, n1, w1 = route
    route_spec = pl.BlockSpec((PEER_HEADS, PEER_NKEYS, TOKEN_TILE), lambda i, c: (0, 0, i))
    return pl.pallas_call(
        _peer_dense_kernel,
        grid=(tok.tiles, PEER_EXPERTS // EXPERT_BLOCK),
        in_specs=[
            pl.BlockSpec((TOKEN_TILE, D_MODEL), lambda i, c: (i, 0)),
            pl.BlockSpec((1, 6, D_MODEL), lambda i, c: (tok.mod_row(i), 0, 0)),
            pl.BlockSpec((EXPERT_BLOCK, D_MODEL), lambda i, c: (c, 0)),
            pl.BlockSpec((D_MODEL, EXPERT_BLOCK), lambda i, c: (0, c)),
            route_spec, route_spec, route_spec, route_spec,
        ],
        out_specs=pl.BlockSpec((TOKEN_TILE, D_MODEL), lambda i, c: (i, 0)),
        out_shape=jax.ShapeDtypeStruct((tok.total, D_MODEL), F32),
        scratch_shapes=[pltpu.VMEM((TOKEN_TILE, D_MODEL), BF16), pltpu.VMEM((D_MODEL, TOKEN_TILE), F32)],
        compiler_params=_cparams("parallel", "arbitrary"),
        name="peer_dense",
    )(x, mods, u_tab, vt_tab, r2, w2, n1, w1)


def kernel(x_prompt, x_sample, state_ssm_re, state_ssm_im, cache_k, cache_v, c, c_ctx, w_mod, b_mod, ln_g, ln_b, ssm_w_in, ssm_a_re, ssm_a_im, ssm_log_dt, ssm_b_re, ssm_b_im, ssm_c_re, ssm_c_im, ssm_d, ssm_w_glu, ssm_w_out, attn_w_qkv, attn_sink, attn_w_out, peer_w_q, peer_keys, peer_u, peer_v):
    depth = w_mod.shape[0]
    n_batch, seq, _ = x_prompt.shape
    dec_batch, dec_seq, _ = x_sample.shape
    tok = _Tokens(n_batch * seq, dec_batch, dec_seq)
    alpha = (2 * depth) ** 0.25

    cond = jnp.concatenate([c, c_ctx[None, :],
                            jnp.zeros((MOD_ROWS - dec_batch - 1, D_MODEL), F32)], axis=0)
    mods_all = _modulation_all(cond, w_mod, b_mod)

    x = jnp.concatenate([x_prompt.reshape(-1, D_MODEL), x_sample.reshape(-1, D_MODEL)], axis=0)
    st_re, st_im, st_k, st_v = [], [], [], []
    for i in range(depth):
        j = i // 2
        mods = mods_all[i]
        if i % 2 == 0:
            u = _mod_linear(tok, x, mods, ssm_w_in[j].astype(BF16), 0, "s5_in")
            ops = _s5_operators(ssm_a_re[j], ssm_a_im[j], ssm_log_dt[j], ssm_b_re[j], ssm_b_im[j],
                                ssm_c_re[j], ssm_c_im[j])
            zeros = jnp.zeros((SSM_GROUPS, n_batch, 4 * SSM_STATE), F32)
            y_c, s_c = _s5_core(_to_chunks(u[:tok.n_ctx], n_batch, seq), zeros, ops,
                                seq // SSM_CHUNK, n_batch, "s5_core_context")
            h0 = _state_to_cols(state_ssm_re[:, j], state_ssm_im[:, j])
            y_s, _ = _s5_core(_to_chunks(u[tok.n_ctx:], dec_batch, dec_seq), h0, ops,
                              dec_seq // SSM_CHUNK, dec_batch, "s5_core_latent")
            y = jnp.concatenate([_from_chunks(y_c, n_batch, seq), _from_chunks(y_s, dec_batch, dec_seq)], axis=0)
            s_re, s_im = _cols_to_state(s_c)
            st_re.append(s_re)
            st_im.append(s_im)
            x = _s5_tail(tok, y, u, x, mods, ssm_d[j], ssm_w_glu[j].astype(BF16), ssm_w_out[j].astype(BF16),
                         ln_g[i, 0], ln_b[i, 0], alpha)
        else:
            qkv = _mod_linear(tok, x, mods, _attn_weights(attn_w_qkv[j]), 0, "attn_qkv")
            o_c = _attn_context(qkv, attn_sink[j], n_batch, seq)
            past = cache_k.shape[2]
            o_s = _attn_latent(qkv, _rope_tables(dec_seq), cache_k[:, j].reshape(dec_batch, past, KV_DIM),
                               cache_v[:, j].reshape(dec_batch, past, KV_DIM), attn_sink[j], tok)
            kv_c = qkv[:tok.n_ctx, 2 * D_MODEL:2 * D_MODEL + 2 * KV_DIM]
            st_k.append(kv_c[:, :KV_DIM].reshape(n_batch, seq, N_KV_HEADS, HEAD_DIM))
            st_v.append(kv_c[:, KV_DIM:].reshape(n_batch, seq, N_KV_HEADS, HEAD_DIM))
            x = _mm_postnorm(tok, jnp.concatenate([o_c, o_s], axis=0), x, mods, attn_w_out[j].astype(BF16),
                             ln_g[i, 0], ln_b[i, 0], 2, alpha, "attn_out")
        q = _mod_linear(tok, x, mods, peer_w_q[i].astype(BF16), 3, "peer_query")
        route = _peer_route(q, peer_keys[i], tok.total)
        o = _peer_dense(tok, x, mods, peer_u[i].astype(BF16), peer_v[i].T.astype(BF16), route)
        x = _postnorm(tok, o, x, mods, ln_g[i, 1], ln_b[i, 1], 5, alpha, "peer_norm")

    y_prompt = x[:tok.n_ctx].reshape(n_batch, seq, D_MODEL)
    y_sample = x[tok.n_ctx:].reshape(dec_batch, dec_seq, D_MODEL)
    return (y_prompt, y_sample, jnp.stack(st_re, axis=1), jnp.stack(st_im, axis=1),
            jnp.stack(st_k, axis=1), jnp.stack(st_v, axis=1))
```

```python
import functools
import math

import jax
import jax.numpy as jnp
from jax import lax
from jax.experimental import pallas as pl
from jax.experimental.pallas import tpu as pltpu

F32 = jnp.float32
BF16 = jnp.bfloat16

D_MODEL = 1024
GRID_W = 64
SSM_GROUP = 16
SSM_GROUPS = D_MODEL // SSM_GROUP
SSM_STATE = 64
SSM_CHUNK = 16
SSM_ROW = SSM_CHUNK * SSM_GROUP
HEAD_DIM = 64
N_HEADS = D_MODEL // HEAD_DIM
N_KV_HEADS = 4
KV_REP = N_HEADS // N_KV_HEADS
KV_DIM = N_KV_HEADS * HEAD_DIM
WINDOW = 128
ATTN_BLOCK = 128
ROPE_BASE = 10000.0
ROPE_AXIS_DIM = HEAD_DIM // 2
NEG_INF = -1e30
PEER_HEADS = 8
PEER_NKEYS = 128
PEER_EXPERTS = PEER_NKEYS * PEER_NKEYS
PEER_TOPK = 16
PEER_KEY_DIM = 256
PEER_HALF = PEER_KEY_DIM // 2
LN_EPS = 1e-5

TOKEN_TILE = 512
EXPERT_BLOCK = 1024
SUB_EXPERTS = 256
MOD_ROWS = 16
BF16_SUBLANES = 16
VMEM_LIMIT = 56 * 1024 * 1024

_NT = (((1,), (1,)), ((), ()))


def _cparams(*sem):
    return pltpu.CompilerParams(dimension_semantics=sem, vmem_limit_bytes=VMEM_LIMIT)


def _gelu(x):
    return 0.5 * x * (1.0 + jnp.tanh(0.7978845608028654 * (x + 0.044715 * (x * x * x))))


def _layer_norm_rows(z, g, b):
    mu = jnp.mean(z, axis=-1, keepdims=True)
    zc = z - mu
    var = jnp.mean(zc * zc, axis=-1, keepdims=True)
    return zc * lax.rsqrt(var + LN_EPS) * g + b


def _mod_kernel(c_ref, w_ref, b_ref, o_ref):
    c = c_ref[...]
    s = c * jax.nn.sigmoid(c)
    o_ref[...] = jnp.dot(s.astype(BF16), w_ref[...].astype(BF16),
                         preferred_element_type=F32) + b_ref[...]


def _modulation_all(cond, w_mod, b_mod):
    depth = w_mod.shape[0]
    n_out = w_mod.shape[2]
    nb = n_out // D_MODEL
    out = pl.pallas_call(
        _mod_kernel,
        grid=(depth, nb),
        in_specs=[
            pl.BlockSpec((MOD_ROWS, D_MODEL), lambda i, j: (0, 0)),
            pl.BlockSpec((None, D_MODEL, D_MODEL), lambda i, j: (i, 0, j)),
            pl.BlockSpec((None, 1, D_MODEL), lambda i, j: (i, 0, j)),
        ],
        out_specs=pl.BlockSpec((None, MOD_ROWS, D_MODEL), lambda i, j: (i, 0, j)),
        out_shape=jax.ShapeDtypeStruct((depth, MOD_ROWS, n_out), F32),
        compiler_params=_cparams("arbitrary", "arbitrary"),
        name="modulation",
    )(cond, w_mod, b_mod.reshape(depth, 1, n_out))
    return out.reshape(depth, MOD_ROWS, nb, D_MODEL)


class _Tokens:
    def __init__(self, n_ctx_tokens, dec_batch, dec_seq):
        self.n_ctx = n_ctx_tokens
        self.dec_batch = dec_batch
        self.dec_seq = dec_seq
        self.total = n_ctx_tokens + dec_batch * dec_seq
        assert n_ctx_tokens % TOKEN_TILE == 0 and dec_seq % TOKEN_TILE == 0
        self.ctx_tiles = n_ctx_tokens // TOKEN_TILE
        self.tiles = self.total // TOKEN_TILE
        self.tiles_per_seq = dec_seq // TOKEN_TILE

    def mod_row(self, i):
        return jnp.where(i < self.ctx_tiles, self.dec_batch, (i - self.ctx_tiles) // self.tiles_per_seq)


def _mod_spec(tok):
    return pl.BlockSpec((1, 6, D_MODEL), lambda i: (tok.mod_row(i), 0, 0))


def _row_spec(width):
    return pl.BlockSpec((TOKEN_TILE, width), lambda i: (i, 0))


def _full_spec(shape):
    nd = len(shape)
    return pl.BlockSpec(shape, lambda i: (0,) * nd)


def _linear_kernel(x_ref, m_ref, w_ref, o_ref, *, shift_idx):
    m = m_ref[0]
    h = x_ref[...] * (1.0 + m[shift_idx + 1:shift_idx + 2]) + m[shift_idx:shift_idx + 1]
    o_ref[...] = jnp.dot(h.astype(BF16), w_ref[...], preferred_element_type=F32)


def _mod_linear(tok, x, mods, w, shift_idx, name):
    n = w.shape[1]
    return pl.pallas_call(
        functools.partial(_linear_kernel, shift_idx=shift_idx),
        grid=(tok.tiles,),
        in_specs=[_row_spec(D_MODEL), _mod_spec(tok), _full_spec((D_MODEL, n))],
        out_specs=_row_spec(n),
        out_shape=jax.ShapeDtypeStruct((tok.total, n), F32),
        compiler_params=_cparams("parallel"),
        name=name,
    )(x, mods, w)


def _post_norm_rows(x, out, m, gate_idx, g_ref, b_ref, alpha):
    z = alpha * x + (1.0 + m[gate_idx:gate_idx + 1]) * out
    return _layer_norm_rows(z, g_ref[...], b_ref[...])


def _mm_postnorm_kernel(a_ref, x_ref, m_ref, w_ref, g_ref, b_ref, o_ref, *, gate_idx, alpha):
    y = jnp.dot(a_ref[...].astype(BF16), w_ref[...], preferred_element_type=F32)
    o_ref[...] = _post_norm_rows(x_ref[...], y, m_ref[0], gate_idx, g_ref, b_ref, alpha)


def _mm_postnorm(tok, a, x, mods, w, g, b, gate_idx, alpha, name):
    return pl.pallas_call(
        functools.partial(_mm_postnorm_kernel, gate_idx=gate_idx, alpha=alpha),
        grid=(tok.tiles,),
        in_specs=[_row_spec(D_MODEL), _row_spec(D_MODEL), _mod_spec(tok),
                  _full_spec((D_MODEL, D_MODEL)), _full_spec((1, D_MODEL)), _full_spec((1, D_MODEL))],
        out_specs=_row_spec(D_MODEL),
        out_shape=jax.ShapeDtypeStruct((tok.total, D_MODEL), F32),
        compiler_params=_cparams("parallel"),
        name=name,
    )(a, x, mods, w, g.reshape(1, D_MODEL), b.reshape(1, D_MODEL))


def _postnorm_kernel(o_in_ref, x_ref, m_ref, g_ref, b_ref, o_ref, *, gate_idx, alpha):
    o_ref[...] = _post_norm_rows(x_ref[...], o_in_ref[...], m_ref[0], gate_idx, g_ref, b_ref, alpha)


def _postnorm(tok, out, x, mods, g, b, gate_idx, alpha, name):
    return pl.pallas_call(
        functools.partial(_postnorm_kernel, gate_idx=gate_idx, alpha=alpha),
        grid=(tok.tiles,),
        in_specs=[_row_spec(D_MODEL), _row_spec(D_MODEL), _mod_spec(tok),
                  _full_spec((1, D_MODEL)), _full_spec((1, D_MODEL))],
        out_specs=_row_spec(D_MODEL),
        out_shape=jax.ShapeDtypeStruct((tok.total, D_MODEL), F32),
        compiler_params=_cparams("parallel"),
        name=name,
    )(out, x, mods, g.reshape(1, D_MODEL), b.reshape(1, D_MODEL))


def _s5_operators(a_re, a_im, log_dt, b_re, b_im, c_re, c_im):
    hi = lax.Precision.HIGHEST
    dt = jnp.exp(log_dt)[..., None]
    ar, ai = a_re * dt, a_im * dt
    k = jnp.arange(SSM_CHUNK + 1, dtype=F32)[:, None, None, None]
    mag = jnp.exp(k * ar[None])
    pw_re, pw_im = mag * jnp.cos(k * ai[None]), mag * jnp.sin(k * ai[None])
    l_re, l_im = pw_re[1], pw_im[1]
    den = a_re * a_re + a_im * a_im
    q_re = ((l_re - 1.0) * a_re + l_im * a_im) / den
    q_im = (l_im * a_re - (l_re - 1.0) * a_im) / den
    bt_re = q_re[..., None] * b_re - q_im[..., None] * b_im
    bt_im = q_re[..., None] * b_im + q_im[..., None] * b_re
    d_re = pw_re[..., None] * bt_re[None] - pw_im[..., None] * bt_im[None]
    d_im = pw_re[..., None] * bt_im[None] + pw_im[..., None] * bt_re[None]
    pw_re_c, pw_im_c = pw_re[:, :, :, None, :], pw_im[:, :, :, None, :]
    e_re = c_re[None] * pw_re_c - c_im[None] * pw_im_c
    e_im = c_re[None] * pw_im_c + c_im[None] * pw_re_c
    kern = (jnp.einsum('dgip,kdgpj->kdgij', c_re, d_re, precision=hi)
            - jnp.einsum('dgip,kdgpj->kdgij', c_im, d_im, precision=hi))
    s = jnp.arange(SSM_CHUNK)
    lag = s[None, :] - s[:, None]
    kf = kern[jnp.clip(lag, 0, SSM_CHUNK), 0]
    kb = kern[jnp.clip(-lag, 0, SSM_CHUNK), 1]
    mix = (jnp.where((lag >= 0)[:, :, None, None, None], kf, 0.0)
           + jnp.where((lag <= 0)[:, :, None, None, None], kb, 0.0))
    g = SSM_GROUPS
    m_op = jnp.transpose(mix, (2, 0, 4, 1, 3)).reshape(g, SSM_ROW, SSM_ROW)
    pf = (SSM_CHUNK - 1) - s
    win = jnp.stack([d_re[pf, 0], d_re[s, 1], d_im[pf, 0], d_im[s, 1]], axis=0)
    win_op = jnp.transpose(win, (2, 1, 4, 0, 3)).reshape(g, SSM_ROW, 4 * SSM_STATE)
    wout = jnp.stack([e_re[s + 1, 0], e_re[SSM_CHUNK - s, 1], -e_im[s + 1, 0], -e_im[SSM_CHUNK - s, 1]],
                     axis=0)
    wout_op = jnp.transpose(wout, (2, 0, 4, 1, 3)).reshape(g, 4 * SSM_STATE, SSM_ROW)
    lam = jnp.stack([jnp.concatenate([pw_re[SSM_CHUNK, 0], pw_re[SSM_CHUNK, 1]], axis=-1),
                     jnp.concatenate([pw_im[SSM_CHUNK, 0], pw_im[SSM_CHUNK, 1]], axis=-1)], axis=1)
    return m_op, win_op, wout_op, lam


def _s5_core_kernel(u_ref, h0_ref, m_ref, win_ref, wout_ref, lam_ref, y_ref, s_ref, z_scr, p_scr,
                    *, n_chunks, batch):
    hi = lax.Precision.HIGHEST
    p2 = 2 * SSM_STATE
    u = u_ref[...]
    z_scr[...] = jnp.dot(u, win_ref[...], preferred_element_type=F32, precision=hi)
    l_re = lam_ref[0:1, :]
    l_im = lam_ref[1:2, :]
    is_fwd = lax.broadcasted_iota(jnp.int32, (batch, p2), 1) < SSM_STATE

    def step(k, carry):
        s_re, s_im = carry
        rf = pl.multiple_of(k * batch, batch)
        rb = pl.multiple_of((n_chunks - 1 - k) * batch, batch)
        zf = z_scr[pl.ds(rf, batch), :]
        zb = z_scr[pl.ds(rb, batch), :]
        z_re = jnp.where(is_fwd, zf[:, :p2], zb[:, :p2])
        z_im = jnp.where(is_fwd, zf[:, p2:], zb[:, p2:])
        p_scr[pl.ds(rf, batch), 0:SSM_STATE] = s_re[:, :SSM_STATE]
        p_scr[pl.ds(rf, batch), p2:p2 + SSM_STATE] = s_im[:, :SSM_STATE]
        p_scr[pl.ds(rb, batch), SSM_STATE:p2] = s_re[:, SSM_STATE:]
        p_scr[pl.ds(rb, batch), p2 + SSM_STATE:] = s_im[:, SSM_STATE:]
        n_re = l_re * s_re - l_im * s_im + z_re
        n_im = l_re * s_im + l_im * s_re + z_im
        return n_re, n_im

    h0 = h0_ref[...]
    s_re, s_im = lax.fori_loop(0, n_chunks, step, (h0[:, :p2], h0[:, p2:]))
    s_ref[:, :p2] = s_re
    s_ref[:, p2:] = s_im
    y_ref[...] = (jnp.dot(u, m_ref[...], preferred_element_type=F32, precision=hi)
                  + jnp.dot(p_scr[...], wout_ref[...], preferred_element_type=F32, precision=hi))


def _s5_core(ug, h0, ops, n_chunks, batch, name):
    m_op, win_op, wout_op, lam = ops
    g, rows, _ = ug.shape
    st = 4 * SSM_STATE
    return pl.pallas_call(
        functools.partial(_s5_core_kernel, n_chunks=n_chunks, batch=batch),
        grid=(g,),
        in_specs=[
            pl.BlockSpec((None, rows, SSM_ROW), lambda i: (i, 0, 0)),
            pl.BlockSpec((None, batch, st), lambda i: (i, 0, 0)),
            pl.BlockSpec((None, SSM_ROW, SSM_ROW), lambda i: (i, 0, 0)),
            pl.BlockSpec((None, SSM_ROW, st), lambda i: (i, 0, 0)),
            pl.BlockSpec((None, st, SSM_ROW), lambda i: (i, 0, 0)),
            pl.BlockSpec((None, 2, 2 * SSM_STATE), lambda i: (i, 0, 0)),
        ],
        out_specs=[
            pl.BlockSpec((None, rows, SSM_ROW), lambda i: (i, 0, 0)),
            pl.BlockSpec((None, batch, st), lambda i: (i, 0, 0)),
        ],
        out_shape=[jax.ShapeDtypeStruct((g, rows, SSM_ROW), F32),
                   jax.ShapeDtypeStruct((g, batch, st), F32)],
        scratch_shapes=[pltpu.VMEM((rows, st), F32), pltpu.VMEM((rows, st), F32)],
        compiler_params=_cparams("parallel"),
        name=name,
    )(ug, h0, m_op, win_op, wout_op, lam)


def _to_chunks(u, batch, seq):
    n_chunks = seq // SSM_CHUNK
    t = u.reshape(batch, n_chunks, SSM_CHUNK, SSM_GROUPS, SSM_GROUP)
    return jnp.transpose(t, (3, 1, 0, 2, 4)).reshape(SSM_GROUPS, n_chunks * batch, SSM_ROW)


def _from_chunks(y, batch, seq):
    n_chunks = seq // SSM_CHUNK
    t = y.reshape(SSM_GROUPS, n_chunks, batch, SSM_CHUNK, SSM_GROUP)
    return jnp.transpose(t, (2, 1, 3, 0, 4)).reshape(batch * seq, D_MODEL)


def _state_to_cols(h_re, h_im):
    cols = jnp.concatenate([h_re[:, 0], h_re[:, 1], h_im[:, 0], h_im[:, 1]], axis=-1)
    return jnp.transpose(cols, (1, 0, 2))


def _cols_to_state(s):
    t = jnp.transpose(s, (1, 0, 2)).reshape(s.shape[1], SSM_GROUPS, 2, 2, SSM_STATE)
    t = jnp.transpose(t, (2, 0, 3, 1, 4))
    return t[0], t[1]


def _s5_tail_kernel(y_ref, u_ref, x_ref, m_ref, d_ref, wg_ref, wo_ref, g_ref, b_ref, o_ref, *, alpha):
    v = _gelu(y_ref[...] + d_ref[...] * u_ref[...])
    gl = jnp.dot(v.astype(BF16), wg_ref[...], preferred_element_type=F32)
    a = gl[:, :D_MODEL] * jax.nn.sigmoid(gl[:, D_MODEL:])
    out = jnp.dot(a.astype(BF16), wo_ref[...], preferred_element_type=F32)
    o_ref[...] = _post_norm_rows(x_ref[...], out, m_ref[0], 2, g_ref, b_ref, alpha)


def _s5_tail(tok, y, u, x, mods, d, w_glu, w_out, g, b, alpha):
    return pl.pallas_call(
        functools.partial(_s5_tail_kernel, alpha=alpha),
        grid=(tok.tiles,),
        in_specs=[_row_spec(D_MODEL), _row_spec(D_MODEL), _row_spec(D_MODEL), _mod_spec(tok),
                  _full_spec((1, D_MODEL)), _full_spec((D_MODEL, 2 * D_MODEL)),
                  _full_spec((D_MODEL, D_MODEL)), _full_spec((1, D_MODEL)), _full_spec((1, D_MODEL))],
        out_specs=_row_spec(D_MODEL),
        out_shape=jax.ShapeDtypeStruct((tok.total, D_MODEL), F32),
        compiler_params=_cparams("parallel"),
        name="s5_tail",
    )(y, u, x, mods, d.reshape(1, D_MODEL), w_glu, w_out, g.reshape(1, D_MODEL), b.reshape(1, D_MODEL))


QKV_COLS = 3 * D_MODEL


def _attn_weights(w_qkv):
    wq, wk, wv = w_qkv[:, :D_MODEL], w_qkv[:, D_MODEL:D_MODEL + KV_DIM], w_qkv[:, D_MODEL + KV_DIM:]
    half = ROPE_AXIS_DIM // 2

    def partner(w):
        n = w.shape[1]
        t = w.reshape(D_MODEL, n // ROPE_AXIS_DIM, 2, half)
        return jnp.stack([-t[:, :, 1], t[:, :, 0]], axis=2).reshape(D_MODEL, n)

    pad = jnp.zeros((D_MODEL, QKV_COLS - 2 * D_MODEL - 3 * KV_DIM), w_qkv.dtype)
    return jnp.concatenate([wq, partner(wq), wk, wv, partner(wk), pad], axis=1).astype(BF16)


def _rope_tables(seq):
    half = ROPE_AXIS_DIM // 2
    rows = seq // GRID_W
    row = jnp.repeat(jnp.arange(rows, dtype=F32), GRID_W)
    col = jnp.tile(jnp.arange(GRID_W, dtype=F32), rows)
    inv = ROPE_BASE ** (-jnp.arange(half, dtype=F32) / half)
    ang_r = row[:, None] * inv[None, :]
    ang_c = col[:, None] * inv[None, :]
    ang = jnp.concatenate([ang_r, ang_r, ang_c, ang_c], axis=-1)
    cos = jnp.tile(jnp.cos(ang), (1, N_KV_HEADS))
    sin = jnp.tile(jnp.sin(ang), (1, N_KV_HEADS))
    return jnp.concatenate([cos, sin], axis=-1)


def _softmax_heads(q, k_list, v_list, mask_list, sink_ref, o_ref):
    scale = HEAD_DIM ** -0.5
    for h in range(N_HEADS):
        g = h // KV_REP
        qh = q[:, h * HEAD_DIM:(h + 1) * HEAD_DIM].astype(BF16)
        sink = sink_ref[h]
        logits = []
        for k, mask in zip(k_list, mask_list):
            s = lax.dot_general(qh, k[:, g * HEAD_DIM:(g + 1) * HEAD_DIM], _NT,
                                preferred_element_type=F32) * scale
            if mask is not None:
                s = jnp.where(mask, s, NEG_INF)
            logits.append(s)
        m = jnp.full((q.shape[0], 1), sink, F32)
        for s in logits:
            m = jnp.maximum(m, jnp.max(s, axis=-1, keepdims=True))
        denom = jnp.exp(sink - m)
        acc = jnp.zeros((q.shape[0], HEAD_DIM), F32)
        for s, v in zip(logits, v_list):
            p = jnp.exp(s - m)
            denom = denom + jnp.sum(p, axis=-1, keepdims=True)
            acc = acc + jnp.dot(p.astype(BF16), v[:, g * HEAD_DIM:(g + 1) * HEAD_DIM],
                                preferred_element_type=F32)
        o_ref[:, h * HEAD_DIM:(h + 1) * HEAD_DIM] = acc / denom


def _attn_ctx_kernel(sink_ref, q_ref, kv_ref, o_ref):
    kv = kv_ref[...]
    k = kv[:, :KV_DIM].astype(BF16)
    v = kv[:, KV_DIM:2 * KV_DIM].astype(BF16)
    _softmax_heads(q_ref[...], [k], [v], [None], sink_ref, o_ref)


def _attn_context(qkv, sink, n_batch, seq):
    return pl.pallas_call(
        _attn_ctx_kernel,
        grid=(n_batch,),
        in_specs=[
            pl.BlockSpec(memory_space=pltpu.SMEM),
            pl.BlockSpec((seq, D_MODEL), lambda b: (b, 0)),
            pl.BlockSpec((seq, D_MODEL), lambda b: (b, 2)),
        ],
        out_specs=pl.BlockSpec((seq, D_MODEL), lambda b: (b, 0)),
        out_shape=jax.ShapeDtypeStruct((n_batch * seq, D_MODEL), F32),
        compiler_params=_cparams("parallel"),
        name="attn_context",
    )(sink, qkv, qkv)


def _attn_lat_kernel(sink_ref, q_ref, qp_ref, kv0_ref, kv1_ref, kv2_ref, t0_ref, t1_ref, t2_ref,
                     kc_ref, vc_ref, o_ref, *, n_qblocks):
    qb = pl.program_id(1)
    t1 = t1_ref[...]
    cos_q = jnp.concatenate([t1[:, :KV_DIM]] * KV_REP, axis=-1)
    sin_q = jnp.concatenate([t1[:, KV_DIM:]] * KV_REP, axis=-1)
    q = q_ref[...] * cos_q + qp_ref[...] * sin_q

    def rope_k(kv_ref, t_ref):
        kv = kv_ref[...]
        t = t_ref[...]
        k = kv[:, :KV_DIM] * t[:, :KV_DIM] + kv[:, 2 * KV_DIM:3 * KV_DIM] * t[:, KV_DIM:]
        return k.astype(BF16), kv[:, KV_DIM:2 * KV_DIM].astype(BF16)

    k0, v0 = rope_k(kv0_ref, t0_ref)
    k1, v1 = rope_k(kv1_ref, t1_ref)
    k2, v2 = rope_k(kv2_ref, t2_ref)
    k_loc = jnp.concatenate([k0, k1, k2], axis=0)
    v_loc = jnp.concatenate([v0, v1, v2], axis=0)
    span = 3 * ATTN_BLOCK
    qi = lax.broadcasted_iota(jnp.int32, (ATTN_BLOCK, span), 0)
    kj = lax.broadcasted_iota(jnp.int32, (ATTN_BLOCK, span), 1) - ATTN_BLOCK
    ok = (jnp.abs(kj - qi) <= WINDOW)
    ok = ok & ((kj >= 0) | (qb > 0)) & ((kj < ATTN_BLOCK) | (qb < n_qblocks - 1))
    kc = kc_ref[0].astype(BF16)
    vc = vc_ref[0].astype(BF16)
    _softmax_heads(q, [kc, k_loc], [vc, v_loc], [None, ok], sink_ref, o_ref)


def _attn_latent(qkv, tables, k_ctx, v_ctx, sink, tok):
    nqb = tok.dec_seq // ATTN_BLOCK
    base = tok.n_ctx // ATTN_BLOCK
    past = k_ctx.shape[1]

    def rows(b, i, off):
        return base + b * nqb + jnp.clip(i + off, 0, nqb - 1)

    def kv_spec(off):
        return pl.BlockSpec((ATTN_BLOCK, D_MODEL), lambda b, i: (rows(b, i, off), 2))

    def tab_spec(off):
        return pl.BlockSpec((ATTN_BLOCK, 2 * KV_DIM), lambda b, i: (jnp.clip(i + off, 0, nqb - 1), 0))

    return pl.pallas_call(
        functools.partial(_attn_lat_kernel, n_qblocks=nqb),
        grid=(tok.dec_batch, nqb),
        in_specs=[
            pl.BlockSpec(memory_space=pltpu.SMEM),
            pl.BlockSpec((ATTN_BLOCK, D_MODEL), lambda b, i: (rows(b, i, 0), 0)),
            pl.BlockSpec((ATTN_BLOCK, D_MODEL), lambda b, i: (rows(b, i, 0), 1)),
            kv_spec(-1), kv_spec(0), kv_spec(1),
            tab_spec(-1), tab_spec(0), tab_spec(1),
            pl.BlockSpec((1, past, KV_DIM), lambda b, i: (b, 0, 0)),
            pl.BlockSpec((1, past, KV_DIM), lambda b, i: (b, 0, 0)),
        ],
        out_specs=pl.BlockSpec((ATTN_BLOCK, D_MODEL), lambda b, i: (b * nqb + i, 0)),
        out_shape=jax.ShapeDtypeStruct((tok.dec_batch * tok.dec_seq, D_MODEL), F32),
        compiler_params=_cparams("parallel", "arbitrary"),
        name="attn_latent",
    )(sink, qkv, qkv, qkv, qkv, qkv, tables, tables, tables, k_ctx, v_ctx)


ROUTE_TILE = 256
ROUTE_LANES = 128
NKH = PEER_NKEYS * PEER_HEADS


def _oddeven_merge_sort_pairs(n):
    pairs = []

    def merge(lo, hi, r):
        step = r * 2
        if step < hi - lo:
            merge(lo, hi, step)
            merge(lo + r, hi, step)
            pairs.extend((i, i + r) for i in range(lo + r, hi - r, step))
        else:
            pairs.append((lo, lo + r))

    def sort(lo, hi):
        if hi - lo >= 1:
            mid = lo + (hi - lo) // 2
            sort(lo, mid)
            sort(mid + 1, hi)
            merge(lo, hi, 1)

    sort(0, n - 1)
    return pairs


_SORT16 = _oddeven_merge_sort_pairs(PEER_TOPK)


def _exchange(x, i, j):
    x[i], x[j] = jnp.maximum(x[i], x[j]), jnp.minimum(x[i], x[j])


def _top16_sorted(vals):
    groups = []
    for g in range(0, len(vals), PEER_TOPK):
        x = list(vals[g:g + PEER_TOPK])
        for i, j in _SORT16:
            _exchange(x, i, j)
        groups.append(x)
    while len(groups) > 1:
        merged = []
        for a, b in zip(groups[0::2], groups[1::2]):
            x = [jnp.maximum(a[i], b[PEER_TOPK - 1 - i]) for i in range(PEER_TOPK)]
            for d in (8, 4, 2, 1):
                for i in range(PEER_TOPK):
                    if not i & d:
                        _exchange(x, i, i + d)
            merged.append(x)
        groups = merged
    return groups[0]


def _rank_bits(s, v):
    b3 = v[7] > s
    b2 = jnp.where(b3, v[11], v[3]) > s
    b1 = jnp.where(b3, jnp.where(b2, v[13], v[9]), jnp.where(b2, v[5], v[1])) > s
    hi = jnp.where(b2, jnp.where(b1, v[14], v[12]), jnp.where(b1, v[10], v[8]))
    lo = jnp.where(b2, jnp.where(b1, v[6], v[4]), jnp.where(b1, v[2], v[0]))
    b0 = jnp.where(b3, hi, lo) > s
    return (b3, b2, b1, b0), v[15] > s


def _select16(bits, leaves):
    b3, b2, b1, b0 = bits
    lvl = [jnp.where(b0, leaves[2 * i + 1], leaves[2 * i]) for i in range(8)]
    lvl = [jnp.where(b1, lvl[2 * i + 1], lvl[2 * i]) for i in range(4)]
    lvl = [jnp.where(b2, lvl[2 * i + 1], lvl[2 * i]) for i in range(2)]
    return jnp.where(b3, lvl[1], lvl[0])


def _exact_top16(s_ref, rank_ref, top_ref):
    n = s_ref.shape[0]
    rank_ref[...] = jnp.full(rank_ref.shape, float(PEER_TOPK), F32)

    def body(k, carry):
        s = s_ref[...]
        iota = lax.broadcasted_iota(jnp.int32, s.shape, 0).astype(F32)
        m = jnp.max(s, axis=0)
        idx = jnp.min(jnp.where(s == m[None], iota, float(n)), axis=0)
        hit = iota == idx[None]
        rank_ref[...] = jnp.where(hit, k.astype(F32), rank_ref[...])
        s_ref[...] = jnp.where(hit, -jnp.inf, s)
        top_ref[k] = m
        return carry

    lax.fori_loop(0, PEER_TOPK, body, 0)


def _peer_route_kernel(q_ref, k1_ref, k2_ref, r2_ref, w2_ref, n1_ref, w1_ref,
                       r2_st, w2_st, n1_st, w1_st, sa_scr, ra_scr, sb_scr, rb_scr, ta_scr, tb_scr):
    q = q_ref[...].astype(BF16)
    half = PEER_HEADS * PEER_HALF
    s1_all = lax.dot_general(k1_ref[...], q[:, :half], _NT, preferred_element_type=F32)
    s2_all = lax.dot_general(k2_ref[...], q[:, half:], _NT, preferred_element_type=F32)
    slab = (PEER_HEADS, ROUTE_LANES)
    ninf = jnp.full(slab, -jnp.inf, F32)

    for j in range(ROUTE_TILE // ROUTE_LANES):
        lanes = slice(j * ROUTE_LANES, (j + 1) * ROUTE_LANES)
        s1 = [s1_all[k * PEER_HEADS:(k + 1) * PEER_HEADS, lanes] for k in range(PEER_NKEYS)]
        s2 = [s2_all[k * PEER_HEADS:(k + 1) * PEER_HEADS, lanes] for k in range(PEER_NKEYS)]
        v1 = _top16_sorted(s1)
        v2 = _top16_sorted(s2)
        cand = [[v1[a] + v2[b] for b in range(PEER_TOPK // (a + 1))] for a in range(PEER_TOPK)]
        flat = [c for row in cand for c in row]
        flat += [ninf] * (-len(flat) % PEER_TOPK)
        top = _top16_sorted(flat)
        theta = top[PEER_TOPK - 1]
        nb = []
        for row in cand:
            cnt = jnp.zeros(slab, F32)
            for c in row:
                cnt = cnt + jnp.where(c >= theta, 1.0, 0.0)
            nb.append(cnt)
        z = jnp.zeros(slab, F32)
        for k in range(PEER_TOPK):
            z = z + jnp.exp(top[k] - top[0])
        inv_z = 1.0 / z
        total = nb[0]
        for a in range(1, PEER_TOPK):
            total = total + nb[a]
        tied = total != float(PEER_TOPK)
        for k in range(PEER_TOPK - 1):
            tied = tied | (v1[k] == v1[k + 1]) | (v2[k] == v2[k + 1])
        in1 = jnp.zeros(slab, F32)
        in2 = jnp.zeros(slab, F32)
        for k in range(PEER_NKEYS):
            rows = slice(k * PEER_HEADS, (k + 1) * PEER_HEADS)
            bits, out1 = _rank_bits(s1[k], v1)
            n1_st[rows, :] = jnp.where(out1, 0.0, _select16(bits, nb))
            w1_st[rows, :] = jnp.exp(s1[k] - v1[0]) * inv_z
            in1 = in1 + jnp.where(out1, 0.0, 1.0)
            (b3, b2, b1, b0), out2 = _rank_bits(s2[k], v2)
            r = (jnp.where(b3, 8.0, 0.0) + jnp.where(b2, 4.0, 0.0)) + (jnp.where(b1, 2.0, 0.0) + jnp.where(b0, 1.0, 0.0))
            r2_st[rows, :] = jnp.where(out2, float(PEER_TOPK), r)
            w2_st[rows, :] = jnp.exp(s2[k] - v2[0])
            in2 = in2 + jnp.where(out2, 0.0, 1.0)
        tied = tied | (in1 != float(PEER_TOPK)) | (in2 != float(PEER_TOPK))
        n_tied = jnp.sum(jnp.where(tied, 1.0, 0.0))

        @pl.when(n_tied > 0.0)
        def _():
            shape3 = (PEER_NKEYS, PEER_HEADS, ROUTE_LANES)
            sa_scr[0:PEER_NKEYS] = s1_all[:, lanes].reshape(shape3)
            _exact_top16(sa_scr.at[0:PEER_NKEYS], ra_scr.at[0:PEER_NKEYS], ta_scr)
            sb_scr[0:PEER_NKEYS] = s2_all[:, lanes].reshape(shape3)
            _exact_top16(sb_scr.at[0:PEER_NKEYS], rb_scr.at[0:PEER_NKEYS], tb_scr)
            r1 = ra_scr[0:PEER_NKEYS]
            r2_st[...] = rb_scr[0:PEER_NKEYS].reshape(NKH, ROUTE_LANES)
            e1 = jnp.exp(s1_all[:, lanes].reshape(shape3) - ta_scr[0][None])
            w2_st[...] = jnp.exp(s2_all[:, lanes] - jnp.tile(tb_scr[0], (PEER_NKEYS, 1)))
            for a in range(PEER_TOPK):
                sa_scr[a * PEER_TOPK:(a + 1) * PEER_TOPK] = ta_scr[a][None] + tb_scr[...]
            _exact_top16(sa_scr, sb_scr, ta_scr)
            picked = jnp.where(sb_scr[...] < float(PEER_TOPK), 1.0, 0.0)
            zz = jnp.zeros(slab, F32)
            for k in range(PEER_TOPK):
                zz = zz + jnp.exp(ta_scr[k] - ta_scr[0])
            n1 = jnp.zeros(shape3, F32)
            for a in range(PEER_TOPK):
                cnt = jnp.sum(picked[a * PEER_TOPK:(a + 1) * PEER_TOPK], axis=0)
                n1 = jnp.where(r1 == float(a), cnt[None], n1)
            n1_st[...] = n1.reshape(NKH, ROUTE_LANES)
            w1_st[...] = (e1 / zz[None]).reshape(NKH, ROUTE_LANES)

        n1_ref[:, :, lanes] = n1_st[...].reshape(PEER_NKEYS, PEER_HEADS, ROUTE_LANES)
        w1_ref[:, :, lanes] = w1_st[...].reshape(PEER_NKEYS, PEER_HEADS, ROUTE_LANES)
        for h in range(PEER_HEADS):
            r2_ref[h, :, lanes] = r2_st[pl.ds(h, PEER_NKEYS, stride=PEER_HEADS), :].astype(BF16)
            w2_ref[h, :, lanes] = w2_st[pl.ds(h, PEER_NKEYS, stride=PEER_HEADS), :].astype(BF16)


def _route_weights(w_q, keys):
    w = w_q.reshape(D_MODEL, PEER_HEADS, 2, PEER_HALF)
    w = jnp.transpose(w, (0, 2, 1, 3)).reshape(D_MODEL, PEER_HEADS * PEER_KEY_DIM)
    eye = jnp.eye(PEER_HEADS, dtype=keys.dtype)
    big = jnp.einsum('phkd,hg->pkhgd', keys, eye).reshape(2, NKH, PEER_HEADS * PEER_HALF)
    return w.astype(BF16), big.astype(BF16)


def _peer_route(q, key_blocks, n_tokens):
    half = PEER_HEADS * PEER_HALF
    shape_kh = jax.ShapeDtypeStruct((PEER_NKEYS, PEER_HEADS, n_tokens), F32)
    shape_hk = jax.ShapeDtypeStruct((PEER_HEADS, PEER_NKEYS, n_tokens), BF16)
    spec_kh = pl.BlockSpec((PEER_NKEYS, PEER_HEADS, ROUTE_TILE), lambda i: (0, 0, i))
    spec_hk = pl.BlockSpec((PEER_HEADS, PEER_NKEYS, ROUTE_TILE), lambda i: (0, 0, i))
    stage = pltpu.VMEM((NKH, ROUTE_LANES), F32)
    cand3 = pltpu.VMEM((PEER_TOPK * PEER_TOPK, PEER_HEADS, ROUTE_LANES), F32)
    top3 = pltpu.VMEM((PEER_TOPK, PEER_HEADS, ROUTE_LANES), F32)
    return pl.pallas_call(
        _peer_route_kernel,
        grid=(n_tokens // ROUTE_TILE,),
        in_specs=[
            pl.BlockSpec((ROUTE_TILE, 2 * half), lambda i: (i, 0)),
            pl.BlockSpec((None, NKH, half), lambda i: (0, 0, 0)),
            pl.BlockSpec((None, NKH, half), lambda i: (1, 0, 0)),
        ],
        out_specs=[spec_hk, spec_hk, spec_kh, spec_kh],
        out_shape=[shape_hk, shape_hk, shape_kh, shape_kh],
        scratch_shapes=[stage, stage, stage, stage, cand3, cand3, cand3, cand3, top3, top3],
        compiler_params=_cparams("parallel"),
        name="peer_route",
    )(q, key_blocks, key_blocks)


def _peer_dense_kernel(x_ref, m_ref, u_ref, vt_ref, r2_ref, w2_ref, n1_ref, w1_ref, o_ref, h_scr, acc_scr):
    c = pl.program_id(1)

    @pl.when(c == 0)
    def _():
        m = m_ref[0]
        h_scr[...] = (x_ref[...] * (1.0 + m[4:5]) + m[3:4]).astype(BF16)
        acc_scr[...] = jnp.zeros_like(acc_scr)

    sub = BF16_SUBLANES
    zero = jnp.zeros((), BF16)
    keys_per_sub = SUB_EXPERTS // PEER_NKEYS
    parts = []
    for sb in range(EXPERT_BLOCK // SUB_EXPERTS):
        s = lax.dot_general(u_ref[sb * SUB_EXPERTS:(sb + 1) * SUB_EXPERTS, :], h_scr[...], _NT,
                            preferred_element_type=F32)
        act = _gelu(s.astype(BF16))
        for cc in range(keys_per_sub):
            key1 = c * (EXPERT_BLOCK // PEER_NKEYS) + sb * keys_per_sub + cc
            gate = jnp.zeros((PEER_NKEYS // sub, sub, TOKEN_TILE), BF16)
            for h in range(PEER_HEADS):
                n1 = jnp.broadcast_to(n1_ref[key1, pl.ds(h, 1), :], (sub, TOKEN_TILE)).astype(BF16)
                w1 = jnp.broadcast_to(w1_ref[key1, pl.ds(h, 1), :], (sub, TOKEN_TILE)).astype(BF16)
                r2 = r2_ref[h].reshape(PEER_NKEYS // sub, sub, TOKEN_TILE)
                w2 = w2_ref[h].reshape(PEER_NKEYS // sub, sub, TOKEN_TILE)
                gate = gate + jnp.where(r2 < n1[None], w2, zero) * w1[None]
            parts.append(act[cc * PEER_NKEYS:(cc + 1) * PEER_NKEYS] * gate.reshape(PEER_NKEYS, TOKEN_TILE))
    a = jnp.concatenate(parts, axis=0)
    acc_scr[...] += jnp.dot(vt_ref[...], a, preferred_element_type=F32)

    @pl.when(c == pl.num_programs(1) - 1)
    def _():
        o_ref[...] = acc_scr[...].T


def _peer_dense(tok, x, mods, u_tab, vt_tab, route):
    r2, w2, n1, w1 = route
    spec_hk = pl.BlockSpec((PEER_HEADS, PEER_NKEYS, TOKEN_TILE), lambda i, c: (0, 0, i))
    spec_kh = pl.BlockSpec((PEER_NKEYS, PEER_HEADS, TOKEN_TILE), lambda i, c: (0, 0, i))
    return pl.pallas_call(
        _peer_dense_kernel,
        grid=(tok.tiles, PEER_EXPERTS // EXPERT_BLOCK),
        in_specs=[
            pl.BlockSpec((TOKEN_TILE, D_MODEL), lambda i, c: (i, 0)),
            pl.BlockSpec((1, 6, D_MODEL), lambda i, c: (tok.mod_row(i), 0, 0)),
            pl.BlockSpec((EXPERT_BLOCK, D_MODEL), lambda i, c: (c, 0)),
            pl.BlockSpec((D_MODEL, EXPERT_BLOCK), lambda i, c: (0, c)),
            spec_hk, spec_hk, spec_kh, spec_kh,
        ],
        out_specs=pl.BlockSpec((TOKEN_TILE, D_MODEL), lambda i, c: (i, 0)),
        out_shape=jax.ShapeDtypeStruct((tok.total, D_MODEL), F32),
        scratch_shapes=[pltpu.VMEM((TOKEN_TILE, D_MODEL), BF16), pltpu.VMEM((D_MODEL, TOKEN_TILE), F32)],
        compiler_params=_cparams("parallel", "arbitrary"),
        name="peer_dense",
    )(x, mods, u_tab, vt_tab, r2, w2, n1, w1)


def kernel(x_prompt, x_sample, state_ssm_re, state_ssm_im, cache_k, cache_v, c, c_ctx, w_mod, b_mod, ln_g, ln_b, ssm_w_in, ssm_a_re, ssm_a_im, ssm_log_dt, ssm_b_re, ssm_b_im, ssm_c_re, ssm_c_im, ssm_d, ssm_w_glu, ssm_w_out, attn_w_qkv, attn_sink, attn_w_out, peer_w_q, peer_keys, peer_u, peer_v):
    depth = w_mod.shape[0]
    n_batch, seq, _ = x_prompt.shape
    dec_batch, dec_seq, _ = x_sample.shape
    tok = _Tokens(n_batch * seq, dec_batch, dec_seq)
    alpha = (2 * depth) ** 0.25

    cond = jnp.concatenate([c, c_ctx[None, :],
                            jnp.zeros((MOD_ROWS - dec_batch - 1, D_MODEL), F32)], axis=0)
    mods_all = _modulation_all(cond, w_mod, b_mod)

    x = jnp.concatenate([x_prompt.reshape(-1, D_MODEL), x_sample.reshape(-1, D_MODEL)], axis=0)
    st_re, st_im, st_k, st_v = [], [], [], []
    for i in range(depth):
        j = i // 2
        mods = mods_all[i]
        if i % 2 == 0:
            u = _mod_linear(tok, x, mods, ssm_w_in[j].astype(BF16), 0, "s5_in")
            ops = _s5_operators(ssm_a_re[j], ssm_a_im[j], ssm_log_dt[j], ssm_b_re[j], ssm_b_im[j],
                                ssm_c_re[j], ssm_c_im[j])
            zeros = jnp.zeros((SSM_GROUPS, n_batch, 4 * SSM_STATE), F32)
            y_c, s_c = _s5_core(_to_chunks(u[:tok.n_ctx], n_batch, seq), zeros, ops,
                                seq // SSM_CHUNK, n_batch, "s5_core_context")
            h0 = _state_to_cols(state_ssm_re[:, j], state_ssm_im[:, j])
            y_s, _ = _s5_core(_to_chunks(u[tok.n_ctx:], dec_batch, dec_seq), h0, ops,
                              dec_seq // SSM_CHUNK, dec_batch, "s5_core_latent")
            y = jnp.concatenate([_from_chunks(y_c, n_batch, seq), _from_chunks(y_s, dec_batch, dec_seq)], axis=0)
            s_re, s_im = _cols_to_state(s_c)
            st_re.append(s_re)
            st_im.append(s_im)
            x = _s5_tail(tok, y, u, x, mods, ssm_d[j], ssm_w_glu[j].astype(BF16), ssm_w_out[j].astype(BF16),
                         ln_g[i, 0], ln_b[i, 0], alpha)
        else:
            qkv = _mod_linear(tok, x, mods, _attn_weights(attn_w_qkv[j]), 0, "attn_qkv")
            o_c = _attn_context(qkv, attn_sink[j], n_batch, seq)
            past = cache_k.shape[2]
            o_s = _attn_latent(qkv, _rope_tables(dec_seq), cache_k[:, j].reshape(dec_batch, past, KV_DIM),
                               cache_v[:, j].reshape(dec_batch, past, KV_DIM), attn_sink[j], tok)
            kv_c = qkv[:tok.n_ctx, 2 * D_MODEL:2 * D_MODEL + 2 * KV_DIM]
            st_k.append(kv_c[:, :KV_DIM].reshape(n_batch, seq, N_KV_HEADS, HEAD_DIM))
            st_v.append(kv_c[:, KV_DIM:].reshape(n_batch, seq, N_KV_HEADS, HEAD_DIM))
            x = _mm_postnorm(tok, jnp.concatenate([o_c, o_s], axis=0), x, mods, attn_w_out[j].astype(BF16),
                             ln_g[i, 0], ln_b[i, 0], 2, alpha, "attn_out")
        w_q, key_blocks = _route_weights(peer_w_q[i], peer_keys[i])
        q = _mod_linear(tok, x, mods, w_q, 3, "peer_query")
        route = _peer_route(q, key_blocks, tok.total)
        o = _peer_dense(tok, x, mods, peer_u[i].astype(BF16), peer_v[i].T.astype(BF16), route)
        x = _postnorm(tok, o, x, mods, ln_g[i, 1], ln_b[i, 1], 5, alpha, "peer_norm")

    y_prompt = x[:tok.n_ctx].reshape(n_batch, seq, D_MODEL)
    y_sample = x[tok.n_ctx:].reshape(dec_batch, dec_seq, D_MODEL)
    return (y_prompt, y_sample, jnp.stack(st_re, axis=1), jnp.stack(st_im, axis=1),
            jnp.stack(st_k, axis=1), jnp.stack(st_v, axis=1))
```

```python
import functools
import math

import jax
import jax.numpy as jnp
from jax import lax
from jax.experimental import pallas as pl
from jax.experimental.pallas import tpu as pltpu

F32 = jnp.float32
BF16 = jnp.bfloat16

D_MODEL = 1024
GRID_W = 64
SSM_GROUP = 16
SSM_GROUPS = D_MODEL // SSM_GROUP
SSM_STATE = 64
SSM_CHUNK = 16
SSM_ROW = SSM_CHUNK * SSM_GROUP
HEAD_DIM = 64
N_HEADS = D_MODEL // HEAD_DIM
N_KV_HEADS = 4
KV_REP = N_HEADS // N_KV_HEADS
KV_DIM = N_KV_HEADS * HEAD_DIM
WINDOW = 128
ATTN_BLOCK = 128
ROPE_BASE = 10000.0
ROPE_AXIS_DIM = HEAD_DIM // 2
NEG_INF = -1e30
PEER_HEADS = 8
PEER_NKEYS = 128
PEER_EXPERTS = PEER_NKEYS * PEER_NKEYS
PEER_TOPK = 16
PEER_KEY_DIM = 256
PEER_HALF = PEER_KEY_DIM // 2
LN_EPS = 1e-5

TOKEN_TILE = 512
EXPERT_BLOCK = 1024
SUB_EXPERTS = 512
MOD_ROWS = 16
BF16_SUBLANES = 16
VMEM_LIMIT = 56 * 1024 * 1024

_NT = (((1,), (1,)), ((), ()))


def _cparams(*sem):
    return pltpu.CompilerParams(dimension_semantics=sem, vmem_limit_bytes=VMEM_LIMIT)


def _gelu(x):
    return 0.5 * x * (1.0 + jnp.tanh(0.7978845608028654 * (x + 0.044715 * (x * x * x))))


def _layer_norm_rows(z, g, b):
    mu = jnp.mean(z, axis=-1, keepdims=True)
    zc = z - mu
    var = jnp.mean(zc * zc, axis=-1, keepdims=True)
    return zc * lax.rsqrt(var + LN_EPS) * g + b


def _mod_kernel(c_ref, w_ref, b_ref, o_ref):
    c = c_ref[...]
    s = c * jax.nn.sigmoid(c)
    o_ref[...] = jnp.dot(s.astype(BF16), w_ref[...].astype(BF16),
                         preferred_element_type=F32) + b_ref[...]


def _modulation_all(cond, w_mod, b_mod):
    depth = w_mod.shape[0]
    n_out = w_mod.shape[2]
    nb = n_out // D_MODEL
    out = pl.pallas_call(
        _mod_kernel,
        grid=(depth, nb),
        in_specs=[
            pl.BlockSpec((MOD_ROWS, D_MODEL), lambda i, j: (0, 0)),
            pl.BlockSpec((None, D_MODEL, D_MODEL), lambda i, j: (i, 0, j)),
            pl.BlockSpec((None, 1, D_MODEL), lambda i, j: (i, 0, j)),
        ],
        out_specs=pl.BlockSpec((None, MOD_ROWS, D_MODEL), lambda i, j: (i, 0, j)),
        out_shape=jax.ShapeDtypeStruct((depth, MOD_ROWS, n_out), F32),
        compiler_params=_cparams("arbitrary", "arbitrary"),
        name="modulation",
    )(cond, w_mod, b_mod.reshape(depth, 1, n_out))
    return out.reshape(depth, MOD_ROWS, nb, D_MODEL)


class _Tokens:
    def __init__(self, n_ctx_tokens, dec_batch, dec_seq):
        self.n_ctx = n_ctx_tokens
        self.dec_batch = dec_batch
        self.dec_seq = dec_seq
        self.total = n_ctx_tokens + dec_batch * dec_seq
        assert n_ctx_tokens % TOKEN_TILE == 0 and dec_seq % TOKEN_TILE == 0
        self.ctx_tiles = n_ctx_tokens // TOKEN_TILE
        self.tiles = self.total // TOKEN_TILE
        self.tiles_per_seq = dec_seq // TOKEN_TILE

    def mod_row(self, i):
        return jnp.where(i < self.ctx_tiles, self.dec_batch, (i - self.ctx_tiles) // self.tiles_per_seq)


def _mod_spec(tok):
    return pl.BlockSpec((1, 6, D_MODEL), lambda i: (tok.mod_row(i), 0, 0))


def _row_spec(width):
    return pl.BlockSpec((TOKEN_TILE, width), lambda i: (i, 0))


def _full_spec(shape):
    nd = len(shape)
    return pl.BlockSpec(shape, lambda i: (0,) * nd)


def _linear_kernel(x_ref, m_ref, w_ref, o_ref, *, shift_idx):
    m = m_ref[0]
    h = x_ref[...] * (1.0 + m[shift_idx + 1:shift_idx + 2]) + m[shift_idx:shift_idx + 1]
    o_ref[...] = jnp.dot(h.astype(BF16), w_ref[...], preferred_element_type=F32)


def _mod_linear(tok, x, mods, w, shift_idx, name):
    n = w.shape[1]
    return pl.pallas_call(
        functools.partial(_linear_kernel, shift_idx=shift_idx),
        grid=(tok.tiles,),
        in_specs=[_row_spec(D_MODEL), _mod_spec(tok), _full_spec((D_MODEL, n))],
        out_specs=_row_spec(n),
        out_shape=jax.ShapeDtypeStruct((tok.total, n), F32),
        compiler_params=_cparams("parallel"),
        name=name,
    )(x, mods, w)


def _post_norm_rows(x, out, m, gate_idx, g_ref, b_ref, alpha):
    z = alpha * x + (1.0 + m[gate_idx:gate_idx + 1]) * out
    return _layer_norm_rows(z, g_ref[...], b_ref[...])


def _mm_postnorm_kernel(a_ref, x_ref, m_ref, w_ref, g_ref, b_ref, o_ref, *, gate_idx, alpha):
    y = jnp.dot(a_ref[...].astype(BF16), w_ref[...], preferred_element_type=F32)
    o_ref[...] = _post_norm_rows(x_ref[...], y, m_ref[0], gate_idx, g_ref, b_ref, alpha)


def _mm_postnorm(tok, a, x, mods, w, g, b, gate_idx, alpha, name):
    return pl.pallas_call(
        functools.partial(_mm_postnorm_kernel, gate_idx=gate_idx, alpha=alpha),
        grid=(tok.tiles,),
        in_specs=[_row_spec(D_MODEL), _row_spec(D_MODEL), _mod_spec(tok),
                  _full_spec((D_MODEL, D_MODEL)), _full_spec((1, D_MODEL)), _full_spec((1, D_MODEL))],
        out_specs=_row_spec(D_MODEL),
        out_shape=jax.ShapeDtypeStruct((tok.total, D_MODEL), F32),
        compiler_params=_cparams("parallel"),
        name=name,
    )(a, x, mods, w, g.reshape(1, D_MODEL), b.reshape(1, D_MODEL))


def _postnorm_kernel(o_in_ref, x_ref, m_ref, g_ref, b_ref, o_ref, *, gate_idx, alpha):
    o_ref[...] = _post_norm_rows(x_ref[...], o_in_ref[...], m_ref[0], gate_idx, g_ref, b_ref, alpha)


def _postnorm(tok, out, x, mods, g, b, gate_idx, alpha, name):
    return pl.pallas_call(
        functools.partial(_postnorm_kernel, gate_idx=gate_idx, alpha=alpha),
        grid=(tok.tiles,),
        in_specs=[_row_spec(D_MODEL), _row_spec(D_MODEL), _mod_spec(tok),
                  _full_spec((1, D_MODEL)), _full_spec((1, D_MODEL))],
        out_specs=_row_spec(D_MODEL),
        out_shape=jax.ShapeDtypeStruct((tok.total, D_MODEL), F32),
        compiler_params=_cparams("parallel"),
        name=name,
    )(out, x, mods, g.reshape(1, D_MODEL), b.reshape(1, D_MODEL))


def _s5_operators(a_re, a_im, log_dt, b_re, b_im, c_re, c_im):
    hi = lax.Precision.HIGHEST
    dt = jnp.exp(log_dt)[..., None]
    ar, ai = a_re * dt, a_im * dt
    k = jnp.arange(SSM_CHUNK + 1, dtype=F32)[:, None, None, None]
    mag = jnp.exp(k * ar[None])
    pw_re, pw_im = mag * jnp.cos(k * ai[None]), mag * jnp.sin(k * ai[None])
    l_re, l_im = pw_re[1], pw_im[1]
    den = a_re * a_re + a_im * a_im
    q_re = ((l_re - 1.0) * a_re + l_im * a_im) / den
    q_im = (l_im * a_re - (l_re - 1.0) * a_im) / den
    bt_re = q_re[..., None] * b_re - q_im[..., None] * b_im
    bt_im = q_re[..., None] * b_im + q_im[..., None] * b_re
    d_re = pw_re[..., None] * bt_re[None] - pw_im[..., None] * bt_im[None]
    d_im = pw_re[..., None] * bt_im[None] + pw_im[..., None] * bt_re[None]
    pw_re_c, pw_im_c = pw_re[:, :, :, None, :], pw_im[:, :, :, None, :]
    e_re = c_re[None] * pw_re_c - c_im[None] * pw_im_c
    e_im = c_re[None] * pw_im_c + c_im[None] * pw_re_c
    kern = (jnp.einsum('dgip,kdgpj->kdgij', c_re, d_re, precision=hi)
            - jnp.einsum('dgip,kdgpj->kdgij', c_im, d_im, precision=hi))
    s = jnp.arange(SSM_CHUNK)
    lag = s[None, :] - s[:, None]
    kf = kern[jnp.clip(lag, 0, SSM_CHUNK), 0]
    kb = kern[jnp.clip(-lag, 0, SSM_CHUNK), 1]
    mix = (jnp.where((lag >= 0)[:, :, None, None, None], kf, 0.0)
           + jnp.where((lag <= 0)[:, :, None, None, None], kb, 0.0))
    g = SSM_GROUPS
    m_op = jnp.transpose(mix, (2, 0, 4, 1, 3)).reshape(g, SSM_ROW, SSM_ROW)
    pf = (SSM_CHUNK - 1) - s
    win = jnp.stack([d_re[pf, 0], d_re[s, 1], d_im[pf, 0], d_im[s, 1]], axis=0)
    win_op = jnp.transpose(win, (2, 1, 4, 0, 3)).reshape(g, SSM_ROW, 4 * SSM_STATE)
    wout = jnp.stack([e_re[s + 1, 0], e_re[SSM_CHUNK - s, 1], -e_im[s + 1, 0], -e_im[SSM_CHUNK - s, 1]],
                     axis=0)
    wout_op = jnp.transpose(wout, (2, 0, 4, 1, 3)).reshape(g, 4 * SSM_STATE, SSM_ROW)
    lam = jnp.stack([jnp.concatenate([pw_re[SSM_CHUNK, 0], pw_re[SSM_CHUNK, 1]], axis=-1),
                     jnp.concatenate([pw_im[SSM_CHUNK, 0], pw_im[SSM_CHUNK, 1]], axis=-1)], axis=1)
    return m_op, win_op, wout_op, lam


def _s5_core_kernel(u_ref, h0_ref, m_ref, win_ref, wout_ref, lam_ref, y_ref, s_ref, z_scr, p_scr,
                    *, n_chunks, batch):
    hi = lax.Precision.HIGHEST
    p2 = 2 * SSM_STATE
    u = u_ref[...]
    z_scr[...] = jnp.dot(u, win_ref[...], preferred_element_type=F32, precision=hi)
    l_re = lam_ref[0:1, :]
    l_im = lam_ref[1:2, :]
    is_fwd = lax.broadcasted_iota(jnp.int32, (batch, p2), 1) < SSM_STATE

    def step(k, carry):
        s_re, s_im = carry
        rf = pl.multiple_of(k * batch, batch)
        rb = pl.multiple_of((n_chunks - 1 - k) * batch, batch)
        zf = z_scr[pl.ds(rf, batch), :]
        zb = z_scr[pl.ds(rb, batch), :]
        z_re = jnp.where(is_fwd, zf[:, :p2], zb[:, :p2])
        z_im = jnp.where(is_fwd, zf[:, p2:], zb[:, p2:])
        p_scr[pl.ds(rf, batch), 0:SSM_STATE] = s_re[:, :SSM_STATE]
        p_scr[pl.ds(rf, batch), p2:p2 + SSM_STATE] = s_im[:, :SSM_STATE]
        p_scr[pl.ds(rb, batch), SSM_STATE:p2] = s_re[:, SSM_STATE:]
        p_scr[pl.ds(rb, batch), p2 + SSM_STATE:] = s_im[:, SSM_STATE:]
        n_re = l_re * s_re - l_im * s_im + z_re
        n_im = l_re * s_im + l_im * s_re + z_im
        return n_re, n_im

    h0 = h0_ref[...]
    s_re, s_im = lax.fori_loop(0, n_chunks, step, (h0[:, :p2], h0[:, p2:]))
    s_ref[:, :p2] = s_re
    s_ref[:, p2:] = s_im
    y_ref[...] = (jnp.dot(u, m_ref[...], preferred_element_type=F32, precision=hi)
                  + jnp.dot(p_scr[...], wout_ref[...], preferred_element_type=F32, precision=hi))


def _s5_core(ug, h0, ops, n_chunks, batch, name):
    m_op, win_op, wout_op, lam = ops
    g, rows, _ = ug.shape
    st = 4 * SSM_STATE
    return pl.pallas_call(
        functools.partial(_s5_core_kernel, n_chunks=n_chunks, batch=batch),
        grid=(g,),
        in_specs=[
            pl.BlockSpec((None, rows, SSM_ROW), lambda i: (i, 0, 0)),
            pl.BlockSpec((None, batch, st), lambda i: (i, 0, 0)),
            pl.BlockSpec((None, SSM_ROW, SSM_ROW), lambda i: (i, 0, 0)),
            pl.BlockSpec((None, SSM_ROW, st), lambda i: (i, 0, 0)),
            pl.BlockSpec((None, st, SSM_ROW), lambda i: (i, 0, 0)),
            pl.BlockSpec((None, 2, 2 * SSM_STATE), lambda i: (i, 0, 0)),
        ],
        out_specs=[
            pl.BlockSpec((None, rows, SSM_ROW), lambda i: (i, 0, 0)),
            pl.BlockSpec((None, batch, st), lambda i: (i, 0, 0)),
        ],
        out_shape=[jax.ShapeDtypeStruct((g, rows, SSM_ROW), F32),
                   jax.ShapeDtypeStruct((g, batch, st), F32)],
        scratch_shapes=[pltpu.VMEM((rows, st), F32), pltpu.VMEM((rows, st), F32)],
        compiler_params=_cparams("parallel"),
        name=name,
    )(ug, h0, m_op, win_op, wout_op, lam)


def _to_chunks(u, batch, seq):
    n_chunks = seq // SSM_CHUNK
    t = u.reshape(batch, n_chunks, SSM_CHUNK, SSM_GROUPS, SSM_GROUP)
    return jnp.transpose(t, (3, 1, 0, 2, 4)).reshape(SSM_GROUPS, n_chunks * batch, SSM_ROW)


def _from_chunks(y, batch, seq):
    n_chunks = seq // SSM_CHUNK
    t = y.reshape(SSM_GROUPS, n_chunks, batch, SSM_CHUNK, SSM_GROUP)
    return jnp.transpose(t, (2, 1, 3, 0, 4)).reshape(batch * seq, D_MODEL)


def _state_to_cols(h_re, h_im):
    cols = jnp.concatenate([h_re[:, 0], h_re[:, 1], h_im[:, 0], h_im[:, 1]], axis=-1)
    return jnp.transpose(cols, (1, 0, 2))


def _cols_to_state(s):
    t = jnp.transpose(s, (1, 0, 2)).reshape(s.shape[1], SSM_GROUPS, 2, 2, SSM_STATE)
    t = jnp.transpose(t, (2, 0, 3, 1, 4))
    return t[0], t[1]


def _s5_tail_kernel(y_ref, u_ref, x_ref, m_ref, d_ref, wg_ref, wo_ref, g_ref, b_ref, o_ref, *, alpha):
    v = _gelu(y_ref[...] + d_ref[...] * u_ref[...])
    gl = jnp.dot(v.astype(BF16), wg_ref[...], preferred_element_type=F32)
    a = gl[:, :D_MODEL] * jax.nn.sigmoid(gl[:, D_MODEL:])
    out = jnp.dot(a.astype(BF16), wo_ref[...], preferred_element_type=F32)
    o_ref[...] = _post_norm_rows(x_ref[...], out, m_ref[0], 2, g_ref, b_ref, alpha)


def _s5_tail(tok, y, u, x, mods, d, w_glu, w_out, g, b, alpha):
    return pl.pallas_call(
        functools.partial(_s5_tail_kernel, alpha=alpha),
        grid=(tok.tiles,),
        in_specs=[_row_spec(D_MODEL), _row_spec(D_MODEL), _row_spec(D_MODEL), _mod_spec(tok),
                  _full_spec((1, D_MODEL)), _full_spec((D_MODEL, 2 * D_MODEL)),
                  _full_spec((D_MODEL, D_MODEL)), _full_spec((1, D_MODEL)), _full_spec((1, D_MODEL))],
        out_specs=_row_spec(D_MODEL),
        out_shape=jax.ShapeDtypeStruct((tok.total, D_MODEL), F32),
        compiler_params=_cparams("parallel"),
        name="s5_tail",
    )(y, u, x, mods, d.reshape(1, D_MODEL), w_glu, w_out, g.reshape(1, D_MODEL), b.reshape(1, D_MODEL))


QKV_COLS = 3 * D_MODEL


def _attn_weights(w_qkv):
    wq, wk, wv = w_qkv[:, :D_MODEL], w_qkv[:, D_MODEL:D_MODEL + KV_DIM], w_qkv[:, D_MODEL + KV_DIM:]
    half = ROPE_AXIS_DIM // 2

    def partner(w):
        n = w.shape[1]
        t = w.reshape(D_MODEL, n // ROPE_AXIS_DIM, 2, half)
        return jnp.stack([-t[:, :, 1], t[:, :, 0]], axis=2).reshape(D_MODEL, n)

    pad = jnp.zeros((D_MODEL, QKV_COLS - 2 * D_MODEL - 3 * KV_DIM), w_qkv.dtype)
    return jnp.concatenate([wq, partner(wq), wk, wv, partner(wk), pad], axis=1).astype(BF16)


def _rope_tables(seq):
    half = ROPE_AXIS_DIM // 2
    rows = seq // GRID_W
    row = jnp.repeat(jnp.arange(rows, dtype=F32), GRID_W)
    col = jnp.tile(jnp.arange(GRID_W, dtype=F32), rows)
    inv = ROPE_BASE ** (-jnp.arange(half, dtype=F32) / half)
    ang_r = row[:, None] * inv[None, :]
    ang_c = col[:, None] * inv[None, :]
    ang = jnp.concatenate([ang_r, ang_r, ang_c, ang_c], axis=-1)
    cos = jnp.tile(jnp.cos(ang), (1, N_KV_HEADS))
    sin = jnp.tile(jnp.sin(ang), (1, N_KV_HEADS))
    return jnp.concatenate([cos, sin], axis=-1)


def _softmax_heads(q, k_list, v_list, mask_list, sink_ref, o_ref):
    scale = HEAD_DIM ** -0.5
    for h in range(N_HEADS):
        g = h // KV_REP
        qh = q[:, h * HEAD_DIM:(h + 1) * HEAD_DIM].astype(BF16)
        sink = sink_ref[h]
        logits = []
        for k, mask in zip(k_list, mask_list):
            s = lax.dot_general(qh, k[:, g * HEAD_DIM:(g + 1) * HEAD_DIM], _NT,
                                preferred_element_type=F32) * scale
            if mask is not None:
                s = jnp.where(mask, s, NEG_INF)
            logits.append(s)
        m = jnp.full((q.shape[0], 1), sink, F32)
        for s in logits:
            m = jnp.maximum(m, jnp.max(s, axis=-1, keepdims=True))
        denom = jnp.exp(sink - m)
        acc = jnp.zeros((q.shape[0], HEAD_DIM), F32)
        for s, v in zip(logits, v_list):
            p = jnp.exp(s - m)
            denom = denom + jnp.sum(p, axis=-1, keepdims=True)
            acc = acc + jnp.dot(p.astype(BF16), v[:, g * HEAD_DIM:(g + 1) * HEAD_DIM],
                                preferred_element_type=F32)
        o_ref[:, h * HEAD_DIM:(h + 1) * HEAD_DIM] = acc / denom


def _attn_ctx_kernel(sink_ref, q_ref, kv_ref, o_ref):
    kv = kv_ref[...]
    k = kv[:, :KV_DIM].astype(BF16)
    v = kv[:, KV_DIM:2 * KV_DIM].astype(BF16)
    _softmax_heads(q_ref[...], [k], [v], [None], sink_ref, o_ref)


def _attn_context(qkv, sink, n_batch, seq):
    return pl.pallas_call(
        _attn_ctx_kernel,
        grid=(n_batch,),
        in_specs=[
            pl.BlockSpec(memory_space=pltpu.SMEM),
            pl.BlockSpec((seq, D_MODEL), lambda b: (b, 0)),
            pl.BlockSpec((seq, D_MODEL), lambda b: (b, 2)),
        ],
        out_specs=pl.BlockSpec((seq, D_MODEL), lambda b: (b, 0)),
        out_shape=jax.ShapeDtypeStruct((n_batch * seq, D_MODEL), F32),
        compiler_params=_cparams("parallel"),
        name="attn_context",
    )(sink, qkv, qkv)


def _attn_lat_kernel(sink_ref, q_ref, qp_ref, kv0_ref, kv1_ref, kv2_ref, t0_ref, t1_ref, t2_ref,
                     kc_ref, vc_ref, o_ref, *, n_qblocks):
    qb = pl.program_id(1)
    t1 = t1_ref[...]
    cos_q = jnp.concatenate([t1[:, :KV_DIM]] * KV_REP, axis=-1)
    sin_q = jnp.concatenate([t1[:, KV_DIM:]] * KV_REP, axis=-1)
    q = q_ref[...] * cos_q + qp_ref[...] * sin_q

    def rope_k(kv_ref, t_ref):
        kv = kv_ref[...]
        t = t_ref[...]
        k = kv[:, :KV_DIM] * t[:, :KV_DIM] + kv[:, 2 * KV_DIM:3 * KV_DIM] * t[:, KV_DIM:]
        return k.astype(BF16), kv[:, KV_DIM:2 * KV_DIM].astype(BF16)

    k0, v0 = rope_k(kv0_ref, t0_ref)
    k1, v1 = rope_k(kv1_ref, t1_ref)
    k2, v2 = rope_k(kv2_ref, t2_ref)
    k_loc = jnp.concatenate([k0, k1, k2], axis=0)
    v_loc = jnp.concatenate([v0, v1, v2], axis=0)
    span = 3 * ATTN_BLOCK
    qi = lax.broadcasted_iota(jnp.int32, (ATTN_BLOCK, span), 0)
    kj = lax.broadcasted_iota(jnp.int32, (ATTN_BLOCK, span), 1) - ATTN_BLOCK
    ok = (jnp.abs(kj - qi) <= WINDOW)
    ok = ok & ((kj >= 0) | (qb > 0)) & ((kj < ATTN_BLOCK) | (qb < n_qblocks - 1))
    kc = kc_ref[0].astype(BF16)
    vc = vc_ref[0].astype(BF16)
    _softmax_heads(q, [kc, k_loc], [vc, v_loc], [None, ok], sink_ref, o_ref)


def _attn_latent(qkv, tables, k_ctx, v_ctx, sink, tok):
    nqb = tok.dec_seq // ATTN_BLOCK
    base = tok.n_ctx // ATTN_BLOCK
    past = k_ctx.shape[1]

    def rows(b, i, off):
        return base + b * nqb + jnp.clip(i + off, 0, nqb - 1)

    def kv_spec(off):
        return pl.BlockSpec((ATTN_BLOCK, D_MODEL), lambda b, i: (rows(b, i, off), 2))

    def tab_spec(off):
        return pl.BlockSpec((ATTN_BLOCK, 2 * KV_DIM), lambda b, i: (jnp.clip(i + off, 0, nqb - 1), 0))

    return pl.pallas_call(
        functools.partial(_attn_lat_kernel, n_qblocks=nqb),
        grid=(tok.dec_batch, nqb),
        in_specs=[
            pl.BlockSpec(memory_space=pltpu.SMEM),
            pl.BlockSpec((ATTN_BLOCK, D_MODEL), lambda b, i: (rows(b, i, 0), 0)),
            pl.BlockSpec((ATTN_BLOCK, D_MODEL), lambda b, i: (rows(b, i, 0), 1)),
            kv_spec(-1), kv_spec(0), kv_spec(1),
            tab_spec(-1), tab_spec(0), tab_spec(1),
            pl.BlockSpec((1, past, KV_DIM), lambda b, i: (b, 0, 0)),
            pl.BlockSpec((1, past, KV_DIM), lambda b, i: (b, 0, 0)),
        ],
        out_specs=pl.BlockSpec((ATTN_BLOCK, D_MODEL), lambda b, i: (b * nqb + i, 0)),
        out_shape=jax.ShapeDtypeStruct((tok.dec_batch * tok.dec_seq, D_MODEL), F32),
        compiler_params=_cparams("parallel", "arbitrary"),
        name="attn_latent",
    )(sink, qkv, qkv, qkv, qkv, qkv, tables, tables, tables, k_ctx, v_ctx)


ROUTE_TILE = 256
ROUTE_LANES = 128
NKH = PEER_NKEYS * PEER_HEADS


def _oddeven_merge_sort_pairs(n):
    pairs = []

    def merge(lo, hi, r):
        step = r * 2
        if step < hi - lo:
            merge(lo, hi, step)
            merge(lo + r, hi, step)
            pairs.extend((i, i + r) for i in range(lo + r, hi - r, step))
        else:
            pairs.append((lo, lo + r))

    def sort(lo, hi):
        if hi - lo >= 1:
            mid = lo + (hi - lo) // 2
            sort(lo, mid)
            sort(mid + 1, hi)
            merge(lo, hi, 1)

    sort(0, n - 1)
    return pairs


_SORT16 = _oddeven_merge_sort_pairs(PEER_TOPK)


def _exchange(x, i, j):
    x[i], x[j] = jnp.maximum(x[i], x[j]), jnp.minimum(x[i], x[j])


def _top16_sorted(vals):
    groups = []
    for g in range(0, len(vals), PEER_TOPK):
        x = list(vals[g:g + PEER_TOPK])
        for i, j in _SORT16:
            _exchange(x, i, j)
        groups.append(x)
    while len(groups) > 1:
        merged = []
        for a, b in zip(groups[0::2], groups[1::2]):
            x = [jnp.maximum(a[i], b[PEER_TOPK - 1 - i]) for i in range(PEER_TOPK)]
            for d in (8, 4, 2, 1):
                for i in range(PEER_TOPK):
                    if not i & d:
                        _exchange(x, i, i + d)
            merged.append(x)
        groups = merged
    return groups[0]


def _rank_bits(s, v):
    b3 = v[7] > s
    b2 = jnp.where(b3, v[11], v[3]) > s
    b1 = jnp.where(b3, jnp.where(b2, v[13], v[9]), jnp.where(b2, v[5], v[1])) > s
    hi = jnp.where(b2, jnp.where(b1, v[14], v[12]), jnp.where(b1, v[10], v[8]))
    lo = jnp.where(b2, jnp.where(b1, v[6], v[4]), jnp.where(b1, v[2], v[0]))
    b0 = jnp.where(b3, hi, lo) > s
    return (b3, b2, b1, b0), v[15] > s


def _select16(bits, leaves):
    b3, b2, b1, b0 = bits
    lvl = [jnp.where(b0, leaves[2 * i + 1], leaves[2 * i]) for i in range(8)]
    lvl = [jnp.where(b1, lvl[2 * i + 1], lvl[2 * i]) for i in range(4)]
    lvl = [jnp.where(b2, lvl[2 * i + 1], lvl[2 * i]) for i in range(2)]
    return jnp.where(b3, lvl[1], lvl[0])


def _exact_top16(s_ref, rank_ref, top_ref):
    n = s_ref.shape[0]
    rank_ref[...] = jnp.full(rank_ref.shape, float(PEER_TOPK), F32)

    def body(k, carry):
        s = s_ref[...]
        iota = lax.broadcasted_iota(jnp.int32, s.shape, 0).astype(F32)
        m = jnp.max(s, axis=0)
        idx = jnp.min(jnp.where(s == m[None], iota, float(n)), axis=0)
        hit = iota == idx[None]
        rank_ref[...] = jnp.where(hit, lax.convert_element_type(k, F32), rank_ref[...])
        s_ref[...] = jnp.where(hit, -jnp.inf, s)
        top_ref[k] = m
        return carry

    lax.fori_loop(0, PEER_TOPK, body, 0)


def _peer_route_kernel(q_ref, k1_ref, k2_ref, r2_ref, w2_ref, n1_ref, w1_ref,
                       r2_st, w2_st, n1_st, w1_st, sa_scr, ra_scr, sb_scr, rb_scr, ta_scr, tb_scr):
    q = q_ref[...].astype(BF16)
    half = PEER_HEADS * PEER_HALF
    s1_all = lax.dot_general(k1_ref[...], q[:, :half], _NT, preferred_element_type=F32)
    s2_all = lax.dot_general(k2_ref[...], q[:, half:], _NT, preferred_element_type=F32)
    slab = (PEER_HEADS, ROUTE_LANES)
    ninf = jnp.full(slab, -jnp.inf, F32)

    for j in range(ROUTE_TILE // ROUTE_LANES):
        lanes = slice(j * ROUTE_LANES, (j + 1) * ROUTE_LANES)
        s1 = [s1_all[k * PEER_HEADS:(k + 1) * PEER_HEADS, lanes] for k in range(PEER_NKEYS)]
        s2 = [s2_all[k * PEER_HEADS:(k + 1) * PEER_HEADS, lanes] for k in range(PEER_NKEYS)]
        v1 = _top16_sorted(s1)
        v2 = _top16_sorted(s2)
        cand = [[v1[a] + v2[b] for b in range(PEER_TOPK // (a + 1))] for a in range(PEER_TOPK)]
        flat = [c for row in cand for c in row]
        flat += [ninf] * (-len(flat) % PEER_TOPK)
        top = _top16_sorted(flat)
        theta = top[PEER_TOPK - 1]
        nb = []
        for row in cand:
            cnt = jnp.zeros(slab, F32)
            for c in row:
                cnt = cnt + jnp.where(c >= theta, 1.0, 0.0)
            nb.append(cnt)
        z = jnp.zeros(slab, F32)
        for k in range(PEER_TOPK):
            z = z + jnp.exp(top[k] - top[0])
        inv_z = 1.0 / z
        total = nb[0]
        for a in range(1, PEER_TOPK):
            total = total + nb[a]
        tied = total != float(PEER_TOPK)
        for k in range(PEER_TOPK - 1):
            tied = tied | ((v1[k] == v1[k + 1]) & (nb[k] != nb[k + 1])) | ((v2[k] == v2[k + 1]) & (nb[0] > float(k)))
        in1 = jnp.zeros(slab, F32)
        in2 = jnp.zeros(slab, F32)
        for k in range(PEER_NKEYS):
            rows = slice(k * PEER_HEADS, (k + 1) * PEER_HEADS)
            bits, out1 = _rank_bits(s1[k], v1)
            n1_st[rows, :] = jnp.where(out1, 0.0, _select16(bits, nb))
            w1_st[rows, :] = jnp.exp(s1[k] - v1[0]) * inv_z
            in1 = in1 + jnp.where(out1, 0.0, 1.0)
            (b3, b2, b1, b0), out2 = _rank_bits(s2[k], v2)
            r = (jnp.where(b3, 8.0, 0.0) + jnp.where(b2, 4.0, 0.0)) + (jnp.where(b1, 2.0, 0.0) + jnp.where(b0, 1.0, 0.0))
            r2_st[rows, :] = jnp.where(out2, float(PEER_TOPK), r)
            w2_st[rows, :] = jnp.exp(s2[k] - v2[0])
            in2 = in2 + jnp.where(out2, 0.0, 1.0)
        tied = tied | ((in1 != float(PEER_TOPK)) & (nb[PEER_TOPK - 1] != 0.0))
        tied = tied | ((in2 != float(PEER_TOPK)) & (nb[0] == float(PEER_TOPK)))
        n_tied = jnp.sum(jnp.where(tied, 1.0, 0.0))

        @pl.when(n_tied > 0.0)
        def _():
            shape3 = (PEER_NKEYS, PEER_HEADS, ROUTE_LANES)
            sa_scr[...] = s1_all[:, lanes].reshape(shape3)
            _exact_top16(sa_scr, ra_scr, ta_scr)
            sb_scr[...] = s2_all[:, lanes].reshape(shape3)
            _exact_top16(sb_scr, rb_scr, tb_scr)
            r1 = ra_scr[...]
            r2_st[...] = rb_scr[...].reshape(NKH, ROUTE_LANES)
            e1 = jnp.exp(s1_all[:, lanes].reshape(shape3) - ta_scr[0][None])
            w2_st[...] = jnp.exp(s2_all[:, lanes] - jnp.tile(tb_scr[0], (PEER_NKEYS, 1)))
            n_cand = len(flat)
            pos = 0
            for a in range(PEER_TOPK):
                width = PEER_TOPK // (a + 1)
                sa_scr[pos:pos + width] = ta_scr[a][None] + tb_scr[0:width]
                pos += width
            sa_scr[pos:n_cand] = jnp.full((n_cand - pos,) + slab, -jnp.inf, F32)
            _exact_top16(sa_scr.at[0:n_cand], sb_scr.at[0:n_cand], ta_scr)
            picked = jnp.where(sb_scr[0:n_cand] < float(PEER_TOPK), 1.0, 0.0)
            zz = jnp.zeros(slab, F32)
            for k in range(PEER_TOPK):
                zz = zz + jnp.exp(ta_scr[k] - ta_scr[0])
            n1 = jnp.zeros(shape3, F32)
            pos = 0
            for a in range(PEER_TOPK):
                width = PEER_TOPK // (a + 1)
                cnt = jnp.sum(picked[pos:pos + width], axis=0)
                n1 = jnp.where(r1 == float(a), cnt[None], n1)
                pos += width
            n1_st[...] = n1.reshape(NKH, ROUTE_LANES)
            w1_st[...] = (e1 / zz[None]).reshape(NKH, ROUTE_LANES)

        n1_ref[:, :, lanes] = n1_st[...].reshape(PEER_NKEYS, PEER_HEADS, ROUTE_LANES)
        w1_ref[:, :, lanes] = w1_st[...].reshape(PEER_NKEYS, PEER_HEADS, ROUTE_LANES)
        for h in range(PEER_HEADS):
            r2_ref[h, :, lanes] = r2_st[pl.ds(h, PEER_NKEYS, stride=PEER_HEADS), :].astype(BF16)
            w2_ref[h, :, lanes] = w2_st[pl.ds(h, PEER_NKEYS, stride=PEER_HEADS), :].astype(BF16)


def _route_weights(w_q, keys):
    w = w_q.reshape(D_MODEL, PEER_HEADS, 2, PEER_HALF)
    w = jnp.transpose(w, (0, 2, 1, 3)).reshape(D_MODEL, PEER_HEADS * PEER_KEY_DIM)
    eye = jnp.eye(PEER_HEADS, dtype=keys.dtype)
    big = jnp.einsum('phkd,hg->pkhgd', keys, eye).reshape(2, NKH, PEER_HEADS * PEER_HALF)
    return w.astype(BF16), big.astype(BF16)


def _peer_route(q, key_blocks, n_tokens):
    half = PEER_HEADS * PEER_HALF
    shape_kh = jax.ShapeDtypeStruct((PEER_NKEYS, PEER_HEADS, n_tokens), F32)
    shape_hk = jax.ShapeDtypeStruct((PEER_HEADS, PEER_NKEYS, n_tokens), BF16)
    spec_kh = pl.BlockSpec((PEER_NKEYS, PEER_HEADS, ROUTE_TILE), lambda i: (0, 0, i))
    spec_hk = pl.BlockSpec((PEER_HEADS, PEER_NKEYS, ROUTE_TILE), lambda i: (0, 0, i))
    stage = pltpu.VMEM((NKH, ROUTE_LANES), F32)
    cand3 = pltpu.VMEM((PEER_NKEYS, PEER_HEADS, ROUTE_LANES), F32)
    top3 = pltpu.VMEM((PEER_TOPK, PEER_HEADS, ROUTE_LANES), F32)
    return pl.pallas_call(
        _peer_route_kernel,
        grid=(n_tokens // ROUTE_TILE,),
        in_specs=[
            pl.BlockSpec((ROUTE_TILE, 2 * half), lambda i: (i, 0)),
            pl.BlockSpec((None, NKH, half), lambda i: (0, 0, 0)),
            pl.BlockSpec((None, NKH, half), lambda i: (1, 0, 0)),
        ],
        out_specs=[spec_hk, spec_hk, spec_kh, spec_kh],
        out_shape=[shape_hk, shape_hk, shape_kh, shape_kh],
        scratch_shapes=[stage, stage, stage, stage, cand3, cand3, cand3, cand3, top3, top3],
        compiler_params=_cparams("parallel"),
        name="peer_route",
    )(q, key_blocks, key_blocks)


def _peer_dense_kernel(x_ref, m_ref, u_ref, vt_ref, r2_ref, w2_ref, n1_ref, w1_ref, o_ref, h_scr, acc_scr):
    c = pl.program_id(1)

    @pl.when(c == 0)
    def _():
        m = m_ref[0]
        h_scr[...] = (x_ref[...] * (1.0 + m[4:5]) + m[3:4]).T.astype(BF16)
        acc_scr[...] = jnp.zeros_like(acc_scr)

    sub = BF16_SUBLANES
    zero = jnp.zeros((), BF16)
    keys_per_sub = SUB_EXPERTS // PEER_NKEYS
    parts = []
    for sb in range(EXPERT_BLOCK // SUB_EXPERTS):
        s = jnp.dot(u_ref[sb * SUB_EXPERTS:(sb + 1) * SUB_EXPERTS, :], h_scr[...],
                    preferred_element_type=F32)
        act = _gelu(s.astype(BF16))
        for cc in range(keys_per_sub):
            key1 = c * (EXPERT_BLOCK // PEER_NKEYS) + sb * keys_per_sub + cc
            gate = jnp.zeros((PEER_NKEYS // sub, sub, TOKEN_TILE), BF16)
            for h in range(PEER_HEADS):
                n1 = jnp.broadcast_to(n1_ref[key1, pl.ds(h, 1), :], (sub, TOKEN_TILE)).astype(BF16)
                w1 = jnp.broadcast_to(w1_ref[key1, pl.ds(h, 1), :], (sub, TOKEN_TILE)).astype(BF16)
                r2 = r2_ref[h].reshape(PEER_NKEYS // sub, sub, TOKEN_TILE)
                w2 = w2_ref[h].reshape(PEER_NKEYS // sub, sub, TOKEN_TILE)
                gate = gate + jnp.where(r2 < n1[None], w2, zero) * w1[None]
            parts.append(act[cc * PEER_NKEYS:(cc + 1) * PEER_NKEYS] * gate.reshape(PEER_NKEYS, TOKEN_TILE))
    a = jnp.concatenate(parts, axis=0)
    acc_scr[...] += jnp.dot(vt_ref[...], a, preferred_element_type=F32)

    @pl.when(c == pl.num_programs(1) - 1)
    def _():
        o_ref[...] = acc_scr[...].T


def _peer_dense(tok, x, mods, u_tab, vt_tab, route):
    r2, w2, n1, w1 = route
    spec_hk = pl.BlockSpec((PEER_HEADS, PEER_NKEYS, TOKEN_TILE), lambda i, c: (0, 0, i))
    spec_kh = pl.BlockSpec((PEER_NKEYS, PEER_HEADS, TOKEN_TILE), lambda i, c: (0, 0, i))
    return pl.pallas_call(
        _peer_dense_kernel,
        grid=(tok.tiles, PEER_EXPERTS // EXPERT_BLOCK),
        in_specs=[
            pl.BlockSpec((TOKEN_TILE, D_MODEL), lambda i, c: (i, 0)),
            pl.BlockSpec((1, 6, D_MODEL), lambda i, c: (tok.mod_row(i), 0, 0)),
            pl.BlockSpec((EXPERT_BLOCK, D_MODEL), lambda i, c: (c, 0)),
            pl.BlockSpec((D_MODEL, EXPERT_BLOCK), lambda i, c: (0, c)),
            spec_hk, spec_hk, spec_kh, spec_kh,
        ],
        out_specs=pl.BlockSpec((TOKEN_TILE, D_MODEL), lambda i, c: (i, 0)),
        out_shape=jax.ShapeDtypeStruct((tok.total, D_MODEL), F32),
        scratch_shapes=[pltpu.VMEM((D_MODEL, TOKEN_TILE), BF16), pltpu.VMEM((D_MODEL, TOKEN_TILE), F32)],
        compiler_params=_cparams("parallel", "arbitrary"),
        name="peer_dense",
    )(x, mods, u_tab, vt_tab, r2, w2, n1, w1)


def kernel(x_prompt, x_sample, state_ssm_re, state_ssm_im, cache_k, cache_v, c, c_ctx, w_mod, b_mod, ln_g, ln_b, ssm_w_in, ssm_a_re, ssm_a_im, ssm_log_dt, ssm_b_re, ssm_b_im, ssm_c_re, ssm_c_im, ssm_d, ssm_w_glu, ssm_w_out, attn_w_qkv, attn_sink, attn_w_out, peer_w_q, peer_keys, peer_u, peer_v):
    depth = w_mod.shape[0]
    n_batch, seq, _ = x_prompt.shape
    dec_batch, dec_seq, _ = x_sample.shape
    tok = _Tokens(n_batch * seq, dec_batch, dec_seq)
    alpha = (2 * depth) ** 0.25

    cond = jnp.concatenate([c, c_ctx[None, :],
                            jnp.zeros((MOD_ROWS - dec_batch - 1, D_MODEL), F32)], axis=0)
    mods_all = _modulation_all(cond, w_mod, b_mod)

    x = jnp.concatenate([x_prompt.reshape(-1, D_MODEL), x_sample.reshape(-1, D_MODEL)], axis=0)
    st_re, st_im, st_k, st_v = [], [], [], []
    for i in range(depth):
        j = i // 2
        mods = mods_all[i]
        if i % 2 == 0:
            u = _mod_linear(tok, x, mods, ssm_w_in[j].astype(BF16), 0, "s5_in")
            ops = _s5_operators(ssm_a_re[j], ssm_a_im[j], ssm_log_dt[j], ssm_b_re[j], ssm_b_im[j],
                                ssm_c_re[j], ssm_c_im[j])
            zeros = jnp.zeros((SSM_GROUPS, n_batch, 4 * SSM_STATE), F32)
            y_c, s_c = _s5_core(_to_chunks(u[:tok.n_ctx], n_batch, seq), zeros, ops,
                                seq // SSM_CHUNK, n_batch, "s5_core_context")
            h0 = _state_to_cols(state_ssm_re[:, j], state_ssm_im[:, j])
            y_s, _ = _s5_core(_to_chunks(u[tok.n_ctx:], dec_batch, dec_seq), h0, ops,
                              dec_seq // SSM_CHUNK, dec_batch, "s5_core_latent")
            y = jnp.concatenate([_from_chunks(y_c, n_batch, seq), _from_chunks(y_s, dec_batch, dec_seq)], axis=0)
            s_re, s_im = _cols_to_state(s_c)
            st_re.append(s_re)
            st_im.append(s_im)
            x = _s5_tail(tok, y, u, x, mods, ssm_d[j], ssm_w_glu[j].astype(BF16), ssm_w_out[j].astype(BF16),
                         ln_g[i, 0], ln_b[i, 0], alpha)
        else:
            qkv = _mod_linear(tok, x, mods, _attn_weights(attn_w_qkv[j]), 0, "attn_qkv")
            o_c = _attn_context(qkv, attn_sink[j], n_batch, seq)
            past = cache_k.shape[2]
            o_s = _attn_latent(qkv, _rope_tables(dec_seq), cache_k[:, j].reshape(dec_batch, past, KV_DIM),
                               cache_v[:, j].reshape(dec_batch, past, KV_DIM), attn_sink[j], tok)
            kv_c = qkv[:tok.n_ctx, 2 * D_MODEL:2 * D_MODEL + 2 * KV_DIM]
            st_k.append(kv_c[:, :KV_DIM].reshape(n_batch, seq, N_KV_HEADS, HEAD_DIM))
            st_v.append(kv_c[:, KV_DIM:].reshape(n_batch, seq, N_KV_HEADS, HEAD_DIM))
            x = _mm_postnorm(tok, jnp.concatenate([o_c, o_s], axis=0), x, mods, attn_w_out[j].astype(BF16),
                             ln_g[i, 0], ln_b[i, 0], 2, alpha, "attn_out")
        w_q, key_blocks = _route_weights(peer_w_q[i], peer_keys[i])
        q = _mod_linear(tok, x, mods, w_q, 3, "peer_query")
        route = _peer_route(q, key_blocks, tok.total)
        o = _peer_dense(tok, x, mods, peer_u[i].astype(BF16), peer_v[i].T.astype(BF16), route)
        x = _postnorm(tok, o, x, mods, ln_g[i, 1], ln_b[i, 1], 5, alpha, "peer_norm")

    y_prompt = x[:tok.n_ctx].reshape(n_batch, seq, D_MODEL)
    y_sample = x[tok.n_ctx:].reshape(dec_batch, dec_seq, D_MODEL)
    return (y_prompt, y_sample, jnp.stack(st_re, axis=1), jnp.stack(st_im, axis=1),
            jnp.stack(st_k, axis=1), jnp.stack(st_v, axis=1))
```

```python
import functools
import math

import jax
import jax.numpy as jnp
from jax import lax
from jax.experimental import pallas as pl
from jax.experimental.pallas import tpu as pltpu

F32 = jnp.float32
BF16 = jnp.bfloat16

D_MODEL = 1024
GRID_W = 64
SSM_GROUP = 16
SSM_GROUPS = D_MODEL // SSM_GROUP
SSM_STATE = 64
SSM_CHUNK = 16
SSM_ROW = SSM_CHUNK * SSM_GROUP
HEAD_DIM = 64
N_HEADS = D_MODEL // HEAD_DIM
N_KV_HEADS = 4
KV_REP = N_HEADS // N_KV_HEADS
KV_DIM = N_KV_HEADS * HEAD_DIM
WINDOW = 128
ATTN_BLOCK = 128
ROPE_BASE = 10000.0
ROPE_AXIS_DIM = HEAD_DIM // 2
NEG_INF = -1e30
PEER_HEADS = 8
PEER_NKEYS = 128
PEER_EXPERTS = PEER_NKEYS * PEER_NKEYS
PEER_TOPK = 16
PEER_KEY_DIM = 256
PEER_HALF = PEER_KEY_DIM // 2
LN_EPS = 1e-5

TOKEN_TILE = 512
EXPERT_BLOCK = 1024
SUB_EXPERTS = 256
V_ROWS = 512
MOD_ROWS = 16
BF16_SUBLANES = 16
VMEM_LIMIT = 56 * 1024 * 1024

_NT = (((1,), (1,)), ((), ()))


def _cparams(*sem):
    return pltpu.CompilerParams(dimension_semantics=sem, vmem_limit_bytes=VMEM_LIMIT)


def _gelu(x):
    return 0.5 * x * (1.0 + jnp.tanh(0.7978845608028654 * (x + 0.044715 * (x * x * x))))


def _layer_norm_rows(z, g, b):
    mu = jnp.mean(z, axis=-1, keepdims=True)
    zc = z - mu
    var = jnp.mean(zc * zc, axis=-1, keepdims=True)
    return zc * lax.rsqrt(var + LN_EPS) * g + b


def _mod_kernel(c_ref, w_ref, b_ref, o_ref):
    c = c_ref[...]
    s = c * jax.nn.sigmoid(c)
    o_ref[...] = jnp.dot(s.astype(BF16), w_ref[...].astype(BF16),
                         preferred_element_type=F32) + b_ref[...]


def _modulation_all(cond, w_mod, b_mod):
    depth = w_mod.shape[0]
    n_out = w_mod.shape[2]
    nb = n_out // D_MODEL
    out = pl.pallas_call(
        _mod_kernel,
        grid=(depth, nb),
        in_specs=[
            pl.BlockSpec((MOD_ROWS, D_MODEL), lambda i, j: (0, 0)),
            pl.BlockSpec((None, D_MODEL, D_MODEL), lambda i, j: (i, 0, j)),
            pl.BlockSpec((None, 1, D_MODEL), lambda i, j: (i, 0, j)),
        ],
        out_specs=pl.BlockSpec((None, MOD_ROWS, D_MODEL), lambda i, j: (i, 0, j)),
        out_shape=jax.ShapeDtypeStruct((depth, MOD_ROWS, n_out), F32),
        compiler_params=_cparams("arbitrary", "arbitrary"),
        name="modulation",
    )(cond, w_mod, b_mod.reshape(depth, 1, n_out))
    return out.reshape(depth, MOD_ROWS, nb, D_MODEL)


class _Tokens:
    def __init__(self, n_ctx_tokens, dec_batch, dec_seq):
        self.n_ctx = n_ctx_tokens
        self.dec_batch = dec_batch
        self.dec_seq = dec_seq
        self.total = n_ctx_tokens + dec_batch * dec_seq
        assert n_ctx_tokens % TOKEN_TILE == 0 and dec_seq % TOKEN_TILE == 0
        self.ctx_tiles = n_ctx_tokens // TOKEN_TILE
        self.tiles = self.total // TOKEN_TILE
        self.tiles_per_seq = dec_seq // TOKEN_TILE

    def mod_row(self, i):
        return jnp.where(i < self.ctx_tiles, self.dec_batch, (i - self.ctx_tiles) // self.tiles_per_seq)


def _mod_spec(tok):
    return pl.BlockSpec((1, 6, D_MODEL), lambda i: (tok.mod_row(i), 0, 0))


def _row_spec(width):
    return pl.BlockSpec((TOKEN_TILE, width), lambda i: (i, 0))


def _full_spec(shape):
    nd = len(shape)
    return pl.BlockSpec(shape, lambda i: (0,) * nd)


def _linear_kernel(x_ref, m_ref, w_ref, o_ref, *, shift_idx):
    m = m_ref[0]
    h = x_ref[...] * (1.0 + m[shift_idx + 1:shift_idx + 2]) + m[shift_idx:shift_idx + 1]
    o_ref[...] = jnp.dot(h.astype(BF16), w_ref[...], preferred_element_type=F32)


def _mod_linear(tok, x, mods, w, shift_idx, name):
    n = w.shape[1]
    return pl.pallas_call(
        functools.partial(_linear_kernel, shift_idx=shift_idx),
        grid=(tok.tiles,),
        in_specs=[_row_spec(D_MODEL), _mod_spec(tok), _full_spec((D_MODEL, n))],
        out_specs=_row_spec(n),
        out_shape=jax.ShapeDtypeStruct((tok.total, n), F32),
        compiler_params=_cparams("parallel"),
        name=name,
    )(x, mods, w)


def _post_norm_rows(x, out, m, gate_idx, g_ref, b_ref, alpha):
    z = alpha * x + (1.0 + m[gate_idx:gate_idx + 1]) * out
    return _layer_norm_rows(z, g_ref[...], b_ref[...])


def _mm_postnorm_kernel(a_ref, x_ref, m_ref, w_ref, g_ref, b_ref, o_ref, *, gate_idx, alpha):
    y = jnp.dot(a_ref[...].astype(BF16), w_ref[...], preferred_element_type=F32)
    o_ref[...] = _post_norm_rows(x_ref[...], y, m_ref[0], gate_idx, g_ref, b_ref, alpha)


def _mm_postnorm(tok, a, x, mods, w, g, b, gate_idx, alpha, name):
    return pl.pallas_call(
        functools.partial(_mm_postnorm_kernel, gate_idx=gate_idx, alpha=alpha),
        grid=(tok.tiles,),
        in_specs=[_row_spec(D_MODEL), _row_spec(D_MODEL), _mod_spec(tok),
                  _full_spec((D_MODEL, D_MODEL)), _full_spec((1, D_MODEL)), _full_spec((1, D_MODEL))],
        out_specs=_row_spec(D_MODEL),
        out_shape=jax.ShapeDtypeStruct((tok.total, D_MODEL), F32),
        compiler_params=_cparams("parallel"),
        name=name,
    )(a, x, mods, w, g.reshape(1, D_MODEL), b.reshape(1, D_MODEL))


def _postnorm_kernel(o_in_ref, x_ref, m_ref, g_ref, b_ref, o_ref, *, gate_idx, alpha):
    o_ref[...] = _post_norm_rows(x_ref[...], o_in_ref[...], m_ref[0], gate_idx, g_ref, b_ref, alpha)


def _postnorm(tok, out, x, mods, g, b, gate_idx, alpha, name):
    return pl.pallas_call(
        functools.partial(_postnorm_kernel, gate_idx=gate_idx, alpha=alpha),
        grid=(tok.tiles,),
        in_specs=[_row_spec(D_MODEL), _row_spec(D_MODEL), _mod_spec(tok),
                  _full_spec((1, D_MODEL)), _full_spec((1, D_MODEL))],
        out_specs=_row_spec(D_MODEL),
        out_shape=jax.ShapeDtypeStruct((tok.total, D_MODEL), F32),
        compiler_params=_cparams("parallel"),
        name=name,
    )(out, x, mods, g.reshape(1, D_MODEL), b.reshape(1, D_MODEL))


def _s5_operators(a_re, a_im, log_dt, b_re, b_im, c_re, c_im):
    hi = lax.Precision.HIGHEST
    dt = jnp.exp(log_dt)[..., None]
    ar, ai = a_re * dt, a_im * dt
    k = jnp.arange(SSM_CHUNK + 1, dtype=F32)[:, None, None, None]
    mag = jnp.exp(k * ar[None])
    pw_re, pw_im = mag * jnp.cos(k * ai[None]), mag * jnp.sin(k * ai[None])
    l_re, l_im = pw_re[1], pw_im[1]
    den = a_re * a_re + a_im * a_im
    q_re = ((l_re - 1.0) * a_re + l_im * a_im) / den
    q_im = (l_im * a_re - (l_re - 1.0) * a_im) / den
    bt_re = q_re[..., None] * b_re - q_im[..., None] * b_im
    bt_im = q_re[..., None] * b_im + q_im[..., None] * b_re
    d_re = pw_re[..., None] * bt_re[None] - pw_im[..., None] * bt_im[None]
    d_im = pw_re[..., None] * bt_im[None] + pw_im[..., None] * bt_re[None]
    pw_re_c, pw_im_c = pw_re[:, :, :, None, :], pw_im[:, :, :, None, :]
    e_re = c_re[None] * pw_re_c - c_im[None] * pw_im_c
    e_im = c_re[None] * pw_im_c + c_im[None] * pw_re_c
    kern = (jnp.einsum('dgip,kdgpj->kdgij', c_re, d_re, precision=hi)
            - jnp.einsum('dgip,kdgpj->kdgij', c_im, d_im, precision=hi))
    s = jnp.arange(SSM_CHUNK)
    lag = s[None, :] - s[:, None]
    kf = kern[jnp.clip(lag, 0, SSM_CHUNK), 0]
    kb = kern[jnp.clip(-lag, 0, SSM_CHUNK), 1]
    mix = (jnp.where((lag >= 0)[:, :, None, None, None], kf, 0.0)
           + jnp.where((lag <= 0)[:, :, None, None, None], kb, 0.0))
    g = SSM_GROUPS
    m_op = jnp.transpose(mix, (2, 0, 4, 1, 3)).reshape(g, SSM_ROW, SSM_ROW)
    pf = (SSM_CHUNK - 1) - s
    win = jnp.stack([d_re[pf, 0], d_re[s, 1], d_im[pf, 0], d_im[s, 1]], axis=0)
    win_op = jnp.transpose(win, (2, 1, 4, 0, 3)).reshape(g, SSM_ROW, 4 * SSM_STATE)
    wout = jnp.stack([e_re[s + 1, 0], e_re[SSM_CHUNK - s, 1], -e_im[s + 1, 0], -e_im[SSM_CHUNK - s, 1]],
                     axis=0)
    wout_op = jnp.transpose(wout, (2, 0, 4, 1, 3)).reshape(g, 4 * SSM_STATE, SSM_ROW)
    lam = jnp.stack([jnp.concatenate([pw_re[SSM_CHUNK, 0], pw_re[SSM_CHUNK, 1]], axis=-1),
                     jnp.concatenate([pw_im[SSM_CHUNK, 0], pw_im[SSM_CHUNK, 1]], axis=-1)], axis=1)
    return m_op, win_op, wout_op, lam


def _s5_core_kernel(u_ref, h0_ref, m_ref, win_ref, wout_ref, lam_ref, y_ref, s_ref, z_scr, p_scr,
                    *, n_chunks, batch):
    hi = lax.Precision.HIGHEST
    p2 = 2 * SSM_STATE
    u = u_ref[...]
    z_scr[...] = jnp.dot(u, win_ref[...], preferred_element_type=F32, precision=hi)
    l_re = lam_ref[0:1, :]
    l_im = lam_ref[1:2, :]
    is_fwd = lax.broadcasted_iota(jnp.int32, (batch, p2), 1) < SSM_STATE

    def step(k, carry):
        s_re, s_im = carry
        rf = pl.multiple_of(k * batch, batch)
        rb = pl.multiple_of((n_chunks - 1 - k) * batch, batch)
        zf = z_scr[pl.ds(rf, batch), :]
        zb = z_scr[pl.ds(rb, batch), :]
        z_re = jnp.where(is_fwd, zf[:, :p2], zb[:, :p2])
        z_im = jnp.where(is_fwd, zf[:, p2:], zb[:, p2:])
        p_scr[pl.ds(rf, batch), 0:SSM_STATE] = s_re[:, :SSM_STATE]
        p_scr[pl.ds(rf, batch), p2:p2 + SSM_STATE] = s_im[:, :SSM_STATE]
        p_scr[pl.ds(rb, batch), SSM_STATE:p2] = s_re[:, SSM_STATE:]
        p_scr[pl.ds(rb, batch), p2 + SSM_STATE:] = s_im[:, SSM_STATE:]
        n_re = l_re * s_re - l_im * s_im + z_re
        n_im = l_re * s_im + l_im * s_re + z_im
        return n_re, n_im

    h0 = h0_ref[...]
    s_re, s_im = lax.fori_loop(0, n_chunks, step, (h0[:, :p2], h0[:, p2:]))
    s_ref[:, :p2] = s_re
    s_ref[:, p2:] = s_im
    y_ref[...] = (jnp.dot(u, m_ref[...], preferred_element_type=F32, precision=hi)
                  + jnp.dot(p_scr[...], wout_ref[...], preferred_element_type=F32, precision=hi))


def _s5_core(ug, h0, ops, n_chunks, batch, name):
    m_op, win_op, wout_op, lam = ops
    g, rows, _ = ug.shape
    st = 4 * SSM_STATE
    return pl.pallas_call(
        functools.partial(_s5_core_kernel, n_chunks=n_chunks, batch=batch),
        grid=(g,),
        in_specs=[
            pl.BlockSpec((None, rows, SSM_ROW), lambda i: (i, 0, 0)),
            pl.BlockSpec((None, batch, st), lambda i: (i, 0, 0)),
            pl.BlockSpec((None, SSM_ROW, SSM_ROW), lambda i: (i, 0, 0)),
            pl.BlockSpec((None, SSM_ROW, st), lambda i: (i, 0, 0)),
            pl.BlockSpec((None, st, SSM_ROW), lambda i: (i, 0, 0)),
            pl.BlockSpec((None, 2, 2 * SSM_STATE), lambda i: (i, 0, 0)),
        ],
        out_specs=[
            pl.BlockSpec((None, rows, SSM_ROW), lambda i: (i, 0, 0)),
            pl.BlockSpec((None, batch, st), lambda i: (i, 0, 0)),
        ],
        out_shape=[jax.ShapeDtypeStruct((g, rows, SSM_ROW), F32),
                   jax.ShapeDtypeStruct((g, batch, st), F32)],
        scratch_shapes=[pltpu.VMEM((rows, st), F32), pltpu.VMEM((rows, st), F32)],
        compiler_params=_cparams("parallel"),
        name=name,
    )(ug, h0, m_op, win_op, wout_op, lam)


def _to_chunks(u, batch, seq):
    n_chunks = seq // SSM_CHUNK
    t = u.reshape(batch, n_chunks, SSM_CHUNK, SSM_GROUPS, SSM_GROUP)
    return jnp.transpose(t, (3, 1, 0, 2, 4)).reshape(SSM_GROUPS, n_chunks * batch, SSM_ROW)


def _from_chunks(y, batch, seq):
    n_chunks = seq // SSM_CHUNK
    t = y.reshape(SSM_GROUPS, n_chunks, batch, SSM_CHUNK, SSM_GROUP)
    return jnp.transpose(t, (2, 1, 3, 0, 4)).reshape(batch * seq, D_MODEL)


def _state_to_cols(h_re, h_im):
    cols = jnp.concatenate([h_re[:, 0], h_re[:, 1], h_im[:, 0], h_im[:, 1]], axis=-1)
    return jnp.transpose(cols, (1, 0, 2))


def _cols_to_state(s):
    t = jnp.transpose(s, (1, 0, 2)).reshape(s.shape[1], SSM_GROUPS, 2, 2, SSM_STATE)
    t = jnp.transpose(t, (2, 0, 3, 1, 4))
    return t[0], t[1]


def _s5_tail_kernel(y_ref, u_ref, x_ref, m_ref, d_ref, wg_ref, wo_ref, g_ref, b_ref, o_ref, *, alpha):
    v = _gelu(y_ref[...] + d_ref[...] * u_ref[...])
    gl = jnp.dot(v.astype(BF16), wg_ref[...], preferred_element_type=F32)
    a = gl[:, :D_MODEL] * jax.nn.sigmoid(gl[:, D_MODEL:])
    out = jnp.dot(a.astype(BF16), wo_ref[...], preferred_element_type=F32)
    o_ref[...] = _post_norm_rows(x_ref[...], out, m_ref[0], 2, g_ref, b_ref, alpha)


def _s5_tail(tok, y, u, x, mods, d, w_glu, w_out, g, b, alpha):
    return pl.pallas_call(
        functools.partial(_s5_tail_kernel, alpha=alpha),
        grid=(tok.tiles,),
        in_specs=[_row_spec(D_MODEL), _row_spec(D_MODEL), _row_spec(D_MODEL), _mod_spec(tok),
                  _full_spec((1, D_MODEL)), _full_spec((D_MODEL, 2 * D_MODEL)),
                  _full_spec((D_MODEL, D_MODEL)), _full_spec((1, D_MODEL)), _full_spec((1, D_MODEL))],
        out_specs=_row_spec(D_MODEL),
        out_shape=jax.ShapeDtypeStruct((tok.total, D_MODEL), F32),
        compiler_params=_cparams("parallel"),
        name="s5_tail",
    )(y, u, x, mods, d.reshape(1, D_MODEL), w_glu, w_out, g.reshape(1, D_MODEL), b.reshape(1, D_MODEL))


QKV_COLS = 3 * D_MODEL


def _attn_weights(w_qkv):
    wq, wk, wv = w_qkv[:, :D_MODEL], w_qkv[:, D_MODEL:D_MODEL + KV_DIM], w_qkv[:, D_MODEL + KV_DIM:]
    half = ROPE_AXIS_DIM // 2

    def partner(w):
        n = w.shape[1]
        t = w.reshape(D_MODEL, n // ROPE_AXIS_DIM, 2, half)
        return jnp.stack([-t[:, :, 1], t[:, :, 0]], axis=2).reshape(D_MODEL, n)

    pad = jnp.zeros((D_MODEL, QKV_COLS - 2 * D_MODEL - 3 * KV_DIM), w_qkv.dtype)
    return jnp.concatenate([wq, partner(wq), wk, wv, partner(wk), pad], axis=1).astype(BF16)


def _rope_tables(seq):
    half = ROPE_AXIS_DIM // 2
    rows = seq // GRID_W
    row = jnp.repeat(jnp.arange(rows, dtype=F32), GRID_W)
    col = jnp.tile(jnp.arange(GRID_W, dtype=F32), rows)
    inv = ROPE_BASE ** (-jnp.arange(half, dtype=F32) / half)
    ang_r = row[:, None] * inv[None, :]
    ang_c = col[:, None] * inv[None, :]
    ang = jnp.concatenate([ang_r, ang_r, ang_c, ang_c], axis=-1)
    cos = jnp.tile(jnp.cos(ang), (1, N_KV_HEADS))
    sin = jnp.tile(jnp.sin(ang), (1, N_KV_HEADS))
    return jnp.concatenate([cos, sin], axis=-1)


def _softmax_heads(q, k_list, v_list, mask_list, sink_ref, o_ref, heads_per_pass):
    scale = HEAD_DIM ** -0.5
    tq = q.shape[0]
    masks = [None if m is None else jnp.concatenate([m] * heads_per_pass, axis=0) for m in mask_list]
    for first in range(0, N_HEADS, heads_per_pass):
        heads = range(first, first + heads_per_pass)
        g = first // KV_REP
        qg = jnp.concatenate([q[:, h * HEAD_DIM:(h + 1) * HEAD_DIM] for h in heads], axis=0).astype(BF16)
        sink = jnp.concatenate([jnp.full((tq, 1), sink_ref[h], F32) for h in heads], axis=0)
        cols = slice(g * HEAD_DIM, (g + 1) * HEAD_DIM)
        logits = []
        for k, mask in zip(k_list, masks):
            s = lax.dot_general(qg, k[:, cols], _NT, preferred_element_type=F32) * scale
            if mask is not None:
                s = jnp.where(mask, s, NEG_INF)
            logits.append(s)
        m = sink
        for s in logits:
            m = jnp.maximum(m, jnp.max(s, axis=-1, keepdims=True))
        denom = jnp.exp(sink - m)
        acc = jnp.zeros((heads_per_pass * tq, HEAD_DIM), F32)
        for s, v in zip(logits, v_list):
            p = jnp.exp(s - m)
            denom = denom + jnp.sum(p, axis=-1, keepdims=True)
            acc = acc + jnp.dot(p.astype(BF16), v[:, cols], preferred_element_type=F32)
        out = acc / denom
        for r, h in enumerate(heads):
            o_ref[:, h * HEAD_DIM:(h + 1) * HEAD_DIM] = out[r * tq:(r + 1) * tq]


def _attn_ctx_kernel(sink_ref, q_ref, kv_ref, o_ref):
    kv = kv_ref[...]
    k = kv[:, :KV_DIM].astype(BF16)
    v = kv[:, KV_DIM:2 * KV_DIM].astype(BF16)
    _softmax_heads(q_ref[...], [k], [v], [None], sink_ref, o_ref, 1)


def _attn_context(qkv, sink, n_batch, seq):
    return pl.pallas_call(
        _attn_ctx_kernel,
        grid=(n_batch,),
        in_specs=[
            pl.BlockSpec(memory_space=pltpu.SMEM),
            pl.BlockSpec((seq, D_MODEL), lambda b: (b, 0)),
            pl.BlockSpec((seq, D_MODEL), lambda b: (b, 2)),
        ],
        out_specs=pl.BlockSpec((seq, D_MODEL), lambda b: (b, 0)),
        out_shape=jax.ShapeDtypeStruct((n_batch * seq, D_MODEL), F32),
        compiler_params=_cparams("parallel"),
        name="attn_context",
    )(sink, qkv, qkv)


def _attn_lat_kernel(sink_ref, q_ref, qp_ref, kv0_ref, kv1_ref, kv2_ref, t0_ref, t1_ref, t2_ref,
                     kc_ref, vc_ref, o_ref, *, n_qblocks):
    qb = pl.program_id(1)
    t1 = t1_ref[...]
    cos_q = jnp.concatenate([t1[:, :KV_DIM]] * KV_REP, axis=-1)
    sin_q = jnp.concatenate([t1[:, KV_DIM:]] * KV_REP, axis=-1)
    q = q_ref[...] * cos_q + qp_ref[...] * sin_q

    def rope_k(kv_ref, t_ref):
        kv = kv_ref[...]
        t = t_ref[...]
        k = kv[:, :KV_DIM] * t[:, :KV_DIM] + kv[:, 2 * KV_DIM:3 * KV_DIM] * t[:, KV_DIM:]
        return k.astype(BF16), kv[:, KV_DIM:2 * KV_DIM].astype(BF16)

    k0, v0 = rope_k(kv0_ref, t0_ref)
    k1, v1 = rope_k(kv1_ref, t1_ref)
    k2, v2 = rope_k(kv2_ref, t2_ref)
    k_loc = jnp.concatenate([k0, k1, k2], axis=0)
    v_loc = jnp.concatenate([v0, v1, v2], axis=0)
    span = 3 * ATTN_BLOCK
    qi = lax.broadcasted_iota(jnp.int32, (ATTN_BLOCK, span), 0)
    kj = lax.broadcasted_iota(jnp.int32, (ATTN_BLOCK, span), 1) - ATTN_BLOCK
    ok = (jnp.abs(kj - qi) <= WINDOW)
    ok = ok & ((kj >= 0) | (qb > 0)) & ((kj < ATTN_BLOCK) | (qb < n_qblocks - 1))
    kc = kc_ref[0].astype(BF16)
    vc = vc_ref[0].astype(BF16)
    _softmax_heads(q, [kc, k_loc], [vc, v_loc], [None, ok], sink_ref, o_ref, KV_REP)


def _attn_latent(qkv, tables, k_ctx, v_ctx, sink, tok):
    nqb = tok.dec_seq // ATTN_BLOCK
    base = tok.n_ctx // ATTN_BLOCK
    past = k_ctx.shape[1]

    def rows(b, i, off):
        return base + b * nqb + jnp.clip(i + off, 0, nqb - 1)

    def kv_spec(off):
        return pl.BlockSpec((ATTN_BLOCK, D_MODEL), lambda b, i: (rows(b, i, off), 2))

    def tab_spec(off):
        return pl.BlockSpec((ATTN_BLOCK, 2 * KV_DIM), lambda b, i: (jnp.clip(i + off, 0, nqb - 1), 0))

    return pl.pallas_call(
        functools.partial(_attn_lat_kernel, n_qblocks=nqb),
        grid=(tok.dec_batch, nqb),
        in_specs=[
            pl.BlockSpec(memory_space=pltpu.SMEM),
            pl.BlockSpec((ATTN_BLOCK, D_MODEL), lambda b, i: (rows(b, i, 0), 0)),
            pl.BlockSpec((ATTN_BLOCK, D_MODEL), lambda b, i: (rows(b, i, 0), 1)),
            kv_spec(-1), kv_spec(0), kv_spec(1),
            tab_spec(-1), tab_spec(0), tab_spec(1),
            pl.BlockSpec((1, past, KV_DIM), lambda b, i: (b, 0, 0)),
            pl.BlockSpec((1, past, KV_DIM), lambda b, i: (b, 0, 0)),
        ],
        out_specs=pl.BlockSpec((ATTN_BLOCK, D_MODEL), lambda b, i: (b * nqb + i, 0)),
        out_shape=jax.ShapeDtypeStruct((tok.dec_batch * tok.dec_seq, D_MODEL), F32),
        compiler_params=_cparams("parallel", "arbitrary"),
        name="attn_latent",
    )(sink, qkv, qkv, qkv, qkv, qkv, tables, tables, tables, k_ctx, v_ctx)


ROUTE_TILE = 256
ROUTE_LANES = 128
NKH = PEER_NKEYS * PEER_HEADS


def _oddeven_merge_sort_pairs(n):
    pairs = []

    def merge(lo, hi, r):
        step = r * 2
        if step < hi - lo:
            merge(lo, hi, step)
            merge(lo + r, hi, step)
            pairs.extend((i, i + r) for i in range(lo + r, hi - r, step))
        else:
            pairs.append((lo, lo + r))

    def sort(lo, hi):
        if hi - lo >= 1:
            mid = lo + (hi - lo) // 2
            sort(lo, mid)
            sort(mid + 1, hi)
            merge(lo, hi, 1)

    sort(0, n - 1)
    return pairs


_SORT16 = _oddeven_merge_sort_pairs(PEER_TOPK)


def _exchange(x, i, j):
    x[i], x[j] = jnp.maximum(x[i], x[j]), jnp.minimum(x[i], x[j])


def _top16_sorted(vals):
    groups = []
    for g in range(0, len(vals), PEER_TOPK):
        x = list(vals[g:g + PEER_TOPK])
        for i, j in _SORT16:
            _exchange(x, i, j)
        groups.append(x)
    while len(groups) > 1:
        merged = []
        for a, b in zip(groups[0::2], groups[1::2]):
            x = [jnp.maximum(a[i], b[PEER_TOPK - 1 - i]) for i in range(PEER_TOPK)]
            for d in (8, 4, 2, 1):
                for i in range(PEER_TOPK):
                    if not i & d:
                        _exchange(x, i, i + d)
            merged.append(x)
        groups = merged
    return groups[0]


def _rank_bits(s, v):
    b3 = v[7] > s
    b2 = jnp.where(b3, v[11], v[3]) > s
    b1 = jnp.where(b3, jnp.where(b2, v[13], v[9]), jnp.where(b2, v[5], v[1])) > s
    hi = jnp.where(b2, jnp.where(b1, v[14], v[12]), jnp.where(b1, v[10], v[8]))
    lo = jnp.where(b2, jnp.where(b1, v[6], v[4]), jnp.where(b1, v[2], v[0]))
    b0 = jnp.where(b3, hi, lo) > s
    return (b3, b2, b1, b0), v[15] > s


def _select16(bits, leaves):
    b3, b2, b1, b0 = bits
    lvl = [jnp.where(b0, leaves[2 * i + 1], leaves[2 * i]) for i in range(8)]
    lvl = [jnp.where(b1, lvl[2 * i + 1], lvl[2 * i]) for i in range(4)]
    lvl = [jnp.where(b2, lvl[2 * i + 1], lvl[2 * i]) for i in range(2)]
    return jnp.where(b3, lvl[1], lvl[0])


def _exact_top16(s_ref, rank_ref, top_ref):
    n = s_ref.shape[0]
    rank_ref[...] = jnp.full(rank_ref.shape, float(PEER_TOPK), F32)

    def body(k, carry):
        s = s_ref[...]
        iota = lax.broadcasted_iota(jnp.int32, s.shape, 0).astype(F32)
        m = jnp.max(s, axis=0)
        idx = jnp.min(jnp.where(s == m[None], iota, float(n)), axis=0)
        hit = iota == idx[None]
        rank_ref[...] = jnp.where(hit, lax.convert_element_type(k, F32), rank_ref[...])
        s_ref[...] = jnp.where(hit, -jnp.inf, s)
        top_ref[k] = m
        return carry

    lax.fori_loop(0, PEER_TOPK, body, 0)


def _peer_route_kernel(q_ref, k1_ref, k2_ref, r2_ref, w2_ref, n1_ref, w1_ref,
                       r2_st, w2_st, n1_st, w1_st, sa_scr, ra_scr, sb_scr, rb_scr, ta_scr, tb_scr):
    q = q_ref[...].astype(BF16)
    half = PEER_HEADS * PEER_HALF
    s1_all = lax.dot_general(k1_ref[...], q[:, :half], _NT, preferred_element_type=F32)
    s2_all = lax.dot_general(k2_ref[...], q[:, half:], _NT, preferred_element_type=F32)
    slab = (PEER_HEADS, ROUTE_LANES)
    ninf = jnp.full(slab, -jnp.inf, F32)

    for j in range(ROUTE_TILE // ROUTE_LANES):
        lanes = slice(j * ROUTE_LANES, (j + 1) * ROUTE_LANES)
        s1 = [s1_all[k * PEER_HEADS:(k + 1) * PEER_HEADS, lanes] for k in range(PEER_NKEYS)]
        s2 = [s2_all[k * PEER_HEADS:(k + 1) * PEER_HEADS, lanes] for k in range(PEER_NKEYS)]
        v1 = _top16_sorted(s1)
        v2 = _top16_sorted(s2)
        cand = [[v1[a] + v2[b] for b in range(PEER_TOPK // (a + 1))] for a in range(PEER_TOPK)]
        flat = [c for row in cand for c in row]
        flat += [ninf] * (-len(flat) % PEER_TOPK)
        top = _top16_sorted(flat)
        theta = top[PEER_TOPK - 1]
        nb = []
        for row in cand:
            cnt = jnp.zeros(slab, F32)
            for c in row:
                cnt = cnt + jnp.where(c >= theta, 1.0, 0.0)
            nb.append(cnt)
        z = jnp.zeros(slab, F32)
        for k in range(PEER_TOPK):
            z = z + jnp.exp(top[k] - top[0])
        inv_z = 1.0 / z
        total = nb[0]
        for a in range(1, PEER_TOPK):
            total = total + nb[a]
        tied = total != float(PEER_TOPK)
        for k in range(PEER_TOPK - 1):
            tied = tied | ((v1[k] == v1[k + 1]) & (nb[k] != nb[k + 1])) | ((v2[k] == v2[k + 1]) & (nb[0] > float(k)))
        in1 = jnp.zeros(slab, F32)
        in2 = jnp.zeros(slab, F32)
        for k in range(PEER_NKEYS):
            rows = slice(k * PEER_HEADS, (k + 1) * PEER_HEADS)
            bits, out1 = _rank_bits(s1[k], v1)
            n1_st[rows, :] = jnp.where(out1, 0.0, _select16(bits, nb))
            w1_st[rows, :] = jnp.exp(s1[k] - v1[0]) * inv_z
            in1 = in1 + jnp.where(out1, 0.0, 1.0)
            (b3, b2, b1, b0), out2 = _rank_bits(s2[k], v2)
            r = (jnp.where(b3, 8.0, 0.0) + jnp.where(b2, 4.0, 0.0)) + (jnp.where(b1, 2.0, 0.0) + jnp.where(b0, 1.0, 0.0))
            r2_st[rows, :] = jnp.where(out2, float(PEER_TOPK), r)
            w2_st[rows, :] = jnp.exp(s2[k] - v2[0])
            in2 = in2 + jnp.where(out2, 0.0, 1.0)
        tied = tied | ((in1 != float(PEER_TOPK)) & (nb[PEER_TOPK - 1] != 0.0))
        tied = tied | ((in2 != float(PEER_TOPK)) & (nb[0] == float(PEER_TOPK)))
        n_tied = jnp.sum(jnp.where(tied, 1.0, 0.0))

        @pl.when(n_tied > 0.0)
        def _():
            shape3 = (PEER_NKEYS, PEER_HEADS, ROUTE_LANES)
            sa_scr[...] = s1_all[:, lanes].reshape(shape3)
            _exact_top16(sa_scr, ra_scr, ta_scr)
            sb_scr[...] = s2_all[:, lanes].reshape(shape3)
            _exact_top16(sb_scr, rb_scr, tb_scr)
            r1 = ra_scr[...]
            r2_st[...] = rb_scr[...].reshape(NKH, ROUTE_LANES)
            e1 = jnp.exp(s1_all[:, lanes].reshape(shape3) - ta_scr[0][None])
            w2_st[...] = jnp.exp(s2_all[:, lanes] - jnp.tile(tb_scr[0], (PEER_NKEYS, 1)))
            n_cand = len(flat)
            pos = 0
            for a in range(PEER_TOPK):
                width = PEER_TOPK // (a + 1)
                sa_scr[pos:pos + width] = ta_scr[a][None] + tb_scr[0:width]
                pos += width
            sa_scr[pos:n_cand] = jnp.full((n_cand - pos,) + slab, -jnp.inf, F32)
            _exact_top16(sa_scr.at[0:n_cand], sb_scr.at[0:n_cand], ta_scr)
            picked = jnp.where(sb_scr[0:n_cand] < float(PEER_TOPK), 1.0, 0.0)
            zz = jnp.zeros(slab, F32)
            for k in range(PEER_TOPK):
                zz = zz + jnp.exp(ta_scr[k] - ta_scr[0])
            n1 = jnp.zeros(shape3, F32)
            pos = 0
            for a in range(PEER_TOPK):
                width = PEER_TOPK // (a + 1)
                cnt = jnp.sum(picked[pos:pos + width], axis=0)
                n1 = jnp.where(r1 == float(a), cnt[None], n1)
                pos += width
            n1_st[...] = n1.reshape(NKH, ROUTE_LANES)
            w1_st[...] = (e1 / zz[None]).reshape(NKH, ROUTE_LANES)

        n1_ref[:, :, lanes] = n1_st[...].reshape(PEER_NKEYS, PEER_HEADS, ROUTE_LANES)
        w1_ref[:, :, lanes] = w1_st[...].reshape(PEER_NKEYS, PEER_HEADS, ROUTE_LANES)
        for h in range(PEER_HEADS):
            r2_ref[h, :, lanes] = r2_st[pl.ds(h, PEER_NKEYS, stride=PEER_HEADS), :].astype(BF16)
            w2_ref[h, :, lanes] = w2_st[pl.ds(h, PEER_NKEYS, stride=PEER_HEADS), :].astype(BF16)


def _route_weights(w_q, keys):
    w = w_q.reshape(D_MODEL, PEER_HEADS, 2, PEER_HALF)
    w = jnp.transpose(w, (0, 2, 1, 3)).reshape(D_MODEL, PEER_HEADS * PEER_KEY_DIM)
    eye = jnp.eye(PEER_HEADS, dtype=keys.dtype)
    big = jnp.einsum('phkd,hg->pkhgd', keys, eye).reshape(2, NKH, PEER_HEADS * PEER_HALF)
    return w.astype(BF16), big.astype(BF16)


def _peer_route(q, key_blocks, n_tokens):
    half = PEER_HEADS * PEER_HALF
    shape_kh = jax.ShapeDtypeStruct((PEER_NKEYS, PEER_HEADS, n_tokens), F32)
    shape_hk = jax.ShapeDtypeStruct((PEER_HEADS, PEER_NKEYS, n_tokens), BF16)
    spec_kh = pl.BlockSpec((PEER_NKEYS, PEER_HEADS, ROUTE_TILE), lambda i: (0, 0, i))
    spec_hk = pl.BlockSpec((PEER_HEADS, PEER_NKEYS, ROUTE_TILE), lambda i: (0, 0, i))
    stage = pltpu.VMEM((NKH, ROUTE_LANES), F32)
    cand3 = pltpu.VMEM((PEER_NKEYS, PEER_HEADS, ROUTE_LANES), F32)
    top3 = pltpu.VMEM((PEER_TOPK, PEER_HEADS, ROUTE_LANES), F32)
    return pl.pallas_call(
        _peer_route_kernel,
        grid=(n_tokens // ROUTE_TILE,),
        in_specs=[
            pl.BlockSpec((ROUTE_TILE, 2 * half), lambda i: (i, 0)),
            pl.BlockSpec((None, NKH, half), lambda i: (0, 0, 0)),
            pl.BlockSpec((None, NKH, half), lambda i: (1, 0, 0)),
        ],
        out_specs=[spec_hk, spec_hk, spec_kh, spec_kh],
        out_shape=[shape_hk, shape_hk, shape_kh, shape_kh],
        scratch_shapes=[stage, stage, stage, stage, cand3, cand3, cand3, cand3, top3, top3],
        compiler_params=_cparams("parallel"),
        name="peer_route",
    )(q, key_blocks, key_blocks)


def _peer_dense_kernel(x_ref, m_ref, *refs):
    n_sub = EXPERT_BLOCK // SUB_EXPERTS
    n_out = D_MODEL // V_ROWS
    u_refs = refs[:n_sub]
    vt_refs = refs[n_sub:n_sub + n_out]
    r2_ref, w2_ref, n1_ref, w1_ref, o_ref, h_scr, acc_scr = refs[n_sub + n_out:]
    c = pl.program_id(1)

    @pl.when(c == 0)
    def _():
        m = m_ref[0]
        h_scr[...] = (x_ref[...] * (1.0 + m[4:5]) + m[3:4]).T.astype(BF16)
        acc_scr[...] = jnp.zeros_like(acc_scr)

    sub = BF16_SUBLANES
    zero = jnp.zeros((), BF16)
    keys_per_sub = SUB_EXPERTS // PEER_NKEYS
    parts = []
    for sb in range(n_sub):
        s = jnp.dot(u_refs[sb][...], h_scr[...], preferred_element_type=F32)
        act = _gelu(s.astype(BF16))
        for cc in range(keys_per_sub):
            key1 = c * (EXPERT_BLOCK // PEER_NKEYS) + sb * keys_per_sub + cc
            gate = jnp.zeros((PEER_NKEYS // sub, sub, TOKEN_TILE), BF16)
            for h in range(PEER_HEADS):
                n1 = jnp.broadcast_to(n1_ref[key1, pl.ds(h, 1), :], (sub, TOKEN_TILE)).astype(BF16)
                w1 = jnp.broadcast_to(w1_ref[key1, pl.ds(h, 1), :], (sub, TOKEN_TILE)).astype(BF16)
                r2 = r2_ref[h].reshape(PEER_NKEYS // sub, sub, TOKEN_TILE)
                w2 = w2_ref[h].reshape(PEER_NKEYS // sub, sub, TOKEN_TILE)
                gate = gate + jnp.where(r2 < n1[None], w2, zero) * w1[None]
            parts.append(act[cc * PEER_NKEYS:(cc + 1) * PEER_NKEYS] * gate.reshape(PEER_NKEYS, TOKEN_TILE))
    a = jnp.concatenate(parts, axis=0)
    for k in range(n_out):
        rows = slice(k * V_ROWS, (k + 1) * V_ROWS)
        acc_scr[rows, :] += jnp.dot(vt_refs[k][...], a, preferred_element_type=F32)

    @pl.when(c == pl.num_programs(1) - 1)
    def _():
        o_ref[...] = acc_scr[...].T


def _peer_dense(tok, x, mods, u_tab, vt_tab, route):
    r2, w2, n1, w1 = route
    n_sub = EXPERT_BLOCK // SUB_EXPERTS
    n_out = D_MODEL // V_ROWS
    spec_hk = pl.BlockSpec((PEER_HEADS, PEER_NKEYS, TOKEN_TILE), lambda i, c: (0, 0, i))
    spec_kh = pl.BlockSpec((PEER_NKEYS, PEER_HEADS, TOKEN_TILE), lambda i, c: (0, 0, i))
    u_specs = [pl.BlockSpec((SUB_EXPERTS, D_MODEL), functools.partial(lambda i, c, k: (c * n_sub + k, 0), k=k))
               for k in range(n_sub)]
    vt_specs = [pl.BlockSpec((V_ROWS, EXPERT_BLOCK), functools.partial(lambda i, c, k: (k, c), k=k))
                for k in range(n_out)]
    return pl.pallas_call(
        _peer_dense_kernel,
        grid=(tok.tiles, PEER_EXPERTS // EXPERT_BLOCK),
        in_specs=[
            pl.BlockSpec((TOKEN_TILE, D_MODEL), lambda i, c: (i, 0)),
            pl.BlockSpec((1, 6, D_MODEL), lambda i, c: (tok.mod_row(i), 0, 0)),
            *u_specs, *vt_specs,
            spec_hk, spec_hk, spec_kh, spec_kh,
        ],
        out_specs=pl.BlockSpec((TOKEN_TILE, D_MODEL), lambda i, c: (i, 0)),
        out_shape=jax.ShapeDtypeStruct((tok.total, D_MODEL), F32),
        scratch_shapes=[pltpu.VMEM((D_MODEL, TOKEN_TILE), BF16), pltpu.VMEM((D_MODEL, TOKEN_TILE), F32)],
        compiler_params=_cparams("parallel", "arbitrary"),
        name="peer_dense",
    )(x, mods, *([u_tab] * n_sub), *([vt_tab] * n_out), r2, w2, n1, w1)


def kernel(x_prompt, x_sample, state_ssm_re, state_ssm_im, cache_k, cache_v, c, c_ctx, w_mod, b_mod, ln_g, ln_b, ssm_w_in, ssm_a_re, ssm_a_im, ssm_log_dt, ssm_b_re, ssm_b_im, ssm_c_re, ssm_c_im, ssm_d, ssm_w_glu, ssm_w_out, attn_w_qkv, attn_sink, attn_w_out, peer_w_q, peer_keys, peer_u, peer_v):
    depth = w_mod.shape[0]
    n_batch, seq, _ = x_prompt.shape
    dec_batch, dec_seq, _ = x_sample.shape
    tok = _Tokens(n_batch * seq, dec_batch, dec_seq)
    alpha = (2 * depth) ** 0.25

    cond = jnp.concatenate([c, c_ctx[None, :],
                            jnp.zeros((MOD_ROWS - dec_batch - 1, D_MODEL), F32)], axis=0)
    mods_all = _modulation_all(cond, w_mod, b_mod)

    x = jnp.concatenate([x_prompt.reshape(-1, D_MODEL), x_sample.reshape(-1, D_MODEL)], axis=0)
    st_re, st_im, st_k, st_v = [], [], [], []
    for i in range(depth):
        j = i // 2
        mods = mods_all[i]
        if i % 2 == 0:
            u = _mod_linear(tok, x, mods, ssm_w_in[j].astype(BF16), 0, "s5_in")
            ops = _s5_operators(ssm_a_re[j], ssm_a_im[j], ssm_log_dt[j], ssm_b_re[j], ssm_b_im[j],
                                ssm_c_re[j], ssm_c_im[j])
            zeros = jnp.zeros((SSM_GROUPS, n_batch, 4 * SSM_STATE), F32)
            y_c, s_c = _s5_core(_to_chunks(u[:tok.n_ctx], n_batch, seq), zeros, ops,
                                seq // SSM_CHUNK, n_batch, "s5_core_context")
            h0 = _state_to_cols(state_ssm_re[:, j], state_ssm_im[:, j])
            y_s, _ = _s5_core(_to_chunks(u[tok.n_ctx:], dec_batch, dec_seq), h0, ops,
                              dec_seq // SSM_CHUNK, dec_batch, "s5_core_latent")
            y = jnp.concatenate([_from_chunks(y_c, n_batch, seq), _from_chunks(y_s, dec_batch, dec_seq)], axis=0)
            s_re, s_im = _cols_to_state(s_c)
            st_re.append(s_re)
            st_im.append(s_im)
            x = _s5_tail(tok, y, u, x, mods, ssm_d[j], ssm_w_glu[j].astype(BF16), ssm_w_out[j].astype(BF16),
                         ln_g[i, 0], ln_b[i, 0], alpha)
        else:
            qkv = _mod_linear(tok, x, mods, _attn_weights(attn_w_qkv[j]), 0, "attn_qkv")
            o_c = _attn_context(qkv, attn_sink[j], n_batch, seq)
            past = cache_k.shape[2]
            o_s = _attn_latent(qkv, _rope_tables(dec_seq), cache_k[:, j].reshape(dec_batch, past, KV_DIM),
                               cache_v[:, j].reshape(dec_batch, past, KV_DIM), attn_sink[j], tok)
            kv_c = qkv[:tok.n_ctx, 2 * D_MODEL:2 * D_MODEL + 2 * KV_DIM]
            st_k.append(kv_c[:, :KV_DIM].reshape(n_batch, seq, N_KV_HEADS, HEAD_DIM))
            st_v.append(kv_c[:, KV_DIM:].reshape(n_batch, seq, N_KV_HEADS, HEAD_DIM))
            x = _mm_postnorm(tok, jnp.concatenate([o_c, o_s], axis=0), x, mods, attn_w_out[j].astype(BF16),
                             ln_g[i, 0], ln_b[i, 0], 2, alpha, "attn_out")
        w_q, key_blocks = _route_weights(peer_w_q[i], peer_keys[i])
        q = _mod_linear(tok, x, mods, w_q, 3, "peer_query")
        route = _peer_route(q, key_blocks, tok.total)
        o = _peer_dense(tok, x, mods, peer_u[i].astype(BF16), peer_v[i].T.astype(BF16), route)
        x = _postnorm(tok, o, x, mods, ln_g[i, 1], ln_b[i, 1], 5, alpha, "peer_norm")

    y_prompt = x[:tok.n_ctx].reshape(n_batch, seq, D_MODEL)
    y_sample = x[tok.n_ctx:].reshape(dec_batch, dec_seq, D_MODEL)
    return (y_prompt, y_sample, jnp.stack(st_re, axis=1), jnp.stack(st_im, axis=1),
            jnp.stack(st_k, axis=1), jnp.stack(st_v, axis=1))
```

```python
import functools
import math

import jax
import jax.numpy as jnp
from jax import lax
from jax.experimental import pallas as pl
from jax.experimental.pallas import tpu as pltpu

F32 = jnp.float32
BF16 = jnp.bfloat16

D_MODEL = 1024
GRID_W = 64
SSM_GROUP = 16
SSM_GROUPS = D_MODEL // SSM_GROUP
SSM_STATE = 64
SSM_CHUNK = 16
SSM_ROW = SSM_CHUNK * SSM_GROUP
HEAD_DIM = 64
N_HEADS = D_MODEL // HEAD_DIM
N_KV_HEADS = 4
KV_REP = N_HEADS // N_KV_HEADS
KV_DIM = N_KV_HEADS * HEAD_DIM
WINDOW = 128
ATTN_BLOCK = 128
ROPE_BASE = 10000.0
ROPE_AXIS_DIM = HEAD_DIM // 2
NEG_INF = -1e30
PEER_HEADS = 8
PEER_NKEYS = 128
PEER_EXPERTS = PEER_NKEYS * PEER_NKEYS
PEER_TOPK = 16
PEER_KEY_DIM = 256
PEER_HALF = PEER_KEY_DIM // 2
LN_EPS = 1e-5

TOKEN_TILE = 512
EXPERT_BLOCK = 1024
SUB_EXPERTS = 512
MOD_ROWS = 16
BF16_SUBLANES = 16
VMEM_LIMIT = 56 * 1024 * 1024

_NT = (((1,), (1,)), ((), ()))


def _cparams(*sem):
    return pltpu.CompilerParams(dimension_semantics=sem, vmem_limit_bytes=VMEM_LIMIT)


def _gelu(x):
    return 0.5 * x * (1.0 + jnp.tanh(0.7978845608028654 * (x + 0.044715 * (x * x * x))))


def _layer_norm_rows(z, g, b):
    mu = jnp.mean(z, axis=-1, keepdims=True)
    zc = z - mu
    var = jnp.mean(zc * zc, axis=-1, keepdims=True)
    return zc * lax.rsqrt(var + LN_EPS) * g + b


def _mod_kernel(c_ref, w_ref, b_ref, o_ref):
    c = c_ref[...]
    s = c * jax.nn.sigmoid(c)
    o_ref[...] = jnp.dot(s.astype(BF16), w_ref[...].astype(BF16),
                         preferred_element_type=F32) + b_ref[...]


def _modulation_all(cond, w_mod, b_mod):
    depth = w_mod.shape[0]
    n_out = w_mod.shape[2]
    nb = n_out // D_MODEL
    out = pl.pallas_call(
        _mod_kernel,
        grid=(depth, nb),
        in_specs=[
            pl.BlockSpec((MOD_ROWS, D_MODEL), lambda i, j: (0, 0)),
            pl.BlockSpec((None, D_MODEL, D_MODEL), lambda i, j: (i, 0, j)),
            pl.BlockSpec((None, 1, D_MODEL), lambda i, j: (i, 0, j)),
        ],
        out_specs=pl.BlockSpec((None, MOD_ROWS, D_MODEL), lambda i, j: (i, 0, j)),
        out_shape=jax.ShapeDtypeStruct((depth, MOD_ROWS, n_out), F32),
        compiler_params=_cparams("arbitrary", "arbitrary"),
        name="modulation",
    )(cond, w_mod, b_mod.reshape(depth, 1, n_out))
    return out.reshape(depth, MOD_ROWS, nb, D_MODEL)


class _Tokens:
    def __init__(self, n_ctx_tokens, dec_batch, dec_seq):
        self.n_ctx = n_ctx_tokens
        self.dec_batch = dec_batch
        self.dec_seq = dec_seq
        self.total = n_ctx_tokens + dec_batch * dec_seq
        assert n_ctx_tokens % TOKEN_TILE == 0 and dec_seq % TOKEN_TILE == 0
        self.ctx_tiles = n_ctx_tokens // TOKEN_TILE
        self.tiles = self.total // TOKEN_TILE
        self.tiles_per_seq = dec_seq // TOKEN_TILE

    def mod_row(self, i):
        return jnp.where(i < self.ctx_tiles, self.dec_batch, (i - self.ctx_tiles) // self.tiles_per_seq)


def _mod_spec(tok):
    return pl.BlockSpec((1, 6, D_MODEL), lambda i: (tok.mod_row(i), 0, 0))


def _row_spec(width):
    return pl.BlockSpec((TOKEN_TILE, width), lambda i: (i, 0))


def _full_spec(shape):
    nd = len(shape)
    return pl.BlockSpec(shape, lambda i: (0,) * nd)


def _linear_kernel(x_ref, m_ref, w_ref, o_ref, *, shift_idx):
    m = m_ref[0]
    h = x_ref[...] * (1.0 + m[shift_idx + 1:shift_idx + 2]) + m[shift_idx:shift_idx + 1]
    o_ref[...] = jnp.dot(h.astype(BF16), w_ref[...], preferred_element_type=F32)


def _mod_linear(tok, x, mods, w, shift_idx, name):
    n = w.shape[1]
    return pl.pallas_call(
        functools.partial(_linear_kernel, shift_idx=shift_idx),
        grid=(tok.tiles,),
        in_specs=[_row_spec(D_MODEL), _mod_spec(tok), _full_spec((D_MODEL, n))],
        out_specs=_row_spec(n),
        out_shape=jax.ShapeDtypeStruct((tok.total, n), F32),
        compiler_params=_cparams("parallel"),
        name=name,
    )(x, mods, w)


def _col_spec():
    return pl.BlockSpec((D_MODEL, TOKEN_TILE), lambda i: (0, i))


def _linear_t_kernel(x_ref, m_ref, w_ref, o_ref, ht_ref, *, shift_idx):
    m = m_ref[0]
    h = x_ref[...] * (1.0 + m[shift_idx + 1:shift_idx + 2]) + m[shift_idx:shift_idx + 1]
    o_ref[...] = jnp.dot(h.astype(BF16), w_ref[...], preferred_element_type=F32)
    ht_ref[...] = h.T.astype(BF16)


def _mod_linear_t(tok, x, mods, w, shift_idx, name):
    n = w.shape[1]
    return pl.pallas_call(
        functools.partial(_linear_t_kernel, shift_idx=shift_idx),
        grid=(tok.tiles,),
        in_specs=[_row_spec(D_MODEL), _mod_spec(tok), _full_spec((D_MODEL, n))],
        out_specs=[_row_spec(n), _col_spec()],
        out_shape=[jax.ShapeDtypeStruct((tok.total, n), F32),
                   jax.ShapeDtypeStruct((D_MODEL, tok.total), BF16)],
        compiler_params=_cparams("parallel"),
        name=name,
    )(x, mods, w)


def _post_norm_rows(x, out, m, gate_idx, g_ref, b_ref, alpha):
    z = alpha * x + (1.0 + m[gate_idx:gate_idx + 1]) * out
    return _layer_norm_rows(z, g_ref[...], b_ref[...])


def _mm_postnorm_kernel(a_ref, x_ref, m_ref, w_ref, g_ref, b_ref, o_ref, *, gate_idx, alpha):
    y = jnp.dot(a_ref[...].astype(BF16), w_ref[...], preferred_element_type=F32)
    o_ref[...] = _post_norm_rows(x_ref[...], y, m_ref[0], gate_idx, g_ref, b_ref, alpha)


def _mm_postnorm(tok, a, x, mods, w, g, b, gate_idx, alpha, name):
    return pl.pallas_call(
        functools.partial(_mm_postnorm_kernel, gate_idx=gate_idx, alpha=alpha),
        grid=(tok.tiles,),
        in_specs=[_row_spec(D_MODEL), _row_spec(D_MODEL), _mod_spec(tok),
                  _full_spec((D_MODEL, D_MODEL)), _full_spec((1, D_MODEL)), _full_spec((1, D_MODEL))],
        out_specs=_row_spec(D_MODEL),
        out_shape=jax.ShapeDtypeStruct((tok.total, D_MODEL), F32),
        compiler_params=_cparams("parallel"),
        name=name,
    )(a, x, mods, w, g.reshape(1, D_MODEL), b.reshape(1, D_MODEL))


def _postnorm_kernel(ot_ref, x_ref, m_ref, g_ref, b_ref, o_ref, *, gate_idx, alpha):
    o_ref[...] = _post_norm_rows(x_ref[...], ot_ref[...].T, m_ref[0], gate_idx, g_ref, b_ref, alpha)


def _postnorm(tok, out_t, x, mods, g, b, gate_idx, alpha, name):
    return pl.pallas_call(
        functools.partial(_postnorm_kernel, gate_idx=gate_idx, alpha=alpha),
        grid=(tok.tiles,),
        in_specs=[_col_spec(), _row_spec(D_MODEL), _mod_spec(tok),
                  _full_spec((1, D_MODEL)), _full_spec((1, D_MODEL))],
        out_specs=_row_spec(D_MODEL),
        out_shape=jax.ShapeDtypeStruct((tok.total, D_MODEL), F32),
        compiler_params=_cparams("parallel"),
        name=name,
    )(out_t, x, mods, g.reshape(1, D_MODEL), b.reshape(1, D_MODEL))


def _s5_operators(a_re, a_im, log_dt, b_re, b_im, c_re, c_im):
    hi = lax.Precision.HIGHEST
    dt = jnp.exp(log_dt)[..., None]
    ar, ai = a_re * dt, a_im * dt
    k = jnp.arange(SSM_CHUNK + 1, dtype=F32)[:, None, None, None]
    mag = jnp.exp(k * ar[None])
    pw_re, pw_im = mag * jnp.cos(k * ai[None]), mag * jnp.sin(k * ai[None])
    l_re, l_im = pw_re[1], pw_im[1]
    den = a_re * a_re + a_im * a_im
    q_re = ((l_re - 1.0) * a_re + l_im * a_im) / den
    q_im = (l_im * a_re - (l_re - 1.0) * a_im) / den
    bt_re = q_re[..., None] * b_re - q_im[..., None] * b_im
    bt_im = q_re[..., None] * b_im + q_im[..., None] * b_re
    d_re = pw_re[..., None] * bt_re[None] - pw_im[..., None] * bt_im[None]
    d_im = pw_re[..., None] * bt_im[None] + pw_im[..., None] * bt_re[None]
    pw_re_c, pw_im_c = pw_re[:, :, :, None, :], pw_im[:, :, :, None, :]
    e_re = c_re[None] * pw_re_c - c_im[None] * pw_im_c
    e_im = c_re[None] * pw_im_c + c_im[None] * pw_re_c
    kern = (jnp.einsum('dgip,kdgpj->kdgij', c_re, d_re, precision=hi)
            - jnp.einsum('dgip,kdgpj->kdgij', c_im, d_im, precision=hi))
    s = jnp.arange(SSM_CHUNK)
    lag = s[None, :] - s[:, None]
    kf = kern[jnp.clip(lag, 0, SSM_CHUNK), 0]
    kb = kern[jnp.clip(-lag, 0, SSM_CHUNK), 1]
    mix = (jnp.where((lag >= 0)[:, :, None, None, None], kf, 0.0)
           + jnp.where((lag <= 0)[:, :, None, None, None], kb, 0.0))
    g = SSM_GROUPS
    m_op = jnp.transpose(mix, (2, 0, 4, 1, 3)).reshape(g, SSM_ROW, SSM_ROW)
    pf = (SSM_CHUNK - 1) - s
    win = jnp.stack([d_re[pf, 0], d_re[s, 1], d_im[pf, 0], d_im[s, 1]], axis=0)
    win_op = jnp.transpose(win, (2, 1, 4, 0, 3)).reshape(g, SSM_ROW, 4 * SSM_STATE)
    wout = jnp.stack([e_re[s + 1, 0], e_re[SSM_CHUNK - s, 1], -e_im[s + 1, 0], -e_im[SSM_CHUNK - s, 1]],
                     axis=0)
    wout_op = jnp.transpose(wout, (2, 0, 4, 1, 3)).reshape(g, 4 * SSM_STATE, SSM_ROW)
    lam = jnp.stack([jnp.concatenate([pw_re[SSM_CHUNK, 0], pw_re[SSM_CHUNK, 1]], axis=-1),
                     jnp.concatenate([pw_im[SSM_CHUNK, 0], pw_im[SSM_CHUNK, 1]], axis=-1)], axis=1)
    return m_op, win_op, wout_op, lam


def _s5_core_kernel(u_ref, h0_ref, m_ref, win_ref, wout_ref, lam_ref, y_ref, s_ref, z_scr, p_scr,
                    *, n_chunks, batch):
    hi = lax.Precision.HIGHEST
    p2 = 2 * SSM_STATE
    u = u_ref[...]
    z_scr[...] = jnp.dot(u, win_ref[...], preferred_element_type=F32, precision=hi)
    l_re = lam_ref[0:1, :]
    l_im = lam_ref[1:2, :]
    is_fwd = lax.broadcasted_iota(jnp.int32, (batch, p2), 1) < SSM_STATE

    def step(k, carry):
        s_re, s_im = carry
        rf = pl.multiple_of(k * batch, batch)
        rb = pl.multiple_of((n_chunks - 1 - k) * batch, batch)
        zf = z_scr[pl.ds(rf, batch), :]
        zb = z_scr[pl.ds(rb, batch), :]
        z_re = jnp.where(is_fwd, zf[:, :p2], zb[:, :p2])
        z_im = jnp.where(is_fwd, zf[:, p2:], zb[:, p2:])
        p_scr[pl.ds(rf, batch), 0:SSM_STATE] = s_re[:, :SSM_STATE]
        p_scr[pl.ds(rf, batch), p2:p2 + SSM_STATE] = s_im[:, :SSM_STATE]
        p_scr[pl.ds(rb, batch), SSM_STATE:p2] = s_re[:, SSM_STATE:]
        p_scr[pl.ds(rb, batch), p2 + SSM_STATE:] = s_im[:, SSM_STATE:]
        n_re = l_re * s_re - l_im * s_im + z_re
        n_im = l_re * s_im + l_im * s_re + z_im
        return n_re, n_im

    h0 = h0_ref[...]
    s_re, s_im = lax.fori_loop(0, n_chunks, step, (h0[:, :p2], h0[:, p2:]))
    s_ref[:, :p2] = s_re
    s_ref[:, p2:] = s_im
    y_ref[...] = (jnp.dot(u, m_ref[...], preferred_element_type=F32, precision=hi)
                  + jnp.dot(p_scr[...], wout_ref[...], preferred_element_type=F32, precision=hi))


def _s5_core(ug, h0, ops, n_chunks, batch, name):
    m_op, win_op, wout_op, lam = ops
    g, rows, _ = ug.shape
    st = 4 * SSM_STATE
    return pl.pallas_call(
        functools.partial(_s5_core_kernel, n_chunks=n_chunks, batch=batch),
        grid=(g,),
        in_specs=[
            pl.BlockSpec((None, rows, SSM_ROW), lambda i: (i, 0, 0)),
            pl.BlockSpec((None, batch, st), lambda i: (i, 0, 0)),
            pl.BlockSpec((None, SSM_ROW, SSM_ROW), lambda i: (i, 0, 0)),
            pl.BlockSpec((None, SSM_ROW, st), lambda i: (i, 0, 0)),
            pl.BlockSpec((None, st, SSM_ROW), lambda i: (i, 0, 0)),
            pl.BlockSpec((None, 2, 2 * SSM_STATE), lambda i: (i, 0, 0)),
        ],
        out_specs=[
            pl.BlockSpec((None, rows, SSM_ROW), lambda i: (i, 0, 0)),
            pl.BlockSpec((None, batch, st), lambda i: (i, 0, 0)),
        ],
        out_shape=[jax.ShapeDtypeStruct((g, rows, SSM_ROW), F32),
                   jax.ShapeDtypeStruct((g, batch, st), F32)],
        scratch_shapes=[pltpu.VMEM((rows, st), F32), pltpu.VMEM((rows, st), F32)],
        compiler_params=_cparams("parallel"),
        name=name,
    )(ug, h0, m_op, win_op, wout_op, lam)


def _to_chunks(u, batch, seq):
    n_chunks = seq // SSM_CHUNK
    t = u.reshape(batch, n_chunks, SSM_CHUNK, SSM_GROUPS, SSM_GROUP)
    return jnp.transpose(t, (3, 1, 0, 2, 4)).reshape(SSM_GROUPS, n_chunks * batch, SSM_ROW)


def _from_chunks(y, batch, seq):
    n_chunks = seq // SSM_CHUNK
    t = y.reshape(SSM_GROUPS, n_chunks, batch, SSM_CHUNK, SSM_GROUP)
    return jnp.transpose(t, (2, 1, 3, 0, 4)).reshape(batch * seq, D_MODEL)


def _state_to_cols(h_re, h_im):
    cols = jnp.concatenate([h_re[:, 0], h_re[:, 1], h_im[:, 0], h_im[:, 1]], axis=-1)
    return jnp.transpose(cols, (1, 0, 2))


def _cols_to_state(s):
    t = jnp.transpose(s, (1, 0, 2)).reshape(s.shape[1], SSM_GROUPS, 2, 2, SSM_STATE)
    t = jnp.transpose(t, (2, 0, 3, 1, 4))
    return t[0], t[1]


def _s5_tail_kernel(y_ref, u_ref, x_ref, m_ref, d_ref, wg_ref, wo_ref, g_ref, b_ref, o_ref, *, alpha):
    v = _gelu(y_ref[...] + d_ref[...] * u_ref[...])
    gl = jnp.dot(v.astype(BF16), wg_ref[...], preferred_element_type=F32)
    a = gl[:, :D_MODEL] * jax.nn.sigmoid(gl[:, D_MODEL:])
    out = jnp.dot(a.astype(BF16), wo_ref[...], preferred_element_type=F32)
    o_ref[...] = _post_norm_rows(x_ref[...], out, m_ref[0], 2, g_ref, b_ref, alpha)


def _s5_tail(tok, y, u, x, mods, d, w_glu, w_out, g, b, alpha):
    return pl.pallas_call(
        functools.partial(_s5_tail_kernel, alpha=alpha),
        grid=(tok.tiles,),
        in_specs=[_row_spec(D_MODEL), _row_spec(D_MODEL), _row_spec(D_MODEL), _mod_spec(tok),
                  _full_spec((1, D_MODEL)), _full_spec((D_MODEL, 2 * D_MODEL)),
                  _full_spec((D_MODEL, D_MODEL)), _full_spec((1, D_MODEL)), _full_spec((1, D_MODEL))],
        out_specs=_row_spec(D_MODEL),
        out_shape=jax.ShapeDtypeStruct((tok.total, D_MODEL), F32),
        compiler_params=_cparams("parallel"),
        name="s5_tail",
    )(y, u, x, mods, d.reshape(1, D_MODEL), w_glu, w_out, g.reshape(1, D_MODEL), b.reshape(1, D_MODEL))


QKV_COLS = 3 * D_MODEL


def _attn_weights(w_qkv):
    wq, wk, wv = w_qkv[:, :D_MODEL], w_qkv[:, D_MODEL:D_MODEL + KV_DIM], w_qkv[:, D_MODEL + KV_DIM:]
    half = ROPE_AXIS_DIM // 2

    def partner(w):
        n = w.shape[1]
        t = w.reshape(D_MODEL, n // ROPE_AXIS_DIM, 2, half)
        return jnp.stack([-t[:, :, 1], t[:, :, 0]], axis=2).reshape(D_MODEL, n)

    pad = jnp.zeros((D_MODEL, QKV_COLS - 2 * D_MODEL - 3 * KV_DIM), w_qkv.dtype)
    return jnp.concatenate([wq, partner(wq), wk, wv, partner(wk), pad], axis=1).astype(BF16)


def _rope_tables(seq):
    half = ROPE_AXIS_DIM // 2
    rows = seq // GRID_W
    row = jnp.repeat(jnp.arange(rows, dtype=F32), GRID_W)
    col = jnp.tile(jnp.arange(GRID_W, dtype=F32), rows)
    inv = ROPE_BASE ** (-jnp.arange(half, dtype=F32) / half)
    ang_r = row[:, None] * inv[None, :]
    ang_c = col[:, None] * inv[None, :]
    ang = jnp.concatenate([ang_r, ang_r, ang_c, ang_c], axis=-1)
    cos = jnp.tile(jnp.cos(ang), (1, N_KV_HEADS))
    sin = jnp.tile(jnp.sin(ang), (1, N_KV_HEADS))
    return jnp.concatenate([cos, sin], axis=-1)


def _softmax_heads(q, k_list, v_list, mask_list, sink_ref, o_ref, heads_per_pass):
    scale = HEAD_DIM ** -0.5
    tq = q.shape[0]
    masks = [None if m is None else jnp.concatenate([m] * heads_per_pass, axis=0) for m in mask_list]
    for first in range(0, N_HEADS, heads_per_pass):
        heads = range(first, first + heads_per_pass)
        g = first // KV_REP
        qg = jnp.concatenate([q[:, h * HEAD_DIM:(h + 1) * HEAD_DIM] for h in heads], axis=0).astype(BF16)
        sink = jnp.concatenate([jnp.full((tq, 1), sink_ref[h], F32) for h in heads], axis=0)
        cols = slice(g * HEAD_DIM, (g + 1) * HEAD_DIM)
        logits = []
        for k, mask in zip(k_list, masks):
            s = lax.dot_general(qg, k[:, cols], _NT, preferred_element_type=F32) * scale
            if mask is not None:
                s = jnp.where(mask, s, NEG_INF)
            logits.append(s)
        m = sink
        for s in logits:
            m = jnp.maximum(m, jnp.max(s, axis=-1, keepdims=True))
        denom = jnp.exp(sink - m)
        acc = jnp.zeros((heads_per_pass * tq, HEAD_DIM), F32)
        for s, v in zip(logits, v_list):
            p = jnp.exp(s - m)
            denom = denom + jnp.sum(p, axis=-1, keepdims=True)
            acc = acc + jnp.dot(p.astype(BF16), v[:, cols], preferred_element_type=F32)
        out = acc / denom
        for r, h in enumerate(heads):
            o_ref[:, h * HEAD_DIM:(h + 1) * HEAD_DIM] = out[r * tq:(r + 1) * tq]


def _attn_ctx_kernel(sink_ref, q_ref, kv_ref, o_ref):
    kv = kv_ref[...]
    k = kv[:, :KV_DIM].astype(BF16)
    v = kv[:, KV_DIM:2 * KV_DIM].astype(BF16)
    _softmax_heads(q_ref[...], [k], [v], [None], sink_ref, o_ref, 1)


def _attn_context(qkv, sink, n_batch, seq):
    return pl.pallas_call(
        _attn_ctx_kernel,
        grid=(n_batch,),
        in_specs=[
            pl.BlockSpec(memory_space=pltpu.SMEM),
            pl.BlockSpec((seq, D_MODEL), lambda b: (b, 0)),
            pl.BlockSpec((seq, D_MODEL), lambda b: (b, 2)),
        ],
        out_specs=pl.BlockSpec((seq, D_MODEL), lambda b: (b, 0)),
        out_shape=jax.ShapeDtypeStruct((n_batch * seq, D_MODEL), F32),
        compiler_params=_cparams("parallel"),
        name="attn_context",
    )(sink, qkv, qkv)


def _attn_lat_kernel(sink_ref, q_ref, qp_ref, kv0_ref, kv1_ref, kv2_ref, t0_ref, t1_ref, t2_ref,
                     kc_ref, vc_ref, o_ref, *, n_qblocks):
    qb = pl.program_id(1)
    t1 = t1_ref[...]
    cos_q = jnp.concatenate([t1[:, :KV_DIM]] * KV_REP, axis=-1)
    sin_q = jnp.concatenate([t1[:, KV_DIM:]] * KV_REP, axis=-1)
    q = q_ref[...] * cos_q + qp_ref[...] * sin_q

    def rope_k(kv_ref, t_ref):
        kv = kv_ref[...]
        t = t_ref[...]
        k = kv[:, :KV_DIM] * t[:, :KV_DIM] + kv[:, 2 * KV_DIM:3 * KV_DIM] * t[:, KV_DIM:]
        return k.astype(BF16), kv[:, KV_DIM:2 * KV_DIM].astype(BF16)

    k0, v0 = rope_k(kv0_ref, t0_ref)
    k1, v1 = rope_k(kv1_ref, t1_ref)
    k2, v2 = rope_k(kv2_ref, t2_ref)
    k_loc = jnp.concatenate([k0, k1, k2], axis=0)
    v_loc = jnp.concatenate([v0, v1, v2], axis=0)
    span = 3 * ATTN_BLOCK
    qi = lax.broadcasted_iota(jnp.int32, (ATTN_BLOCK, span), 0)
    kj = lax.broadcasted_iota(jnp.int32, (ATTN_BLOCK, span), 1) - ATTN_BLOCK
    ok = (jnp.abs(kj - qi) <= WINDOW)
    ok = ok & ((kj >= 0) | (qb > 0)) & ((kj < ATTN_BLOCK) | (qb < n_qblocks - 1))
    kc = kc_ref[0].astype(BF16)
    vc = vc_ref[0].astype(BF16)
    _softmax_heads(q, [kc, k_loc], [vc, v_loc], [None, ok], sink_ref, o_ref, KV_REP)


def _attn_latent(qkv, tables, k_ctx, v_ctx, sink, tok):
    nqb = tok.dec_seq // ATTN_BLOCK
    base = tok.n_ctx // ATTN_BLOCK
    past = k_ctx.shape[1]

    def rows(b, i, off):
        return base + b * nqb + jnp.clip(i + off, 0, nqb - 1)

    def kv_spec(off):
        return pl.BlockSpec((ATTN_BLOCK, D_MODEL), lambda b, i: (rows(b, i, off), 2))

    def tab_spec(off):
        return pl.BlockSpec((ATTN_BLOCK, 2 * KV_DIM), lambda b, i: (jnp.clip(i + off, 0, nqb - 1), 0))

    return pl.pallas_call(
        functools.partial(_attn_lat_kernel, n_qblocks=nqb),
        grid=(tok.dec_batch, nqb),
        in_specs=[
            pl.BlockSpec(memory_space=pltpu.SMEM),
            pl.BlockSpec((ATTN_BLOCK, D_MODEL), lambda b, i: (rows(b, i, 0), 0)),
            pl.BlockSpec((ATTN_BLOCK, D_MODEL), lambda b, i: (rows(b, i, 0), 1)),
            kv_spec(-1), kv_spec(0), kv_spec(1),
            tab_spec(-1), tab_spec(0), tab_spec(1),
            pl.BlockSpec((1, past, KV_DIM), lambda b, i: (b, 0, 0)),
            pl.BlockSpec((1, past, KV_DIM), lambda b, i: (b, 0, 0)),
        ],
        out_specs=pl.BlockSpec((ATTN_BLOCK, D_MODEL), lambda b, i: (b * nqb + i, 0)),
        out_shape=jax.ShapeDtypeStruct((tok.dec_batch * tok.dec_seq, D_MODEL), F32),
        compiler_params=_cparams("parallel", "arbitrary"),
        name="attn_latent",
    )(sink, qkv, qkv, qkv, qkv, qkv, tables, tables, tables, k_ctx, v_ctx)


ROUTE_TILE = 256
ROUTE_LANES = 128
NKH = PEER_NKEYS * PEER_HEADS


def _oddeven_merge_sort_pairs(n):
    pairs = []

    def merge(lo, hi, r):
        step = r * 2
        if step < hi - lo:
            merge(lo, hi, step)
            merge(lo + r, hi, step)
            pairs.extend((i, i + r) for i in range(lo + r, hi - r, step))
        else:
            pairs.append((lo, lo + r))

    def sort(lo, hi):
        if hi - lo >= 1:
            mid = lo + (hi - lo) // 2
            sort(lo, mid)
            sort(mid + 1, hi)
            merge(lo, hi, 1)

    sort(0, n - 1)
    return pairs


_SORT16 = _oddeven_merge_sort_pairs(PEER_TOPK)


def _exchange(x, i, j):
    x[i], x[j] = jnp.maximum(x[i], x[j]), jnp.minimum(x[i], x[j])


def _top16_sorted(vals):
    groups = []
    for g in range(0, len(vals), PEER_TOPK):
        x = list(vals[g:g + PEER_TOPK])
        for i, j in _SORT16:
            _exchange(x, i, j)
        groups.append(x)
    while len(groups) > 1:
        merged = []
        for a, b in zip(groups[0::2], groups[1::2]):
            x = [jnp.maximum(a[i], b[PEER_TOPK - 1 - i]) for i in range(PEER_TOPK)]
            for d in (8, 4, 2, 1):
                for i in range(PEER_TOPK):
                    if not i & d:
                        _exchange(x, i, i + d)
            merged.append(x)
        groups = merged
    return groups[0]


def _rank_bits(s, v):
    b3 = v[7] > s
    b2 = jnp.where(b3, v[11], v[3]) > s
    b1 = jnp.where(b3, jnp.where(b2, v[13], v[9]), jnp.where(b2, v[5], v[1])) > s
    hi = jnp.where(b2, jnp.where(b1, v[14], v[12]), jnp.where(b1, v[10], v[8]))
    lo = jnp.where(b2, jnp.where(b1, v[6], v[4]), jnp.where(b1, v[2], v[0]))
    b0 = jnp.where(b3, hi, lo) > s
    return (b3, b2, b1, b0), v[15] > s


def _select16(bits, leaves):
    b3, b2, b1, b0 = bits
    lvl = [jnp.where(b0, leaves[2 * i + 1], leaves[2 * i]) for i in range(8)]
    lvl = [jnp.where(b1, lvl[2 * i + 1], lvl[2 * i]) for i in range(4)]
    lvl = [jnp.where(b2, lvl[2 * i + 1], lvl[2 * i]) for i in range(2)]
    return jnp.where(b3, lvl[1], lvl[0])


def _exact_top16(s_ref, rank_ref, top_ref):
    n = s_ref.shape[0]
    rank_ref[...] = jnp.full(rank_ref.shape, float(PEER_TOPK), F32)

    def body(k, carry):
        s = s_ref[...]
        iota = lax.broadcasted_iota(jnp.int32, s.shape, 0).astype(F32)
        m = jnp.max(s, axis=0)
        idx = jnp.min(jnp.where(s == m[None], iota, float(n)), axis=0)
        hit = iota == idx[None]
        rank_ref[...] = jnp.where(hit, lax.convert_element_type(k, F32), rank_ref[...])
        s_ref[...] = jnp.where(hit, -jnp.inf, s)
        top_ref[k] = m
        return carry

    lax.fori_loop(0, PEER_TOPK, body, 0)


def _peer_route_kernel(q_ref, k1_ref, k2_ref, r2_ref, w2_ref, n1_ref, w1_ref,
                       r2_st, w2_st, n1_st, w1_st, sa_scr, ra_scr, sb_scr, rb_scr, ta_scr, tb_scr):
    q = q_ref[...].astype(BF16)
    half = PEER_HEADS * PEER_HALF
    s1_all = lax.dot_general(k1_ref[...], q[:, :half], _NT, preferred_element_type=F32)
    s2_all = lax.dot_general(k2_ref[...], q[:, half:], _NT, preferred_element_type=F32)
    slab = (PEER_HEADS, ROUTE_LANES)
    ninf = jnp.full(slab, -jnp.inf, F32)

    for j in range(ROUTE_TILE // ROUTE_LANES):
        lanes = slice(j * ROUTE_LANES, (j + 1) * ROUTE_LANES)
        s1 = [s1_all[k * PEER_HEADS:(k + 1) * PEER_HEADS, lanes] for k in range(PEER_NKEYS)]
        s2 = [s2_all[k * PEER_HEADS:(k + 1) * PEER_HEADS, lanes] for k in range(PEER_NKEYS)]
        v1 = _top16_sorted(s1)
        v2 = _top16_sorted(s2)
        cand = [[v1[a] + v2[b] for b in range(PEER_TOPK // (a + 1))] for a in range(PEER_TOPK)]
        flat = [c for row in cand for c in row]
        flat += [ninf] * (-len(flat) % PEER_TOPK)
        top = _top16_sorted(flat)
        theta = top[PEER_TOPK - 1]
        nb = []
        for row in cand:
            cnt = jnp.zeros(slab, F32)
            for c in row:
                cnt = cnt + jnp.where(c >= theta, 1.0, 0.0)
            nb.append(cnt)
        z = jnp.zeros(slab, F32)
        for k in range(PEER_TOPK):
            z = z + jnp.exp(top[k] - top[0])
        inv_z = 1.0 / z
        total = nb[0]
        for a in range(1, PEER_TOPK):
            total = total + nb[a]
        tied = total != float(PEER_TOPK)
        for k in range(PEER_TOPK - 1):
            tied = tied | ((v1[k] == v1[k + 1]) & (nb[k] != nb[k + 1])) | ((v2[k] == v2[k + 1]) & (nb[0] > float(k)))
        in1 = jnp.zeros(slab, F32)
        in2 = jnp.zeros(slab, F32)
        for k in range(PEER_NKEYS):
            rows = slice(k * PEER_HEADS, (k + 1) * PEER_HEADS)
            bits, out1 = _rank_bits(s1[k], v1)
            n1_st[rows, :] = jnp.where(out1, 0.0, _select16(bits, nb))
            w1_st[rows, :] = jnp.exp(s1[k] - v1[0]) * inv_z
            in1 = in1 + jnp.where(out1, 0.0, 1.0)
            (b3, b2, b1, b0), out2 = _rank_bits(s2[k], v2)
            r = (jnp.where(b3, 8.0, 0.0) + jnp.where(b2, 4.0, 0.0)) + (jnp.where(b1, 2.0, 0.0) + jnp.where(b0, 1.0, 0.0))
            r2_st[rows, :] = jnp.where(out2, float(PEER_TOPK), r)
            w2_st[rows, :] = jnp.exp(s2[k] - v2[0])
            in2 = in2 + jnp.where(out2, 0.0, 1.0)
        tied = tied | ((in1 != float(PEER_TOPK)) & (nb[PEER_TOPK - 1] != 0.0))
        tied = tied | ((in2 != float(PEER_TOPK)) & (nb[0] == float(PEER_TOPK)))
        n_tied = jnp.sum(jnp.where(tied, 1.0, 0.0))

        @pl.when(n_tied > 0.0)
        def _():
            shape3 = (PEER_NKEYS, PEER_HEADS, ROUTE_LANES)
            sa_scr[...] = s1_all[:, lanes].reshape(shape3)
            _exact_top16(sa_scr, ra_scr, ta_scr)
            sb_scr[...] = s2_all[:, lanes].reshape(shape3)
            _exact_top16(sb_scr, rb_scr, tb_scr)
            r1 = ra_scr[...]
            r2_st[...] = rb_scr[...].reshape(NKH, ROUTE_LANES)
            e1 = jnp.exp(s1_all[:, lanes].reshape(shape3) - ta_scr[0][None])
            w2_st[...] = jnp.exp(s2_all[:, lanes] - jnp.tile(tb_scr[0], (PEER_NKEYS, 1)))
            n_cand = len(flat)
            pos = 0
            for a in range(PEER_TOPK):
                width = PEER_TOPK // (a + 1)
                sa_scr[pos:pos + width] = ta_scr[a][None] + tb_scr[0:width]
                pos += width
            sa_scr[pos:n_cand] = jnp.full((n_cand - pos,) + slab, -jnp.inf, F32)
            _exact_top16(sa_scr.at[0:n_cand], sb_scr.at[0:n_cand], ta_scr)
            picked = jnp.where(sb_scr[0:n_cand] < float(PEER_TOPK), 1.0, 0.0)
            zz = jnp.zeros(slab, F32)
            for k in range(PEER_TOPK):
                zz = zz + jnp.exp(ta_scr[k] - ta_scr[0])
            n1 = jnp.zeros(shape3, F32)
            pos = 0
            for a in range(PEER_TOPK):
                width = PEER_TOPK // (a + 1)
                cnt = jnp.sum(picked[pos:pos + width], axis=0)
                n1 = jnp.where(r1 == float(a), cnt[None], n1)
                pos += width
            n1_st[...] = n1.reshape(NKH, ROUTE_LANES)
            w1_st[...] = (e1 / zz[None]).reshape(NKH, ROUTE_LANES)

        n1_ref[:, :, lanes] = n1_st[...].reshape(PEER_NKEYS, PEER_HEADS, ROUTE_LANES)
        w1_ref[:, :, lanes] = w1_st[...].reshape(PEER_NKEYS, PEER_HEADS, ROUTE_LANES)
        for h in range(PEER_HEADS):
            r2_ref[h, :, lanes] = r2_st[pl.ds(h, PEER_NKEYS, stride=PEER_HEADS), :].astype(BF16)
            w2_ref[h, :, lanes] = w2_st[pl.ds(h, PEER_NKEYS, stride=PEER_HEADS), :].astype(BF16)


def _route_weights(w_q, keys):
    w = w_q.reshape(D_MODEL, PEER_HEADS, 2, PEER_HALF)
    w = jnp.transpose(w, (0, 2, 1, 3)).reshape(D_MODEL, PEER_HEADS * PEER_KEY_DIM)
    eye = jnp.eye(PEER_HEADS, dtype=keys.dtype)
    big = jnp.einsum('phkd,hg->pkhgd', keys, eye).reshape(2, NKH, PEER_HEADS * PEER_HALF)
    return w.astype(BF16), big.astype(BF16)


def _peer_route(q, key_blocks, n_tokens):
    half = PEER_HEADS * PEER_HALF
    shape_kh = jax.ShapeDtypeStruct((PEER_NKEYS, PEER_HEADS, n_tokens), F32)
    shape_hk = jax.ShapeDtypeStruct((PEER_HEADS, PEER_NKEYS, n_tokens), BF16)
    spec_kh = pl.BlockSpec((PEER_NKEYS, PEER_HEADS, ROUTE_TILE), lambda i: (0, 0, i))
    spec_hk = pl.BlockSpec((PEER_HEADS, PEER_NKEYS, ROUTE_TILE), lambda i: (0, 0, i))
    stage = pltpu.VMEM((NKH, ROUTE_LANES), F32)
    cand3 = pltpu.VMEM((PEER_NKEYS, PEER_HEADS, ROUTE_LANES), F32)
    top3 = pltpu.VMEM((PEER_TOPK, PEER_HEADS, ROUTE_LANES), F32)
    return pl.pallas_call(
        _peer_route_kernel,
        grid=(n_tokens // ROUTE_TILE,),
        in_specs=[
            pl.BlockSpec((ROUTE_TILE, 2 * half), lambda i: (i, 0)),
            pl.BlockSpec((None, NKH, half), lambda i: (0, 0, 0)),
            pl.BlockSpec((None, NKH, half), lambda i: (1, 0, 0)),
        ],
        out_specs=[spec_hk, spec_hk, spec_kh, spec_kh],
        out_shape=[shape_hk, shape_hk, shape_kh, shape_kh],
        scratch_shapes=[stage, stage, stage, stage, cand3, cand3, cand3, cand3, top3, top3],
        compiler_params=_cparams("parallel"),
        name="peer_route",
    )(q, key_blocks, key_blocks)


def _peer_dense_kernel(ht_ref, u_ref, vt_ref, r2_ref, w2_ref, n1_ref, w1_ref, o_ref):
    c = pl.program_id(1)
    sub = BF16_SUBLANES
    zero = jnp.zeros((), BF16)
    keys_per_sub = SUB_EXPERTS // PEER_NKEYS
    parts = []
    for sb in range(EXPERT_BLOCK // SUB_EXPERTS):
        s = jnp.dot(u_ref[sb * SUB_EXPERTS:(sb + 1) * SUB_EXPERTS, :], ht_ref[...],
                    preferred_element_type=F32)
        act = _gelu(s.astype(BF16))
        for cc in range(keys_per_sub):
            key1 = c * (EXPERT_BLOCK // PEER_NKEYS) + sb * keys_per_sub + cc
            gate = jnp.zeros((PEER_NKEYS // sub, sub, TOKEN_TILE), BF16)
            for h in range(PEER_HEADS):
                n1 = jnp.broadcast_to(n1_ref[key1, pl.ds(h, 1), :], (sub, TOKEN_TILE)).astype(BF16)
                w1 = jnp.broadcast_to(w1_ref[key1, pl.ds(h, 1), :], (sub, TOKEN_TILE)).astype(BF16)
                r2 = r2_ref[h].reshape(PEER_NKEYS // sub, sub, TOKEN_TILE)
                w2 = w2_ref[h].reshape(PEER_NKEYS // sub, sub, TOKEN_TILE)
                gate = gate + jnp.where(r2 < n1[None], w2, zero) * w1[None]
            parts.append(act[cc * PEER_NKEYS:(cc + 1) * PEER_NKEYS] * gate.reshape(PEER_NKEYS, TOKEN_TILE))
    a = jnp.concatenate(parts, axis=0)
    upd = jnp.dot(vt_ref[...], a, preferred_element_type=F32)
    o_ref[...] = jnp.where(c == 0, 0.0, o_ref[...]) + upd


def _peer_dense(tok, h_t, u_tab, vt_tab, route):
    r2, w2, n1, w1 = route
    spec_hk = pl.BlockSpec((PEER_HEADS, PEER_NKEYS, TOKEN_TILE), lambda i, c: (0, 0, i))
    spec_kh = pl.BlockSpec((PEER_NKEYS, PEER_HEADS, TOKEN_TILE), lambda i, c: (0, 0, i))
    col_spec = pl.BlockSpec((D_MODEL, TOKEN_TILE), lambda i, c: (0, i))
    return pl.pallas_call(
        _peer_dense_kernel,
        grid=(tok.tiles, PEER_EXPERTS // EXPERT_BLOCK),
        in_specs=[
            col_spec,
            pl.BlockSpec((EXPERT_BLOCK, D_MODEL), lambda i, c: (c, 0)),
            pl.BlockSpec((D_MODEL, EXPERT_BLOCK), lambda i, c: (0, c)),
            spec_hk, spec_hk, spec_kh, spec_kh,
        ],
        out_specs=col_spec,
        out_shape=jax.ShapeDtypeStruct((D_MODEL, tok.total), F32),
        compiler_params=_cparams("parallel", "arbitrary"),
        name="peer_dense",
    )(h_t, u_tab, vt_tab, r2, w2, n1, w1)


def kernel(x_prompt, x_sample, state_ssm_re, state_ssm_im, cache_k, cache_v, c, c_ctx, w_mod, b_mod, ln_g, ln_b, ssm_w_in, ssm_a_re, ssm_a_im, ssm_log_dt, ssm_b_re, ssm_b_im, ssm_c_re, ssm_c_im, ssm_d, ssm_w_glu, ssm_w_out, attn_w_qkv, attn_sink, attn_w_out, peer_w_q, peer_keys, peer_u, peer_v):
    depth = w_mod.shape[0]
    n_batch, seq, _ = x_prompt.shape
    dec_batch, dec_seq, _ = x_sample.shape
    tok = _Tokens(n_batch * seq, dec_batch, dec_seq)
    alpha = (2 * depth) ** 0.25

    cond = jnp.concatenate([c, c_ctx[None, :],
                            jnp.zeros((MOD_ROWS - dec_batch - 1, D_MODEL), F32)], axis=0)
    mods_all = _modulation_all(cond, w_mod, b_mod)

    x = jnp.concatenate([x_prompt.reshape(-1, D_MODEL), x_sample.reshape(-1, D_MODEL)], axis=0)
    st_re, st_im, st_k, st_v = [], [], [], []
    for i in range(depth):
        j = i // 2
        mods = mods_all[i]
        if i % 2 == 0:
            u = _mod_linear(tok, x, mods, ssm_w_in[j].astype(BF16), 0, "s5_in")
            ops = _s5_operators(ssm_a_re[j], ssm_a_im[j], ssm_log_dt[j], ssm_b_re[j], ssm_b_im[j],
                                ssm_c_re[j], ssm_c_im[j])
            zeros = jnp.zeros((SSM_GROUPS, n_batch, 4 * SSM_STATE), F32)
            y_c, s_c = _s5_core(_to_chunks(u[:tok.n_ctx], n_batch, seq), zeros, ops,
                                seq // SSM_CHUNK, n_batch, "s5_core_context")
            h0 = _state_to_cols(state_ssm_re[:, j], state_ssm_im[:, j])
            y_s, _ = _s5_core(_to_chunks(u[tok.n_ctx:], dec_batch, dec_seq), h0, ops,
                              dec_seq // SSM_CHUNK, dec_batch, "s5_core_latent")
            y = jnp.concatenate([_from_chunks(y_c, n_batch, seq), _from_chunks(y_s, dec_batch, dec_seq)], axis=0)
            s_re, s_im = _cols_to_state(s_c)
            st_re.append(s_re)
            st_im.append(s_im)
            x = _s5_tail(tok, y, u, x, mods, ssm_d[j], ssm_w_glu[j].astype(BF16), ssm_w_out[j].astype(BF16),
                         ln_g[i, 0], ln_b[i, 0], alpha)
        else:
            qkv = _mod_linear(tok, x, mods, _attn_weights(attn_w_qkv[j]), 0, "attn_qkv")
            o_c = _attn_context(qkv, attn_sink[j], n_batch, seq)
            past = cache_k.shape[2]
            o_s = _attn_latent(qkv, _rope_tables(dec_seq), cache_k[:, j].reshape(dec_batch, past, KV_DIM),
                               cache_v[:, j].reshape(dec_batch, past, KV_DIM), attn_sink[j], tok)
            kv_c = qkv[:tok.n_ctx, 2 * D_MODEL:2 * D_MODEL + 2 * KV_DIM]
            st_k.append(kv_c[:, :KV_DIM].reshape(n_batch, seq, N_KV_HEADS, HEAD_DIM))
            st_v.append(kv_c[:, KV_DIM:].reshape(n_batch, seq, N_KV_HEADS, HEAD_DIM))
            x = _mm_postnorm(tok, jnp.concatenate([o_c, o_s], axis=0), x, mods, attn_w_out[j].astype(BF16),
                             ln_g[i, 0], ln_b[i, 0], 2, alpha, "attn_out")
        w_q, key_blocks = _route_weights(peer_w_q[i], peer_keys[i])
        q, h_t = _mod_linear_t(tok, x, mods, w_q, 3, "peer_query")
        route = _peer_route(q, key_blocks, tok.total)
        o_t = _peer_dense(tok, h_t, peer_u[i].astype(BF16), peer_v[i].T.astype(BF16), route)
        x = _postnorm(tok, o_t, x, mods, ln_g[i, 1], ln_b[i, 1], 5, alpha, "peer_norm")

    y_prompt = x[:tok.n_ctx].reshape(n_batch, seq, D_MODEL)
    y_sample = x[tok.n_ctx:].reshape(dec_batch, dec_seq, D_MODEL)
    return (y_prompt, y_sample, jnp.stack(st_re, axis=1), jnp.stack(st_im, axis=1),
            jnp.stack(st_k, axis=1), jnp.stack(st_v, axis=1))
```

```python
import functools
import math

import jax
import jax.numpy as jnp
from jax import lax
from jax.experimental import pallas as pl
from jax.experimental.pallas import tpu as pltpu

F32 = jnp.float32
BF16 = jnp.bfloat16

D_MODEL = 1024
GRID_W = 64
SSM_GROUP = 16
SSM_GROUPS = D_MODEL // SSM_GROUP
SSM_STATE = 64
SSM_CHUNK = 16
SSM_ROW = SSM_CHUNK * SSM_GROUP
HEAD_DIM = 64
N_HEADS = D_MODEL // HEAD_DIM
N_KV_HEADS = 4
KV_REP = N_HEADS // N_KV_HEADS
KV_DIM = N_KV_HEADS * HEAD_DIM
WINDOW = 128
ATTN_BLOCK = 128
ROPE_BASE = 10000.0
ROPE_AXIS_DIM = HEAD_DIM // 2
NEG_INF = -1e30
PEER_HEADS = 8
PEER_NKEYS = 128
PEER_EXPERTS = PEER_NKEYS * PEER_NKEYS
PEER_TOPK = 16
PEER_KEY_DIM = 256
PEER_HALF = PEER_KEY_DIM // 2
LN_EPS = 1e-5

TOKEN_TILE = 512
EXPERT_BLOCK = 2048
TABLE_BLOCK = 1024
SUB_EXPERTS = 512
MOD_ROWS = 16
BF16_SUBLANES = 16
VMEM_LIMIT = 56 * 1024 * 1024

_NT = (((1,), (1,)), ((), ()))


def _cparams(*sem):
    return pltpu.CompilerParams(dimension_semantics=sem, vmem_limit_bytes=VMEM_LIMIT)


def _gelu(x):
    return 0.5 * x * (1.0 + jnp.tanh(0.7978845608028654 * (x + 0.044715 * (x * x * x))))


def _layer_norm_rows(z, g, b):
    mu = jnp.mean(z, axis=-1, keepdims=True)
    zc = z - mu
    var = jnp.mean(zc * zc, axis=-1, keepdims=True)
    return zc * lax.rsqrt(var + LN_EPS) * g + b


def _mod_kernel(c_ref, w_ref, b_ref, o_ref):
    c = c_ref[...]
    s = c * jax.nn.sigmoid(c)
    o_ref[...] = jnp.dot(s.astype(BF16), w_ref[...].astype(BF16),
                         preferred_element_type=F32) + b_ref[...]


def _modulation_all(cond, w_mod, b_mod):
    depth = w_mod.shape[0]
    n_out = w_mod.shape[2]
    nb = n_out // D_MODEL
    out = pl.pallas_call(
        _mod_kernel,
        grid=(depth, nb),
        in_specs=[
            pl.BlockSpec((MOD_ROWS, D_MODEL), lambda i, j: (0, 0)),
            pl.BlockSpec((None, D_MODEL, D_MODEL), lambda i, j: (i, 0, j)),
            pl.BlockSpec((None, 1, D_MODEL), lambda i, j: (i, 0, j)),
        ],
        out_specs=pl.BlockSpec((None, MOD_ROWS, D_MODEL), lambda i, j: (i, 0, j)),
        out_shape=jax.ShapeDtypeStruct((depth, MOD_ROWS, n_out), F32),
        compiler_params=_cparams("arbitrary", "arbitrary"),
        name="modulation",
    )(cond, w_mod, b_mod.reshape(depth, 1, n_out))
    return out.reshape(depth, MOD_ROWS, nb, D_MODEL)


class _Tokens:
    def __init__(self, n_ctx_tokens, dec_batch, dec_seq):
        self.n_ctx = n_ctx_tokens
        self.dec_batch = dec_batch
        self.dec_seq = dec_seq
        self.total = n_ctx_tokens + dec_batch * dec_seq
        assert n_ctx_tokens % TOKEN_TILE == 0 and dec_seq % TOKEN_TILE == 0
        self.ctx_tiles = n_ctx_tokens // TOKEN_TILE
        self.tiles = self.total // TOKEN_TILE
        self.tiles_per_seq = dec_seq // TOKEN_TILE

    def mod_row(self, i):
        return jnp.where(i < self.ctx_tiles, self.dec_batch, (i - self.ctx_tiles) // self.tiles_per_seq)


def _mod_spec(tok):
    return pl.BlockSpec((1, 6, D_MODEL), lambda i: (tok.mod_row(i), 0, 0))


def _row_spec(width):
    return pl.BlockSpec((TOKEN_TILE, width), lambda i: (i, 0))


def _full_spec(shape):
    nd = len(shape)
    return pl.BlockSpec(shape, lambda i: (0,) * nd)


def _linear_kernel(x_ref, m_ref, w_ref, o_ref, *, shift_idx):
    m = m_ref[0]
    h = x_ref[...] * (1.0 + m[shift_idx + 1:shift_idx + 2]) + m[shift_idx:shift_idx + 1]
    o_ref[...] = jnp.dot(h.astype(BF16), w_ref[...], preferred_element_type=F32)


def _mod_linear(tok, x, mods, w, shift_idx, name):
    n = w.shape[1]
    return pl.pallas_call(
        functools.partial(_linear_kernel, shift_idx=shift_idx),
        grid=(tok.tiles,),
        in_specs=[_row_spec(D_MODEL), _mod_spec(tok), _full_spec((D_MODEL, n))],
        out_specs=_row_spec(n),
        out_shape=jax.ShapeDtypeStruct((tok.total, n), F32),
        compiler_params=_cparams("parallel"),
        name=name,
    )(x, mods, w)


def _col_spec():
    return pl.BlockSpec((D_MODEL, TOKEN_TILE), lambda i: (0, i))


def _linear_t_kernel(x_ref, m_ref, w_ref, o_ref, ht_ref, *, shift_idx):
    m = m_ref[0]
    h = x_ref[...] * (1.0 + m[shift_idx + 1:shift_idx + 2]) + m[shift_idx:shift_idx + 1]
    o_ref[...] = jnp.dot(h.astype(BF16), w_ref[...], preferred_element_type=F32)
    ht_ref[...] = h.T.astype(BF16)


def _mod_linear_t(tok, x, mods, w, shift_idx, name):
    n = w.shape[1]
    return pl.pallas_call(
        functools.partial(_linear_t_kernel, shift_idx=shift_idx),
        grid=(tok.tiles,),
        in_specs=[_row_spec(D_MODEL), _mod_spec(tok), _full_spec((D_MODEL, n))],
        out_specs=[_row_spec(n), _col_spec()],
        out_shape=[jax.ShapeDtypeStruct((tok.total, n), F32),
                   jax.ShapeDtypeStruct((D_MODEL, tok.total), BF16)],
        compiler_params=_cparams("parallel"),
        name=name,
    )(x, mods, w)


def _post_norm_rows(x, out, m, gate_idx, g_ref, b_ref, alpha):
    z = alpha * x + (1.0 + m[gate_idx:gate_idx + 1]) * out
    return _layer_norm_rows(z, g_ref[...], b_ref[...])


def _mm_postnorm_kernel(a_ref, x_ref, m_ref, w_ref, g_ref, b_ref, o_ref, *, gate_idx, alpha):
    y = jnp.dot(a_ref[...].astype(BF16), w_ref[...], preferred_element_type=F32)
    o_ref[...] = _post_norm_rows(x_ref[...], y, m_ref[0], gate_idx, g_ref, b_ref, alpha)


def _mm_postnorm(tok, a, x, mods, w, g, b, gate_idx, alpha, name):
    return pl.pallas_call(
        functools.partial(_mm_postnorm_kernel, gate_idx=gate_idx, alpha=alpha),
        grid=(tok.tiles,),
        in_specs=[_row_spec(D_MODEL), _row_spec(D_MODEL), _mod_spec(tok),
                  _full_spec((D_MODEL, D_MODEL)), _full_spec((1, D_MODEL)), _full_spec((1, D_MODEL))],
        out_specs=_row_spec(D_MODEL),
        out_shape=jax.ShapeDtypeStruct((tok.total, D_MODEL), F32),
        compiler_params=_cparams("parallel"),
        name=name,
    )(a, x, mods, w, g.reshape(1, D_MODEL), b.reshape(1, D_MODEL))


def _postnorm_kernel(ot_ref, x_ref, m_ref, g_ref, b_ref, o_ref, *, gate_idx, alpha):
    o_ref[...] = _post_norm_rows(x_ref[...], ot_ref[...].T, m_ref[0], gate_idx, g_ref, b_ref, alpha)


def _postnorm(tok, out_t, x, mods, g, b, gate_idx, alpha, name):
    return pl.pallas_call(
        functools.partial(_postnorm_kernel, gate_idx=gate_idx, alpha=alpha),
        grid=(tok.tiles,),
        in_specs=[_col_spec(), _row_spec(D_MODEL), _mod_spec(tok),
                  _full_spec((1, D_MODEL)), _full_spec((1, D_MODEL))],
        out_specs=_row_spec(D_MODEL),
        out_shape=jax.ShapeDtypeStruct((tok.total, D_MODEL), F32),
        compiler_params=_cparams("parallel"),
        name=name,
    )(out_t, x, mods, g.reshape(1, D_MODEL), b.reshape(1, D_MODEL))


def _s5_operators(a_re, a_im, log_dt, b_re, b_im, c_re, c_im):
    hi = lax.Precision.HIGHEST
    dt = jnp.exp(log_dt)[..., None]
    ar, ai = a_re * dt, a_im * dt
    k = jnp.arange(SSM_CHUNK + 1, dtype=F32)[:, None, None, None]
    mag = jnp.exp(k * ar[None])
    pw_re, pw_im = mag * jnp.cos(k * ai[None]), mag * jnp.sin(k * ai[None])
    l_re, l_im = pw_re[1], pw_im[1]
    den = a_re * a_re + a_im * a_im
    q_re = ((l_re - 1.0) * a_re + l_im * a_im) / den
    q_im = (l_im * a_re - (l_re - 1.0) * a_im) / den
    bt_re = q_re[..., None] * b_re - q_im[..., None] * b_im
    bt_im = q_re[..., None] * b_im + q_im[..., None] * b_re
    d_re = pw_re[..., None] * bt_re[None] - pw_im[..., None] * bt_im[None]
    d_im = pw_re[..., None] * bt_im[None] + pw_im[..., None] * bt_re[None]
    pw_re_c, pw_im_c = pw_re[:, :, :, None, :], pw_im[:, :, :, None, :]
    e_re = c_re[None] * pw_re_c - c_im[None] * pw_im_c
    e_im = c_re[None] * pw_im_c + c_im[None] * pw_re_c
    kern = (jnp.einsum('dgip,kdgpj->kdgij', c_re, d_re, precision=hi)
            - jnp.einsum('dgip,kdgpj->kdgij', c_im, d_im, precision=hi))
    s = jnp.arange(SSM_CHUNK)
    lag = s[None, :] - s[:, None]
    kf = kern[jnp.clip(lag, 0, SSM_CHUNK), 0]
    kb = kern[jnp.clip(-lag, 0, SSM_CHUNK), 1]
    mix = (jnp.where((lag >= 0)[:, :, None, None, None], kf, 0.0)
           + jnp.where((lag <= 0)[:, :, None, None, None], kb, 0.0))
    g = SSM_GROUPS
    m_op = jnp.transpose(mix, (2, 0, 4, 1, 3)).reshape(g, SSM_ROW, SSM_ROW)
    pf = (SSM_CHUNK - 1) - s
    win = jnp.stack([d_re[pf, 0], d_re[s, 1], d_im[pf, 0], d_im[s, 1]], axis=0)
    win_op = jnp.transpose(win, (2, 1, 4, 0, 3)).reshape(g, SSM_ROW, 4 * SSM_STATE)
    wout = jnp.stack([e_re[s + 1, 0], e_re[SSM_CHUNK - s, 1], -e_im[s + 1, 0], -e_im[SSM_CHUNK - s, 1]],
                     axis=0)
    wout_op = jnp.transpose(wout, (2, 0, 4, 1, 3)).reshape(g, 4 * SSM_STATE, SSM_ROW)
    lam = jnp.stack([jnp.concatenate([pw_re[SSM_CHUNK, 0], pw_re[SSM_CHUNK, 1]], axis=-1),
                     jnp.concatenate([pw_im[SSM_CHUNK, 0], pw_im[SSM_CHUNK, 1]], axis=-1)], axis=1)
    return m_op, win_op, wout_op, lam


def _s5_core_kernel(u_ref, h0_ref, m_ref, win_ref, wout_ref, lam_ref, y_ref, s_ref, z_scr, p_scr,
                    *, n_chunks, batch):
    hi = lax.Precision.HIGHEST
    p2 = 2 * SSM_STATE
    u = u_ref[...]
    z_scr[...] = jnp.dot(u, win_ref[...], preferred_element_type=F32, precision=hi)
    l_re = lam_ref[0:1, :]
    l_im = lam_ref[1:2, :]
    is_fwd = lax.broadcasted_iota(jnp.int32, (batch, p2), 1) < SSM_STATE

    def step(k, carry):
        s_re, s_im = carry
        rf = pl.multiple_of(k * batch, batch)
        rb = pl.multiple_of((n_chunks - 1 - k) * batch, batch)
        zf = z_scr[pl.ds(rf, batch), :]
        zb = z_scr[pl.ds(rb, batch), :]
        z_re = jnp.where(is_fwd, zf[:, :p2], zb[:, :p2])
        z_im = jnp.where(is_fwd, zf[:, p2:], zb[:, p2:])
        p_scr[pl.ds(rf, batch), 0:SSM_STATE] = s_re[:, :SSM_STATE]
        p_scr[pl.ds(rf, batch), p2:p2 + SSM_STATE] = s_im[:, :SSM_STATE]
        p_scr[pl.ds(rb, batch), SSM_STATE:p2] = s_re[:, SSM_STATE:]
        p_scr[pl.ds(rb, batch), p2 + SSM_STATE:] = s_im[:, SSM_STATE:]
        n_re = l_re * s_re - l_im * s_im + z_re
        n_im = l_re * s_im + l_im * s_re + z_im
        return n_re, n_im

    h0 = h0_ref[...]
    s_re, s_im = lax.fori_loop(0, n_chunks, step, (h0[:, :p2], h0[:, p2:]))
    s_ref[:, :p2] = s_re
    s_ref[:, p2:] = s_im
    y_ref[...] = (jnp.dot(u, m_ref[...], preferred_element_type=F32, precision=hi)
                  + jnp.dot(p_scr[...], wout_ref[...], preferred_element_type=F32, precision=hi))


def _s5_core(ug, h0, ops, n_chunks, batch, name):
    m_op, win_op, wout_op, lam = ops
    g, rows, _ = ug.shape
    st = 4 * SSM_STATE
    return pl.pallas_call(
        functools.partial(_s5_core_kernel, n_chunks=n_chunks, batch=batch),
        grid=(g,),
        in_specs=[
            pl.BlockSpec((None, rows, SSM_ROW), lambda i: (i, 0, 0)),
            pl.BlockSpec((None, batch, st), lambda i: (i, 0, 0)),
            pl.BlockSpec((None, SSM_ROW, SSM_ROW), lambda i: (i, 0, 0)),
            pl.BlockSpec((None, SSM_ROW, st), lambda i: (i, 0, 0)),
            pl.BlockSpec((None, st, SSM_ROW), lambda i: (i, 0, 0)),
            pl.BlockSpec((None, 2, 2 * SSM_STATE), lambda i: (i, 0, 0)),
        ],
        out_specs=[
            pl.BlockSpec((None, rows, SSM_ROW), lambda i: (i, 0, 0)),
            pl.BlockSpec((None, batch, st), lambda i: (i, 0, 0)),
        ],
        out_shape=[jax.ShapeDtypeStruct((g, rows, SSM_ROW), F32),
                   jax.ShapeDtypeStruct((g, batch, st), F32)],
        scratch_shapes=[pltpu.VMEM((rows, st), F32), pltpu.VMEM((rows, st), F32)],
        compiler_params=_cparams("parallel"),
        name=name,
    )(ug, h0, m_op, win_op, wout_op, lam)


def _to_chunks(u, batch, seq):
    n_chunks = seq // SSM_CHUNK
    t = u.reshape(batch, n_chunks, SSM_CHUNK, SSM_GROUPS, SSM_GROUP)
    return jnp.transpose(t, (3, 1, 0, 2, 4)).reshape(SSM_GROUPS, n_chunks * batch, SSM_ROW)


def _from_chunks(y, batch, seq):
    n_chunks = seq // SSM_CHUNK
    t = y.reshape(SSM_GROUPS, n_chunks, batch, SSM_CHUNK, SSM_GROUP)
    return jnp.transpose(t, (2, 1, 3, 0, 4)).reshape(batch * seq, D_MODEL)


def _state_to_cols(h_re, h_im):
    cols = jnp.concatenate([h_re[:, 0], h_re[:, 1], h_im[:, 0], h_im[:, 1]], axis=-1)
    return jnp.transpose(cols, (1, 0, 2))


def _cols_to_state(s):
    t = jnp.transpose(s, (1, 0, 2)).reshape(s.shape[1], SSM_GROUPS, 2, 2, SSM_STATE)
    t = jnp.transpose(t, (2, 0, 3, 1, 4))
    return t[0], t[1]


def _s5_tail_kernel(y_ref, u_ref, x_ref, m_ref, d_ref, wg_ref, wo_ref, g_ref, b_ref, o_ref, *, alpha):
    v = _gelu(y_ref[...] + d_ref[...] * u_ref[...])
    gl = jnp.dot(v.astype(BF16), wg_ref[...], preferred_element_type=F32)
    a = gl[:, :D_MODEL] * jax.nn.sigmoid(gl[:, D_MODEL:])
    out = jnp.dot(a.astype(BF16), wo_ref[...], preferred_element_type=F32)
    o_ref[...] = _post_norm_rows(x_ref[...], out, m_ref[0], 2, g_ref, b_ref, alpha)


def _s5_tail(tok, y, u, x, mods, d, w_glu, w_out, g, b, alpha):
    return pl.pallas_call(
        functools.partial(_s5_tail_kernel, alpha=alpha),
        grid=(tok.tiles,),
        in_specs=[_row_spec(D_MODEL), _row_spec(D_MODEL), _row_spec(D_MODEL), _mod_spec(tok),
                  _full_spec((1, D_MODEL)), _full_spec((D_MODEL, 2 * D_MODEL)),
                  _full_spec((D_MODEL, D_MODEL)), _full_spec((1, D_MODEL)), _full_spec((1, D_MODEL))],
        out_specs=_row_spec(D_MODEL),
        out_shape=jax.ShapeDtypeStruct((tok.total, D_MODEL), F32),
        compiler_params=_cparams("parallel"),
        name="s5_tail",
    )(y, u, x, mods, d.reshape(1, D_MODEL), w_glu, w_out, g.reshape(1, D_MODEL), b.reshape(1, D_MODEL))


QKV_COLS = 3 * D_MODEL


def _attn_weights(w_qkv):
    wq, wk, wv = w_qkv[:, :D_MODEL], w_qkv[:, D_MODEL:D_MODEL + KV_DIM], w_qkv[:, D_MODEL + KV_DIM:]
    half = ROPE_AXIS_DIM // 2

    def partner(w):
        n = w.shape[1]
        t = w.reshape(D_MODEL, n // ROPE_AXIS_DIM, 2, half)
        return jnp.stack([-t[:, :, 1], t[:, :, 0]], axis=2).reshape(D_MODEL, n)

    pad = jnp.zeros((D_MODEL, QKV_COLS - 2 * D_MODEL - 3 * KV_DIM), w_qkv.dtype)
    return jnp.concatenate([wq, partner(wq), wk, wv, partner(wk), pad], axis=1).astype(BF16)


def _rope_tables(seq):
    half = ROPE_AXIS_DIM // 2
    rows = seq // GRID_W
    row = jnp.repeat(jnp.arange(rows, dtype=F32), GRID_W)
    col = jnp.tile(jnp.arange(GRID_W, dtype=F32), rows)
    inv = ROPE_BASE ** (-jnp.arange(half, dtype=F32) / half)
    ang_r = row[:, None] * inv[None, :]
    ang_c = col[:, None] * inv[None, :]
    ang = jnp.concatenate([ang_r, ang_r, ang_c, ang_c], axis=-1)
    cos = jnp.tile(jnp.cos(ang), (1, N_KV_HEADS))
    sin = jnp.tile(jnp.sin(ang), (1, N_KV_HEADS))
    return jnp.concatenate([cos, sin], axis=-1)


def _softmax_heads(q, k_list, v_list, mask_list, sink_ref, o_ref, heads_per_pass):
    scale = HEAD_DIM ** -0.5
    tq = q.shape[0]
    masks = [None if m is None else jnp.concatenate([m] * heads_per_pass, axis=0) for m in mask_list]
    for first in range(0, N_HEADS, heads_per_pass):
        heads = range(first, first + heads_per_pass)
        g = first // KV_REP
        qg = jnp.concatenate([q[:, h * HEAD_DIM:(h + 1) * HEAD_DIM] for h in heads], axis=0).astype(BF16)
        sink = jnp.concatenate([jnp.full((tq, 1), sink_ref[h], F32) for h in heads], axis=0)
        cols = slice(g * HEAD_DIM, (g + 1) * HEAD_DIM)
        logits = []
        for k, mask in zip(k_list, masks):
            s = lax.dot_general(qg, k[:, cols], _NT, preferred_element_type=F32) * scale
            if mask is not None:
                s = jnp.where(mask, s, NEG_INF)
            logits.append(s)
        m = sink
        for s in logits:
            m = jnp.maximum(m, jnp.max(s, axis=-1, keepdims=True))
        denom = jnp.exp(sink - m)
        acc = jnp.zeros((heads_per_pass * tq, HEAD_DIM), F32)
        for s, v in zip(logits, v_list):
            p = jnp.exp(s - m)
            denom = denom + jnp.sum(p, axis=-1, keepdims=True)
            acc = acc + jnp.dot(p.astype(BF16), v[:, cols], preferred_element_type=F32)
        out = acc / denom
        for r, h in enumerate(heads):
            o_ref[:, h * HEAD_DIM:(h + 1) * HEAD_DIM] = out[r * tq:(r + 1) * tq]


def _attn_ctx_kernel(sink_ref, q_ref, kv_ref, o_ref):
    kv = kv_ref[...]
    k = kv[:, :KV_DIM].astype(BF16)
    v = kv[:, KV_DIM:2 * KV_DIM].astype(BF16)
    _softmax_heads(q_ref[...], [k], [v], [None], sink_ref, o_ref, 1)


def _attn_context(qkv, sink, n_batch, seq):
    return pl.pallas_call(
        _attn_ctx_kernel,
        grid=(n_batch,),
        in_specs=[
            pl.BlockSpec(memory_space=pltpu.SMEM),
            pl.BlockSpec((seq, D_MODEL), lambda b: (b, 0)),
            pl.BlockSpec((seq, D_MODEL), lambda b: (b, 2)),
        ],
        out_specs=pl.BlockSpec((seq, D_MODEL), lambda b: (b, 0)),
        out_shape=jax.ShapeDtypeStruct((n_batch * seq, D_MODEL), F32),
        compiler_params=_cparams("parallel"),
        name="attn_context",
    )(sink, qkv, qkv)


def _attn_lat_kernel(sink_ref, q_ref, qp_ref, kv0_ref, kv1_ref, kv2_ref, t0_ref, t1_ref, t2_ref,
                     kc_ref, vc_ref, o_ref, *, n_qblocks):
    qb = pl.program_id(1)
    t1 = t1_ref[...]
    cos_q = jnp.concatenate([t1[:, :KV_DIM]] * KV_REP, axis=-1)
    sin_q = jnp.concatenate([t1[:, KV_DIM:]] * KV_REP, axis=-1)
    q = q_ref[...] * cos_q + qp_ref[...] * sin_q

    def rope_k(kv_ref, t_ref):
        kv = kv_ref[...]
        t = t_ref[...]
        k = kv[:, :KV_DIM] * t[:, :KV_DIM] + kv[:, 2 * KV_DIM:3 * KV_DIM] * t[:, KV_DIM:]
        return k.astype(BF16), kv[:, KV_DIM:2 * KV_DIM].astype(BF16)

    k0, v0 = rope_k(kv0_ref, t0_ref)
    k1, v1 = rope_k(kv1_ref, t1_ref)
    k2, v2 = rope_k(kv2_ref, t2_ref)
    k_loc = jnp.concatenate([k0, k1, k2], axis=0)
    v_loc = jnp.concatenate([v0, v1, v2], axis=0)
    span = 3 * ATTN_BLOCK
    qi = lax.broadcasted_iota(jnp.int32, (ATTN_BLOCK, span), 0)
    kj = lax.broadcasted_iota(jnp.int32, (ATTN_BLOCK, span), 1) - ATTN_BLOCK
    ok = (jnp.abs(kj - qi) <= WINDOW)
    ok = ok & ((kj >= 0) | (qb > 0)) & ((kj < ATTN_BLOCK) | (qb < n_qblocks - 1))
    kc = kc_ref[0].astype(BF16)
    vc = vc_ref[0].astype(BF16)
    _softmax_heads(q, [kc, k_loc], [vc, v_loc], [None, ok], sink_ref, o_ref, KV_REP)


def _attn_latent(qkv, tables, k_ctx, v_ctx, sink, tok):
    nqb = tok.dec_seq // ATTN_BLOCK
    base = tok.n_ctx // ATTN_BLOCK
    past = k_ctx.shape[1]

    def rows(b, i, off):
        return base + b * nqb + jnp.clip(i + off, 0, nqb - 1)

    def kv_spec(off):
        return pl.BlockSpec((ATTN_BLOCK, D_MODEL), lambda b, i: (rows(b, i, off), 2))

    def tab_spec(off):
        return pl.BlockSpec((ATTN_BLOCK, 2 * KV_DIM), lambda b, i: (jnp.clip(i + off, 0, nqb - 1), 0))

    return pl.pallas_call(
        functools.partial(_attn_lat_kernel, n_qblocks=nqb),
        grid=(tok.dec_batch, nqb),
        in_specs=[
            pl.BlockSpec(memory_space=pltpu.SMEM),
            pl.BlockSpec((ATTN_BLOCK, D_MODEL), lambda b, i: (rows(b, i, 0), 0)),
            pl.BlockSpec((ATTN_BLOCK, D_MODEL), lambda b, i: (rows(b, i, 0), 1)),
            kv_spec(-1), kv_spec(0), kv_spec(1),
            tab_spec(-1), tab_spec(0), tab_spec(1),
            pl.BlockSpec((1, past, KV_DIM), lambda b, i: (b, 0, 0)),
            pl.BlockSpec((1, past, KV_DIM), lambda b, i: (b, 0, 0)),
        ],
        out_specs=pl.BlockSpec((ATTN_BLOCK, D_MODEL), lambda b, i: (b * nqb + i, 0)),
        out_shape=jax.ShapeDtypeStruct((tok.dec_batch * tok.dec_seq, D_MODEL), F32),
        compiler_params=_cparams("parallel", "arbitrary"),
        name="attn_latent",
    )(sink, qkv, qkv, qkv, qkv, qkv, tables, tables, tables, k_ctx, v_ctx)


ROUTE_TILE = 256
ROUTE_LANES = 128
NKH = PEER_NKEYS * PEER_HEADS


def _oddeven_merge_sort_pairs(n):
    pairs = []

    def merge(lo, hi, r):
        step = r * 2
        if step < hi - lo:
            merge(lo, hi, step)
            merge(lo + r, hi, step)
            pairs.extend((i, i + r) for i in range(lo + r, hi - r, step))
        else:
            pairs.append((lo, lo + r))

    def sort(lo, hi):
        if hi - lo >= 1:
            mid = lo + (hi - lo) // 2
            sort(lo, mid)
            sort(mid + 1, hi)
            merge(lo, hi, 1)

    sort(0, n - 1)
    return pairs


_SORT16 = _oddeven_merge_sort_pairs(PEER_TOPK)


def _exchange(x, i, j):
    x[i], x[j] = jnp.maximum(x[i], x[j]), jnp.minimum(x[i], x[j])


def _top16_sorted(vals):
    groups = []
    for g in range(0, len(vals), PEER_TOPK):
        x = list(vals[g:g + PEER_TOPK])
        for i, j in _SORT16:
            _exchange(x, i, j)
        groups.append(x)
    while len(groups) > 1:
        merged = []
        for a, b in zip(groups[0::2], groups[1::2]):
            x = [jnp.maximum(a[i], b[PEER_TOPK - 1 - i]) for i in range(PEER_TOPK)]
            for d in (8, 4, 2, 1):
                for i in range(PEER_TOPK):
                    if not i & d:
                        _exchange(x, i, i + d)
            merged.append(x)
        groups = merged
    return groups[0]


def _rank_bits(s, v):
    b3 = v[7] > s
    b2 = jnp.where(b3, v[11], v[3]) > s
    b1 = jnp.where(b3, jnp.where(b2, v[13], v[9]), jnp.where(b2, v[5], v[1])) > s
    hi = jnp.where(b2, jnp.where(b1, v[14], v[12]), jnp.where(b1, v[10], v[8]))
    lo = jnp.where(b2, jnp.where(b1, v[6], v[4]), jnp.where(b1, v[2], v[0]))
    b0 = jnp.where(b3, hi, lo) > s
    return (b3, b2, b1, b0), v[15] > s


def _select16(bits, leaves):
    b3, b2, b1, b0 = bits
    lvl = [jnp.where(b0, leaves[2 * i + 1], leaves[2 * i]) for i in range(8)]
    lvl = [jnp.where(b1, lvl[2 * i + 1], lvl[2 * i]) for i in range(4)]
    lvl = [jnp.where(b2, lvl[2 * i + 1], lvl[2 * i]) for i in range(2)]
    return jnp.where(b3, lvl[1], lvl[0])


def _exact_top16(s_ref, rank_ref, top_ref):
    n = s_ref.shape[0]
    rank_ref[...] = jnp.full(rank_ref.shape, float(PEER_TOPK), F32)

    def body(k, carry):
        s = s_ref[...]
        iota = lax.broadcasted_iota(jnp.int32, s.shape, 0).astype(F32)
        m = jnp.max(s, axis=0)
        idx = jnp.min(jnp.where(s == m[None], iota, float(n)), axis=0)
        hit = iota == idx[None]
        rank_ref[...] = jnp.where(hit, lax.convert_element_type(k, F32), rank_ref[...])
        s_ref[...] = jnp.where(hit, -jnp.inf, s)
        top_ref[k] = m
        return carry

    lax.fori_loop(0, PEER_TOPK, body, 0)


def _peer_route_kernel(q_ref, k1_ref, k2_ref, r2_ref, w2_ref, n1_ref, w1_ref,
                       r2_st, w2_st, n1_st, w1_st, sa_scr, ra_scr, sb_scr, rb_scr, ta_scr, tb_scr):
    q = q_ref[...].astype(BF16)
    half = PEER_HEADS * PEER_HALF
    s1_all = lax.dot_general(k1_ref[...], q[:, :half], _NT, preferred_element_type=F32)
    s2_all = lax.dot_general(k2_ref[...], q[:, half:], _NT, preferred_element_type=F32)
    slab = (PEER_HEADS, ROUTE_LANES)
    ninf = jnp.full(slab, -jnp.inf, F32)

    for j in range(ROUTE_TILE // ROUTE_LANES):
        lanes = slice(j * ROUTE_LANES, (j + 1) * ROUTE_LANES)
        s1 = [s1_all[k * PEER_HEADS:(k + 1) * PEER_HEADS, lanes] for k in range(PEER_NKEYS)]
        s2 = [s2_all[k * PEER_HEADS:(k + 1) * PEER_HEADS, lanes] for k in range(PEER_NKEYS)]
        v1 = _top16_sorted(s1)
        v2 = _top16_sorted(s2)
        cand = [[v1[a] + v2[b] for b in range(PEER_TOPK // (a + 1))] for a in range(PEER_TOPK)]
        flat = [c for row in cand for c in row]
        flat += [ninf] * (-len(flat) % PEER_TOPK)
        top = _top16_sorted(flat)
        theta = top[PEER_TOPK - 1]
        nb = []
        for row in cand:
            cnt = jnp.zeros(slab, F32)
            for c in row:
                cnt = cnt + jnp.where(c >= theta, 1.0, 0.0)
            nb.append(cnt)
        z = jnp.zeros(slab, F32)
        for k in range(PEER_TOPK):
            z = z + jnp.exp(top[k] - top[0])
        inv_z = 1.0 / z
        total = nb[0]
        for a in range(1, PEER_TOPK):
            total = total + nb[a]
        tied = total != float(PEER_TOPK)
        for k in range(PEER_TOPK - 1):
            tied = tied | ((v1[k] == v1[k + 1]) & (nb[k] != nb[k + 1])) | ((v2[k] == v2[k + 1]) & (nb[0] > float(k)))
        in1 = jnp.zeros(slab, F32)
        in2 = jnp.zeros(slab, F32)
        for k in range(PEER_NKEYS):
            rows = slice(k * PEER_HEADS, (k + 1) * PEER_HEADS)
            bits, out1 = _rank_bits(s1[k], v1)
            n1_st[rows, :] = jnp.where(out1, 0.0, _select16(bits, nb))
            w1_st[rows, :] = jnp.exp(s1[k] - v1[0]) * inv_z
            in1 = in1 + jnp.where(out1, 0.0, 1.0)
            (b3, b2, b1, b0), out2 = _rank_bits(s2[k], v2)
            r = (jnp.where(b3, 8.0, 0.0) + jnp.where(b2, 4.0, 0.0)) + (jnp.where(b1, 2.0, 0.0) + jnp.where(b0, 1.0, 0.0))
            r2_st[rows, :] = jnp.where(out2, float(PEER_TOPK), r)
            w2_st[rows, :] = jnp.exp(s2[k] - v2[0])
            in2 = in2 + jnp.where(out2, 0.0, 1.0)
        tied = tied | ((in1 != float(PEER_TOPK)) & (nb[PEER_TOPK - 1] != 0.0))
        tied = tied | ((in2 != float(PEER_TOPK)) & (nb[0] == float(PEER_TOPK)))
        n_tied = jnp.sum(jnp.where(tied, 1.0, 0.0))

        @pl.when(n_tied > 0.0)
        def _():
            shape3 = (PEER_NKEYS, PEER_HEADS, ROUTE_LANES)
            sa_scr[...] = s1_all[:, lanes].reshape(shape3)
            _exact_top16(sa_scr, ra_scr, ta_scr)
            sb_scr[...] = s2_all[:, lanes].reshape(shape3)
            _exact_top16(sb_scr, rb_scr, tb_scr)
            r1 = ra_scr[...]
            r2_st[...] = rb_scr[...].reshape(NKH, ROUTE_LANES)
            e1 = jnp.exp(s1_all[:, lanes].reshape(shape3) - ta_scr[0][None])
            w2_st[...] = jnp.exp(s2_all[:, lanes] - jnp.tile(tb_scr[0], (PEER_NKEYS, 1)))
            n_cand = len(flat)
            pos = 0
            for a in range(PEER_TOPK):
                width = PEER_TOPK // (a + 1)
                sa_scr[pos:pos + width] = ta_scr[a][None] + tb_scr[0:width]
                pos += width
            sa_scr[pos:n_cand] = jnp.full((n_cand - pos,) + slab, -jnp.inf, F32)
            _exact_top16(sa_scr.at[0:n_cand], sb_scr.at[0:n_cand], ta_scr)
            picked = jnp.where(sb_scr[0:n_cand] < float(PEER_TOPK), 1.0, 0.0)
            zz = jnp.zeros(slab, F32)
            for k in range(PEER_TOPK):
                zz = zz + jnp.exp(ta_scr[k] - ta_scr[0])
            n1 = jnp.zeros(shape3, F32)
            pos = 0
            for a in range(PEER_TOPK):
                width = PEER_TOPK // (a + 1)
                cnt = jnp.sum(picked[pos:pos + width], axis=0)
                n1 = jnp.where(r1 == float(a), cnt[None], n1)
                pos += width
            n1_st[...] = n1.reshape(NKH, ROUTE_LANES)
            w1_st[...] = (e1 / zz[None]).reshape(NKH, ROUTE_LANES)

        n1_ref[:, :, lanes] = n1_st[...].reshape(PEER_NKEYS, PEER_HEADS, ROUTE_LANES)
        w1_ref[:, :, lanes] = w1_st[...].reshape(PEER_NKEYS, PEER_HEADS, ROUTE_LANES)
        for h in range(PEER_HEADS):
            r2_ref[h, :, lanes] = r2_st[pl.ds(h, PEER_NKEYS, stride=PEER_HEADS), :].astype(BF16)
            w2_ref[h, :, lanes] = w2_st[pl.ds(h, PEER_NKEYS, stride=PEER_HEADS), :].astype(BF16)


def _route_weights(w_q, keys):
    w = w_q.reshape(D_MODEL, PEER_HEADS, 2, PEER_HALF)
    w = jnp.transpose(w, (0, 2, 1, 3)).reshape(D_MODEL, PEER_HEADS * PEER_KEY_DIM)
    eye = jnp.eye(PEER_HEADS, dtype=keys.dtype)
    big = jnp.einsum('phkd,hg->pkhgd', keys, eye).reshape(2, NKH, PEER_HEADS * PEER_HALF)
    return w.astype(BF16), big.astype(BF16)


def _peer_route(q, key_blocks, n_tokens):
    half = PEER_HEADS * PEER_HALF
    shape_kh = jax.ShapeDtypeStruct((PEER_NKEYS, PEER_HEADS, n_tokens), F32)
    shape_hk = jax.ShapeDtypeStruct((PEER_HEADS, PEER_NKEYS, n_tokens), BF16)
    spec_kh = pl.BlockSpec((PEER_NKEYS, PEER_HEADS, ROUTE_TILE), lambda i: (0, 0, i))
    spec_hk = pl.BlockSpec((PEER_HEADS, PEER_NKEYS, ROUTE_TILE), lambda i: (0, 0, i))
    stage = pltpu.VMEM((NKH, ROUTE_LANES), F32)
    cand3 = pltpu.VMEM((PEER_NKEYS, PEER_HEADS, ROUTE_LANES), F32)
    top3 = pltpu.VMEM((PEER_TOPK, PEER_HEADS, ROUTE_LANES), F32)
    return pl.pallas_call(
        _peer_route_kernel,
        grid=(n_tokens // ROUTE_TILE,),
        in_specs=[
            pl.BlockSpec((ROUTE_TILE, 2 * half), lambda i: (i, 0)),
            pl.BlockSpec((None, NKH, half), lambda i: (0, 0, 0)),
            pl.BlockSpec((None, NKH, half), lambda i: (1, 0, 0)),
        ],
        out_specs=[spec_hk, spec_hk, spec_kh, spec_kh],
        out_shape=[shape_hk, shape_hk, shape_kh, shape_kh],
        scratch_shapes=[stage, stage, stage, stage, cand3, cand3, cand3, cand3, top3, top3],
        compiler_params=_cparams("parallel"),
        name="peer_route",
    )(q, key_blocks, key_blocks)


def _tables_kernel(u_ref, v_ref, ub_ref, vt_ref):
    ub_ref[...] = u_ref[...].astype(BF16)
    vt_ref[...] = v_ref[...].T.astype(BF16)


def _prepare_tables(u_tab, v_tab):
    n_exp = u_tab.shape[0]
    row = pl.BlockSpec((TABLE_BLOCK, D_MODEL), lambda i: (i, 0))
    return pl.pallas_call(
        _tables_kernel,
        grid=(n_exp // TABLE_BLOCK,),
        in_specs=[row, row],
        out_specs=[row, pl.BlockSpec((D_MODEL, TABLE_BLOCK), lambda i: (0, i))],
        out_shape=[jax.ShapeDtypeStruct((n_exp, D_MODEL), BF16), jax.ShapeDtypeStruct((D_MODEL, n_exp), BF16)],
        compiler_params=_cparams("parallel"),
        name="peer_tables",
    )(u_tab, v_tab)


def _peer_dense_kernel(ht_ref, u_ref, vt_ref, r2_ref, w2_ref, n1_ref, w1_ref, o_ref):
    c = pl.program_id(1)
    sub = BF16_SUBLANES
    zero = jnp.zeros((), BF16)
    keys_per_sub = SUB_EXPERTS // PEER_NKEYS
    parts = []
    for sb in range(EXPERT_BLOCK // SUB_EXPERTS):
        s = jnp.dot(u_ref[sb * SUB_EXPERTS:(sb + 1) * SUB_EXPERTS, :], ht_ref[...],
                    preferred_element_type=F32)
        act = _gelu(s.astype(BF16))
        for cc in range(keys_per_sub):
            key1 = c * (EXPERT_BLOCK // PEER_NKEYS) + sb * keys_per_sub + cc
            gate = jnp.zeros((PEER_NKEYS // sub, sub, TOKEN_TILE), BF16)
            for h in range(PEER_HEADS):
                n1 = jnp.broadcast_to(n1_ref[key1, pl.ds(h, 1), :], (sub, TOKEN_TILE)).astype(BF16)
                w1 = jnp.broadcast_to(w1_ref[key1, pl.ds(h, 1), :], (sub, TOKEN_TILE)).astype(BF16)
                r2 = r2_ref[h].reshape(PEER_NKEYS // sub, sub, TOKEN_TILE)
                w2 = w2_ref[h].reshape(PEER_NKEYS // sub, sub, TOKEN_TILE)
                gate = gate + jnp.where(r2 < n1[None], w2, zero) * w1[None]
            parts.append(act[cc * PEER_NKEYS:(cc + 1) * PEER_NKEYS] * gate.reshape(PEER_NKEYS, TOKEN_TILE))
    a = jnp.concatenate(parts, axis=0)
    upd = jnp.dot(vt_ref[...], a, preferred_element_type=F32)
    o_ref[...] = jnp.where(c == 0, 0.0, o_ref[...]) + upd


def _peer_dense(tok, h_t, u_tab, vt_tab, route):
    r2, w2, n1, w1 = route
    spec_hk = pl.BlockSpec((PEER_HEADS, PEER_NKEYS, TOKEN_TILE), lambda i, c: (0, 0, i))
    spec_kh = pl.BlockSpec((PEER_NKEYS, PEER_HEADS, TOKEN_TILE), lambda i, c: (0, 0, i))
    col_spec = pl.BlockSpec((D_MODEL, TOKEN_TILE), lambda i, c: (0, i))
    return pl.pallas_call(
        _peer_dense_kernel,
        grid=(tok.tiles, PEER_EXPERTS // EXPERT_BLOCK),
        in_specs=[
            col_spec,
            pl.BlockSpec((EXPERT_BLOCK, D_MODEL), lambda i, c: (c, 0)),
            pl.BlockSpec((D_MODEL, EXPERT_BLOCK), lambda i, c: (0, c)),
            spec_hk, spec_hk, spec_kh, spec_kh,
        ],
        out_specs=col_spec,
        out_shape=jax.ShapeDtypeStruct((D_MODEL, tok.total), F32),
        compiler_params=_cparams("parallel", "arbitrary"),
        name="peer_dense",
    )(h_t, u_tab, vt_tab, r2, w2, n1, w1)


def kernel(x_prompt, x_sample, state_ssm_re, state_ssm_im, cache_k, cache_v, c, c_ctx, w_mod, b_mod, ln_g, ln_b, ssm_w_in, ssm_a_re, ssm_a_im, ssm_log_dt, ssm_b_re, ssm_b_im, ssm_c_re, ssm_c_im, ssm_d, ssm_w_glu, ssm_w_out, attn_w_qkv, attn_sink, attn_w_out, peer_w_q, peer_keys, peer_u, peer_v):
    depth = w_mod.shape[0]
    n_batch, seq, _ = x_prompt.shape
    dec_batch, dec_seq, _ = x_sample.shape
    tok = _Tokens(n_batch * seq, dec_batch, dec_seq)
    alpha = (2 * depth) ** 0.25

    cond = jnp.concatenate([c, c_ctx[None, :],
                            jnp.zeros((MOD_ROWS - dec_batch - 1, D_MODEL), F32)], axis=0)
    mods_all = _modulation_all(cond, w_mod, b_mod)

    x = jnp.concatenate([x_prompt.reshape(-1, D_MODEL), x_sample.reshape(-1, D_MODEL)], axis=0)
    st_re, st_im, st_k, st_v = [], [], [], []
    for i in range(depth):
        j = i // 2
        mods = mods_all[i]
        if i % 2 == 0:
            u = _mod_linear(tok, x, mods, ssm_w_in[j].astype(BF16), 0, "s5_in")
            ops = _s5_operators(ssm_a_re[j], ssm_a_im[j], ssm_log_dt[j], ssm_b_re[j], ssm_b_im[j],
                                ssm_c_re[j], ssm_c_im[j])
            zeros = jnp.zeros((SSM_GROUPS, n_batch, 4 * SSM_STATE), F32)
            y_c, s_c = _s5_core(_to_chunks(u[:tok.n_ctx], n_batch, seq), zeros, ops,
                                seq // SSM_CHUNK, n_batch, "s5_core_context")
            h0 = _state_to_cols(state_ssm_re[:, j], state_ssm_im[:, j])
            y_s, _ = _s5_core(_to_chunks(u[tok.n_ctx:], dec_batch, dec_seq), h0, ops,
                              dec_seq // SSM_CHUNK, dec_batch, "s5_core_latent")
            y = jnp.concatenate([_from_chunks(y_c, n_batch, seq), _from_chunks(y_s, dec_batch, dec_seq)], axis=0)
            s_re, s_im = _cols_to_state(s_c)
            st_re.append(s_re)
            st_im.append(s_im)
            x = _s5_tail(tok, y, u, x, mods, ssm_d[j], ssm_w_glu[j].astype(BF16), ssm_w_out[j].astype(BF16),
                         ln_g[i, 0], ln_b[i, 0], alpha)
        else:
            qkv = _mod_linear(tok, x, mods, _attn_weights(attn_w_qkv[j]), 0, "attn_qkv")
            o_c = _attn_context(qkv, attn_sink[j], n_batch, seq)
            past = cache_k.shape[2]
            o_s = _attn_latent(qkv, _rope_tables(dec_seq), cache_k[:, j].reshape(dec_batch, past, KV_DIM),
                               cache_v[:, j].reshape(dec_batch, past, KV_DIM), attn_sink[j], tok)
            kv_c = qkv[:tok.n_ctx, 2 * D_MODEL:2 * D_MODEL + 2 * KV_DIM]
            st_k.append(kv_c[:, :KV_DIM].reshape(n_batch, seq, N_KV_HEADS, HEAD_DIM))
            st_v.append(kv_c[:, KV_DIM:].reshape(n_batch, seq, N_KV_HEADS, HEAD_DIM))
            x = _mm_postnorm(tok, jnp.concatenate([o_c, o_s], axis=0), x, mods, attn_w_out[j].astype(BF16),
                             ln_g[i, 0], ln_b[i, 0], 2, alpha, "attn_out")
        w_q, key_blocks = _route_weights(peer_w_q[i], peer_keys[i])
        q, h_t = _mod_linear_t(tok, x, mods, w_q, 3, "peer_query")
        route = _peer_route(q, key_blocks, tok.total)
        u_tab, vt_tab = _prepare_tables(peer_u[i], peer_v[i])
        o_t = _peer_dense(tok, h_t, u_tab, vt_tab, route)
        x = _postnorm(tok, o_t, x, mods, ln_g[i, 1], ln_b[i, 1], 5, alpha, "peer_norm")

    y_prompt = x[:tok.n_ctx].reshape(n_batch, seq, D_MODEL)
    y_sample = x[tok.n_ctx:].reshape(dec_batch, dec_seq, D_MODEL)
    return (y_prompt, y_sample, jnp.stack(st_re, axis=1), jnp.stack(st_im, axis=1),
            jnp.stack(st_k, axis=1), jnp.stack(st_v, axis=1))
```

```python
import functools
import math

import jax
import jax.numpy as jnp
from jax import lax
from jax.experimental import pallas as pl
from jax.experimental.pallas import tpu as pltpu

F32 = jnp.float32
BF16 = jnp.bfloat16

D_MODEL = 1024
GRID_W = 64
SSM_GROUP = 16
SSM_GROUPS = D_MODEL // SSM_GROUP
SSM_STATE = 64
SSM_CHUNK = 16
SSM_ROW = SSM_CHUNK * SSM_GROUP
HEAD_DIM = 64
N_HEADS = D_MODEL // HEAD_DIM
N_KV_HEADS = 4
KV_REP = N_HEADS // N_KV_HEADS
KV_DIM = N_KV_HEADS * HEAD_DIM
WINDOW = 128
ATTN_BLOCK = 128
ROPE_BASE = 10000.0
ROPE_AXIS_DIM = HEAD_DIM // 2
NEG_INF = -1e30
PEER_HEADS = 8
PEER_NKEYS = 128
PEER_EXPERTS = PEER_NKEYS * PEER_NKEYS
PEER_TOPK = 16
PEER_KEY_DIM = 256
PEER_HALF = PEER_KEY_DIM // 2
LN_EPS = 1e-5

TOKEN_TILE = 512
DENSE_TILE = 1024
DENSE_EXPERTS = 2048
TABLE_BLOCK = 1024
SUB_EXPERTS = 512
MOD_ROWS = 16
BF16_SUBLANES = 16
VMEM_LIMIT = 56 * 1024 * 1024

_NT = (((1,), (1,)), ((), ()))


def _cparams(*sem):
    return pltpu.CompilerParams(dimension_semantics=sem, vmem_limit_bytes=VMEM_LIMIT)


def _gelu(x):
    return 0.5 * x * (1.0 + jnp.tanh(0.7978845608028654 * (x + 0.044715 * (x * x * x))))


def _layer_norm_rows(z, g, b):
    mu = jnp.mean(z, axis=-1, keepdims=True)
    zc = z - mu
    var = jnp.mean(zc * zc, axis=-1, keepdims=True)
    return zc * lax.rsqrt(var + LN_EPS) * g + b


def _mod_kernel(c_ref, w_ref, b_ref, o_ref):
    c = c_ref[...]
    s = c * jax.nn.sigmoid(c)
    o_ref[...] = jnp.dot(s.astype(BF16), w_ref[...].astype(BF16),
                         preferred_element_type=F32) + b_ref[...]


def _modulation_all(cond, w_mod, b_mod):
    depth = w_mod.shape[0]
    n_out = w_mod.shape[2]
    nb = n_out // D_MODEL
    out = pl.pallas_call(
        _mod_kernel,
        grid=(depth, nb),
        in_specs=[
            pl.BlockSpec((MOD_ROWS, D_MODEL), lambda i, j: (0, 0)),
            pl.BlockSpec((None, D_MODEL, D_MODEL), lambda i, j: (i, 0, j)),
            pl.BlockSpec((None, 1, D_MODEL), lambda i, j: (i, 0, j)),
        ],
        out_specs=pl.BlockSpec((None, MOD_ROWS, D_MODEL), lambda i, j: (i, 0, j)),
        out_shape=jax.ShapeDtypeStruct((depth, MOD_ROWS, n_out), F32),
        compiler_params=_cparams("arbitrary", "arbitrary"),
        name="modulation",
    )(cond, w_mod, b_mod.reshape(depth, 1, n_out))
    return out.reshape(depth, MOD_ROWS, nb, D_MODEL)


class _Tokens:
    def __init__(self, n_ctx_tokens, dec_batch, dec_seq):
        self.n_ctx = n_ctx_tokens
        self.dec_batch = dec_batch
        self.dec_seq = dec_seq
        self.total = n_ctx_tokens + dec_batch * dec_seq
        assert n_ctx_tokens % TOKEN_TILE == 0 and dec_seq % TOKEN_TILE == 0
        self.ctx_tiles = n_ctx_tokens // TOKEN_TILE
        self.tiles = self.total // TOKEN_TILE
        self.tiles_per_seq = dec_seq // TOKEN_TILE

    def mod_row(self, i):
        return jnp.where(i < self.ctx_tiles, self.dec_batch, (i - self.ctx_tiles) // self.tiles_per_seq)


def _mod_spec(tok):
    return pl.BlockSpec((1, 6, D_MODEL), lambda i: (tok.mod_row(i), 0, 0))


def _row_spec(width):
    return pl.BlockSpec((TOKEN_TILE, width), lambda i: (i, 0))


def _full_spec(shape):
    nd = len(shape)
    return pl.BlockSpec(shape, lambda i: (0,) * nd)


def _linear_kernel(x_ref, m_ref, w_ref, o_ref, *, shift_idx):
    m = m_ref[0]
    h = x_ref[...] * (1.0 + m[shift_idx + 1:shift_idx + 2]) + m[shift_idx:shift_idx + 1]
    o_ref[...] = jnp.dot(h.astype(BF16), w_ref[...], preferred_element_type=F32)


def _mod_linear(tok, x, mods, w, shift_idx, name):
    n = w.shape[1]
    return pl.pallas_call(
        functools.partial(_linear_kernel, shift_idx=shift_idx),
        grid=(tok.tiles,),
        in_specs=[_row_spec(D_MODEL), _mod_spec(tok), _full_spec((D_MODEL, n))],
        out_specs=_row_spec(n),
        out_shape=jax.ShapeDtypeStruct((tok.total, n), F32),
        compiler_params=_cparams("parallel"),
        name=name,
    )(x, mods, w)


def _col_spec():
    return pl.BlockSpec((D_MODEL, TOKEN_TILE), lambda i: (0, i))


def _linear_t_kernel(x_ref, m_ref, w_ref, o_ref, ht_ref, *, shift_idx):
    m = m_ref[0]
    h = x_ref[...] * (1.0 + m[shift_idx + 1:shift_idx + 2]) + m[shift_idx:shift_idx + 1]
    o_ref[...] = jnp.dot(h.astype(BF16), w_ref[...], preferred_element_type=F32)
    ht_ref[...] = h.T.astype(BF16)


def _mod_linear_t(tok, x, mods, w, shift_idx, name):
    n = w.shape[1]
    return pl.pallas_call(
        functools.partial(_linear_t_kernel, shift_idx=shift_idx),
        grid=(tok.tiles,),
        in_specs=[_row_spec(D_MODEL), _mod_spec(tok), _full_spec((D_MODEL, n))],
        out_specs=[_row_spec(n), _col_spec()],
        out_shape=[jax.ShapeDtypeStruct((tok.total, n), F32),
                   jax.ShapeDtypeStruct((D_MODEL, tok.total), BF16)],
        compiler_params=_cparams("parallel"),
        name=name,
    )(x, mods, w)


def _post_norm_rows(x, out, m, gate_idx, g_ref, b_ref, alpha):
    z = alpha * x + (1.0 + m[gate_idx:gate_idx + 1]) * out
    return _layer_norm_rows(z, g_ref[...], b_ref[...])


def _mm_postnorm_kernel(a_ref, x_ref, m_ref, w_ref, g_ref, b_ref, o_ref, *, gate_idx, alpha):
    y = jnp.dot(a_ref[...].astype(BF16), w_ref[...], preferred_element_type=F32)
    o_ref[...] = _post_norm_rows(x_ref[...], y, m_ref[0], gate_idx, g_ref, b_ref, alpha)


def _mm_postnorm(tok, a, x, mods, w, g, b, gate_idx, alpha, name):
    return pl.pallas_call(
        functools.partial(_mm_postnorm_kernel, gate_idx=gate_idx, alpha=alpha),
        grid=(tok.tiles,),
        in_specs=[_row_spec(D_MODEL), _row_spec(D_MODEL), _mod_spec(tok),
                  _full_spec((D_MODEL, D_MODEL)), _full_spec((1, D_MODEL)), _full_spec((1, D_MODEL))],
        out_specs=_row_spec(D_MODEL),
        out_shape=jax.ShapeDtypeStruct((tok.total, D_MODEL), F32),
        compiler_params=_cparams("parallel"),
        name=name,
    )(a, x, mods, w, g.reshape(1, D_MODEL), b.reshape(1, D_MODEL))


def _postnorm_kernel(ot_ref, x_ref, m_ref, g_ref, b_ref, o_ref, *, gate_idx, alpha):
    o_ref[...] = _post_norm_rows(x_ref[...], ot_ref[...].T, m_ref[0], gate_idx, g_ref, b_ref, alpha)


def _postnorm(tok, out_t, x, mods, g, b, gate_idx, alpha, name):
    return pl.pallas_call(
        functools.partial(_postnorm_kernel, gate_idx=gate_idx, alpha=alpha),
        grid=(tok.tiles,),
        in_specs=[_col_spec(), _row_spec(D_MODEL), _mod_spec(tok),
                  _full_spec((1, D_MODEL)), _full_spec((1, D_MODEL))],
        out_specs=_row_spec(D_MODEL),
        out_shape=jax.ShapeDtypeStruct((tok.total, D_MODEL), F32),
        compiler_params=_cparams("parallel"),
        name=name,
    )(out_t, x, mods, g.reshape(1, D_MODEL), b.reshape(1, D_MODEL))


def _s5_operators(a_re, a_im, log_dt, b_re, b_im, c_re, c_im):
    hi = lax.Precision.HIGHEST
    dt = jnp.exp(log_dt)[..., None]
    ar, ai = a_re * dt, a_im * dt
    k = jnp.arange(SSM_CHUNK + 1, dtype=F32)[:, None, None, None]
    mag = jnp.exp(k * ar[None])
    pw_re, pw_im = mag * jnp.cos(k * ai[None]), mag * jnp.sin(k * ai[None])
    l_re, l_im = pw_re[1], pw_im[1]
    den = a_re * a_re + a_im * a_im
    q_re = ((l_re - 1.0) * a_re + l_im * a_im) / den
    q_im = (l_im * a_re - (l_re - 1.0) * a_im) / den
    bt_re = q_re[..., None] * b_re - q_im[..., None] * b_im
    bt_im = q_re[..., None] * b_im + q_im[..., None] * b_re
    d_re = pw_re[..., None] * bt_re[None] - pw_im[..., None] * bt_im[None]
    d_im = pw_re[..., None] * bt_im[None] + pw_im[..., None] * bt_re[None]
    pw_re_c, pw_im_c = pw_re[:, :, :, None, :], pw_im[:, :, :, None, :]
    e_re = c_re[None] * pw_re_c - c_im[None] * pw_im_c
    e_im = c_re[None] * pw_im_c + c_im[None] * pw_re_c
    kern = (jnp.einsum('dgip,kdgpj->kdgij', c_re, d_re, precision=hi)
            - jnp.einsum('dgip,kdgpj->kdgij', c_im, d_im, precision=hi))
    s = jnp.arange(SSM_CHUNK)
    lag = s[None, :] - s[:, None]
    kf = kern[jnp.clip(lag, 0, SSM_CHUNK), 0]
    kb = kern[jnp.clip(-lag, 0, SSM_CHUNK), 1]
    mix = (jnp.where((lag >= 0)[:, :, None, None, None], kf, 0.0)
           + jnp.where((lag <= 0)[:, :, None, None, None], kb, 0.0))
    g = SSM_GROUPS
    m_op = jnp.transpose(mix, (2, 0, 4, 1, 3)).reshape(g, SSM_ROW, SSM_ROW)
    pf = (SSM_CHUNK - 1) - s
    win = jnp.stack([d_re[pf, 0], d_re[s, 1], d_im[pf, 0], d_im[s, 1]], axis=0)
    win_op = jnp.transpose(win, (2, 1, 4, 0, 3)).reshape(g, SSM_ROW, 4 * SSM_STATE)
    wout = jnp.stack([e_re[s + 1, 0], e_re[SSM_CHUNK - s, 1], -e_im[s + 1, 0], -e_im[SSM_CHUNK - s, 1]],
                     axis=0)
    wout_op = jnp.transpose(wout, (2, 0, 4, 1, 3)).reshape(g, 4 * SSM_STATE, SSM_ROW)
    lam = jnp.stack([jnp.concatenate([pw_re[SSM_CHUNK, 0], pw_re[SSM_CHUNK, 1]], axis=-1),
                     jnp.concatenate([pw_im[SSM_CHUNK, 0], pw_im[SSM_CHUNK, 1]], axis=-1)], axis=1)
    return m_op, win_op, wout_op, lam


def _s5_core_kernel(u_ref, h0_ref, m_ref, win_ref, wout_ref, lam_ref, y_ref, s_ref, z_scr, p_scr,
                    *, n_chunks, batch):
    hi = lax.Precision.HIGHEST
    p2 = 2 * SSM_STATE
    u = u_ref[...]
    z_scr[...] = jnp.dot(u, win_ref[...], preferred_element_type=F32, precision=hi)
    l_re = lam_ref[0:1, :]
    l_im = lam_ref[1:2, :]
    is_fwd = lax.broadcasted_iota(jnp.int32, (batch, p2), 1) < SSM_STATE

    def step(k, carry):
        s_re, s_im = carry
        rf = pl.multiple_of(k * batch, batch)
        rb = pl.multiple_of((n_chunks - 1 - k) * batch, batch)
        zf = z_scr[pl.ds(rf, batch), :]
        zb = z_scr[pl.ds(rb, batch), :]
        z_re = jnp.where(is_fwd, zf[:, :p2], zb[:, :p2])
        z_im = jnp.where(is_fwd, zf[:, p2:], zb[:, p2:])
        p_scr[pl.ds(rf, batch), 0:SSM_STATE] = s_re[:, :SSM_STATE]
        p_scr[pl.ds(rf, batch), p2:p2 + SSM_STATE] = s_im[:, :SSM_STATE]
        p_scr[pl.ds(rb, batch), SSM_STATE:p2] = s_re[:, SSM_STATE:]
        p_scr[pl.ds(rb, batch), p2 + SSM_STATE:] = s_im[:, SSM_STATE:]
        n_re = l_re * s_re - l_im * s_im + z_re
        n_im = l_re * s_im + l_im * s_re + z_im
        return n_re, n_im

    h0 = h0_ref[...]
    s_re, s_im = lax.fori_loop(0, n_chunks, step, (h0[:, :p2], h0[:, p2:]))
    s_ref[:, :p2] = s_re
    s_ref[:, p2:] = s_im
    y_ref[...] = (jnp.dot(u, m_ref[...], preferred_element_type=F32, precision=hi)
                  + jnp.dot(p_scr[...], wout_ref[...], preferred_element_type=F32, precision=hi))


def _s5_core(ug, h0, ops, n_chunks, batch, name):
    m_op, win_op, wout_op, lam = ops
    g, rows, _ = ug.shape
    st = 4 * SSM_STATE
    return pl.pallas_call(
        functools.partial(_s5_core_kernel, n_chunks=n_chunks, batch=batch),
        grid=(g,),
        in_specs=[
            pl.BlockSpec((None, rows, SSM_ROW), lambda i: (i, 0, 0)),
            pl.BlockSpec((None, batch, st), lambda i: (i, 0, 0)),
            pl.BlockSpec((None, SSM_ROW, SSM_ROW), lambda i: (i, 0, 0)),
            pl.BlockSpec((None, SSM_ROW, st), lambda i: (i, 0, 0)),
            pl.BlockSpec((None, st, SSM_ROW), lambda i: (i, 0, 0)),
            pl.BlockSpec((None, 2, 2 * SSM_STATE), lambda i: (i, 0, 0)),
        ],
        out_specs=[
            pl.BlockSpec((None, rows, SSM_ROW), lambda i: (i, 0, 0)),
            pl.BlockSpec((None, batch, st), lambda i: (i, 0, 0)),
        ],
        out_shape=[jax.ShapeDtypeStruct((g, rows, SSM_ROW), F32),
                   jax.ShapeDtypeStruct((g, batch, st), F32)],
        scratch_shapes=[pltpu.VMEM((rows, st), F32), pltpu.VMEM((rows, st), F32)],
        compiler_params=_cparams("parallel"),
        name=name,
    )(ug, h0, m_op, win_op, wout_op, lam)


def _to_chunks(u, batch, seq):
    n_chunks = seq // SSM_CHUNK
    t = u.reshape(batch, n_chunks, SSM_CHUNK, SSM_GROUPS, SSM_GROUP)
    return jnp.transpose(t, (3, 1, 0, 2, 4)).reshape(SSM_GROUPS, n_chunks * batch, SSM_ROW)


def _from_chunks(y, batch, seq):
    n_chunks = seq // SSM_CHUNK
    t = y.reshape(SSM_GROUPS, n_chunks, batch, SSM_CHUNK, SSM_GROUP)
    return jnp.transpose(t, (2, 1, 3, 0, 4)).reshape(batch * seq, D_MODEL)


def _state_to_cols(h_re, h_im):
    cols = jnp.concatenate([h_re[:, 0], h_re[:, 1], h_im[:, 0], h_im[:, 1]], axis=-1)
    return jnp.transpose(cols, (1, 0, 2))


def _cols_to_state(s):
    t = jnp.transpose(s, (1, 0, 2)).reshape(s.shape[1], SSM_GROUPS, 2, 2, SSM_STATE)
    t = jnp.transpose(t, (2, 0, 3, 1, 4))
    return t[0], t[1]


def _s5_tail_kernel(y_ref, u_ref, x_ref, m_ref, d_ref, wg_ref, wo_ref, g_ref, b_ref, o_ref, *, alpha):
    v = _gelu(y_ref[...] + d_ref[...] * u_ref[...])
    gl = jnp.dot(v.astype(BF16), wg_ref[...], preferred_element_type=F32)
    a = gl[:, :D_MODEL] * jax.nn.sigmoid(gl[:, D_MODEL:])
    out = jnp.dot(a.astype(BF16), wo_ref[...], preferred_element_type=F32)
    o_ref[...] = _post_norm_rows(x_ref[...], out, m_ref[0], 2, g_ref, b_ref, alpha)


def _s5_tail(tok, y, u, x, mods, d, w_glu, w_out, g, b, alpha):
    return pl.pallas_call(
        functools.partial(_s5_tail_kernel, alpha=alpha),
        grid=(tok.tiles,),
        in_specs=[_row_spec(D_MODEL), _row_spec(D_MODEL), _row_spec(D_MODEL), _mod_spec(tok),
                  _full_spec((1, D_MODEL)), _full_spec((D_MODEL, 2 * D_MODEL)),
                  _full_spec((D_MODEL, D_MODEL)), _full_spec((1, D_MODEL)), _full_spec((1, D_MODEL))],
        out_specs=_row_spec(D_MODEL),
        out_shape=jax.ShapeDtypeStruct((tok.total, D_MODEL), F32),
        compiler_params=_cparams("parallel"),
        name="s5_tail",
    )(y, u, x, mods, d.reshape(1, D_MODEL), w_glu, w_out, g.reshape(1, D_MODEL), b.reshape(1, D_MODEL))


QKV_COLS = 3 * D_MODEL


def _attn_weights(w_qkv):
    wq, wk, wv = w_qkv[:, :D_MODEL], w_qkv[:, D_MODEL:D_MODEL + KV_DIM], w_qkv[:, D_MODEL + KV_DIM:]
    half = ROPE_AXIS_DIM // 2

    def partner(w):
        n = w.shape[1]
        t = w.reshape(D_MODEL, n // ROPE_AXIS_DIM, 2, half)
        return jnp.stack([-t[:, :, 1], t[:, :, 0]], axis=2).reshape(D_MODEL, n)

    pad = jnp.zeros((D_MODEL, QKV_COLS - 2 * D_MODEL - 3 * KV_DIM), w_qkv.dtype)
    return jnp.concatenate([wq, partner(wq), wk, wv, partner(wk), pad], axis=1).astype(BF16)


def _rope_tables(seq):
    half = ROPE_AXIS_DIM // 2
    rows = seq // GRID_W
    row = jnp.repeat(jnp.arange(rows, dtype=F32), GRID_W)
    col = jnp.tile(jnp.arange(GRID_W, dtype=F32), rows)
    inv = ROPE_BASE ** (-jnp.arange(half, dtype=F32) / half)
    ang_r = row[:, None] * inv[None, :]
    ang_c = col[:, None] * inv[None, :]
    ang = jnp.concatenate([ang_r, ang_r, ang_c, ang_c], axis=-1)
    cos = jnp.tile(jnp.cos(ang), (1, N_KV_HEADS))
    sin = jnp.tile(jnp.sin(ang), (1, N_KV_HEADS))
    return jnp.concatenate([cos, sin], axis=-1)


def _softmax_heads(q, k_list, v_list, mask_list, sink_ref, o_ref, heads_per_pass):
    scale = HEAD_DIM ** -0.5
    tq = q.shape[0]
    masks = [None if m is None else jnp.concatenate([m] * heads_per_pass, axis=0) for m in mask_list]
    for first in range(0, N_HEADS, heads_per_pass):
        heads = range(first, first + heads_per_pass)
        g = first // KV_REP
        qg = jnp.concatenate([q[:, h * HEAD_DIM:(h + 1) * HEAD_DIM] for h in heads], axis=0).astype(BF16)
        sink = jnp.concatenate([jnp.full((tq, 1), sink_ref[h], F32) for h in heads], axis=0)
        cols = slice(g * HEAD_DIM, (g + 1) * HEAD_DIM)
        logits = []
        for k, mask in zip(k_list, masks):
            s = lax.dot_general(qg, k[:, cols], _NT, preferred_element_type=F32) * scale
            if mask is not None:
                s = jnp.where(mask, s, NEG_INF)
            logits.append(s)
        m = sink
        for s in logits:
            m = jnp.maximum(m, jnp.max(s, axis=-1, keepdims=True))
        denom = jnp.exp(sink - m)
        acc = jnp.zeros((heads_per_pass * tq, HEAD_DIM), F32)
        for s, v in zip(logits, v_list):
            p = jnp.exp(s - m)
            denom = denom + jnp.sum(p, axis=-1, keepdims=True)
            acc = acc + jnp.dot(p.astype(BF16), v[:, cols], preferred_element_type=F32)
        out = acc / denom
        for r, h in enumerate(heads):
            o_ref[:, h * HEAD_DIM:(h + 1) * HEAD_DIM] = out[r * tq:(r + 1) * tq]


def _attn_ctx_kernel(sink_ref, q_ref, kv_ref, o_ref):
    kv = kv_ref[...]
    k = kv[:, :KV_DIM].astype(BF16)
    v = kv[:, KV_DIM:2 * KV_DIM].astype(BF16)
    _softmax_heads(q_ref[...], [k], [v], [None], sink_ref, o_ref, 1)


def _attn_context(qkv, sink, n_batch, seq):
    return pl.pallas_call(
        _attn_ctx_kernel,
        grid=(n_batch,),
        in_specs=[
            pl.BlockSpec(memory_space=pltpu.SMEM),
            pl.BlockSpec((seq, D_MODEL), lambda b: (b, 0)),
            pl.BlockSpec((seq, D_MODEL), lambda b: (b, 2)),
        ],
        out_specs=pl.BlockSpec((seq, D_MODEL), lambda b: (b, 0)),
        out_shape=jax.ShapeDtypeStruct((n_batch * seq, D_MODEL), F32),
        compiler_params=_cparams("parallel"),
        name="attn_context",
    )(sink, qkv, qkv)


def _attn_lat_kernel(sink_ref, q_ref, qp_ref, kv0_ref, kv1_ref, kv2_ref, t0_ref, t1_ref, t2_ref,
                     kc_ref, vc_ref, o_ref, *, n_qblocks):
    qb = pl.program_id(1)
    t1 = t1_ref[...]
    cos_q = jnp.concatenate([t1[:, :KV_DIM]] * KV_REP, axis=-1)
    sin_q = jnp.concatenate([t1[:, KV_DIM:]] * KV_REP, axis=-1)
    q = q_ref[...] * cos_q + qp_ref[...] * sin_q

    def rope_k(kv_ref, t_ref):
        kv = kv_ref[...]
        t = t_ref[...]
        k = kv[:, :KV_DIM] * t[:, :KV_DIM] + kv[:, 2 * KV_DIM:3 * KV_DIM] * t[:, KV_DIM:]
        return k.astype(BF16), kv[:, KV_DIM:2 * KV_DIM].astype(BF16)

    k0, v0 = rope_k(kv0_ref, t0_ref)
    k1, v1 = rope_k(kv1_ref, t1_ref)
    k2, v2 = rope_k(kv2_ref, t2_ref)
    k_loc = jnp.concatenate([k0, k1, k2], axis=0)
    v_loc = jnp.concatenate([v0, v1, v2], axis=0)
    span = 3 * ATTN_BLOCK
    qi = lax.broadcasted_iota(jnp.int32, (ATTN_BLOCK, span), 0)
    kj = lax.broadcasted_iota(jnp.int32, (ATTN_BLOCK, span), 1) - ATTN_BLOCK
    ok = (jnp.abs(kj - qi) <= WINDOW)
    ok = ok & ((kj >= 0) | (qb > 0)) & ((kj < ATTN_BLOCK) | (qb < n_qblocks - 1))
    kc = kc_ref[0].astype(BF16)
    vc = vc_ref[0].astype(BF16)
    _softmax_heads(q, [kc, k_loc], [vc, v_loc], [None, ok], sink_ref, o_ref, KV_REP)


def _attn_latent(qkv, tables, k_ctx, v_ctx, sink, tok):
    nqb = tok.dec_seq // ATTN_BLOCK
    base = tok.n_ctx // ATTN_BLOCK
    past = k_ctx.shape[1]

    def rows(b, i, off):
        return base + b * nqb + jnp.clip(i + off, 0, nqb - 1)

    def kv_spec(off):
        return pl.BlockSpec((ATTN_BLOCK, D_MODEL), lambda b, i: (rows(b, i, off), 2))

    def tab_spec(off):
        return pl.BlockSpec((ATTN_BLOCK, 2 * KV_DIM), lambda b, i: (jnp.clip(i + off, 0, nqb - 1), 0))

    return pl.pallas_call(
        functools.partial(_attn_lat_kernel, n_qblocks=nqb),
        grid=(tok.dec_batch, nqb),
        in_specs=[
            pl.BlockSpec(memory_space=pltpu.SMEM),
            pl.BlockSpec((ATTN_BLOCK, D_MODEL), lambda b, i: (rows(b, i, 0), 0)),
            pl.BlockSpec((ATTN_BLOCK, D_MODEL), lambda b, i: (rows(b, i, 0), 1)),
            kv_spec(-1), kv_spec(0), kv_spec(1),
            tab_spec(-1), tab_spec(0), tab_spec(1),
            pl.BlockSpec((1, past, KV_DIM), lambda b, i: (b, 0, 0)),
            pl.BlockSpec((1, past, KV_DIM), lambda b, i: (b, 0, 0)),
        ],
        out_specs=pl.BlockSpec((ATTN_BLOCK, D_MODEL), lambda b, i: (b * nqb + i, 0)),
        out_shape=jax.ShapeDtypeStruct((tok.dec_batch * tok.dec_seq, D_MODEL), F32),
        compiler_params=_cparams("parallel", "arbitrary"),
        name="attn_latent",
    )(sink, qkv, qkv, qkv, qkv, qkv, tables, tables, tables, k_ctx, v_ctx)


ROUTE_TILE = 256
ROUTE_LANES = 128
NKH = PEER_NKEYS * PEER_HEADS


def _oddeven_merge_sort_pairs(n):
    pairs = []

    def merge(lo, hi, r):
        step = r * 2
        if step < hi - lo:
            merge(lo, hi, step)
            merge(lo + r, hi, step)
            pairs.extend((i, i + r) for i in range(lo + r, hi - r, step))
        else:
            pairs.append((lo, lo + r))

    def sort(lo, hi):
        if hi - lo >= 1:
            mid = lo + (hi - lo) // 2
            sort(lo, mid)
            sort(mid + 1, hi)
            merge(lo, hi, 1)

    sort(0, n - 1)
    return pairs


_SORT16 = _oddeven_merge_sort_pairs(PEER_TOPK)


def _exchange(x, i, j):
    x[i], x[j] = jnp.maximum(x[i], x[j]), jnp.minimum(x[i], x[j])


def _top16_sorted(vals):
    groups = []
    for g in range(0, len(vals), PEER_TOPK):
        x = list(vals[g:g + PEER_TOPK])
        for i, j in _SORT16:
            _exchange(x, i, j)
        groups.append(x)
    while len(groups) > 1:
        merged = []
        for a, b in zip(groups[0::2], groups[1::2]):
            x = [jnp.maximum(a[i], b[PEER_TOPK - 1 - i]) for i in range(PEER_TOPK)]
            for d in (8, 4, 2, 1):
                for i in range(PEER_TOPK):
                    if not i & d:
                        _exchange(x, i, i + d)
            merged.append(x)
        groups = merged
    return groups[0]


def _rank_bits(s, v):
    b3 = v[7] > s
    b2 = jnp.where(b3, v[11], v[3]) > s
    b1 = jnp.where(b3, jnp.where(b2, v[13], v[9]), jnp.where(b2, v[5], v[1])) > s
    hi = jnp.where(b2, jnp.where(b1, v[14], v[12]), jnp.where(b1, v[10], v[8]))
    lo = jnp.where(b2, jnp.where(b1, v[6], v[4]), jnp.where(b1, v[2], v[0]))
    b0 = jnp.where(b3, hi, lo) > s
    return (b3, b2, b1, b0), v[15] > s


def _select16(bits, leaves):
    b3, b2, b1, b0 = bits
    lvl = [jnp.where(b0, leaves[2 * i + 1], leaves[2 * i]) for i in range(8)]
    lvl = [jnp.where(b1, lvl[2 * i + 1], lvl[2 * i]) for i in range(4)]
    lvl = [jnp.where(b2, lvl[2 * i + 1], lvl[2 * i]) for i in range(2)]
    return jnp.where(b3, lvl[1], lvl[0])


def _exact_top16(s_ref, rank_ref, top_ref):
    n = s_ref.shape[0]
    rank_ref[...] = jnp.full(rank_ref.shape, float(PEER_TOPK), F32)

    def body(k, carry):
        s = s_ref[...]
        iota = lax.broadcasted_iota(jnp.int32, s.shape, 0).astype(F32)
        m = jnp.max(s, axis=0)
        idx = jnp.min(jnp.where(s == m[None], iota, float(n)), axis=0)
        hit = iota == idx[None]
        rank_ref[...] = jnp.where(hit, lax.convert_element_type(k, F32), rank_ref[...])
        s_ref[...] = jnp.where(hit, -jnp.inf, s)
        top_ref[k] = m
        return carry

    lax.fori_loop(0, PEER_TOPK, body, 0)


def _peer_route_kernel(q_ref, k1_ref, k2_ref, r2_ref, w2_ref, n1_ref, w1_ref,
                       r2_st, w2_st, n1_st, w1_st, sa_scr, ra_scr, sb_scr, rb_scr, ta_scr, tb_scr):
    q = q_ref[...].astype(BF16)
    half = PEER_HEADS * PEER_HALF
    s1_all = lax.dot_general(k1_ref[...], q[:, :half], _NT, preferred_element_type=F32)
    s2_all = lax.dot_general(k2_ref[...], q[:, half:], _NT, preferred_element_type=F32)
    slab = (PEER_HEADS, ROUTE_LANES)
    ninf = jnp.full(slab, -jnp.inf, F32)

    for j in range(ROUTE_TILE // ROUTE_LANES):
        lanes = slice(j * ROUTE_LANES, (j + 1) * ROUTE_LANES)
        s1 = [s1_all[k * PEER_HEADS:(k + 1) * PEER_HEADS, lanes] for k in range(PEER_NKEYS)]
        s2 = [s2_all[k * PEER_HEADS:(k + 1) * PEER_HEADS, lanes] for k in range(PEER_NKEYS)]
        v1 = _top16_sorted(s1)
        v2 = _top16_sorted(s2)
        cand = [[v1[a] + v2[b] for b in range(PEER_TOPK // (a + 1))] for a in range(PEER_TOPK)]
        flat = [c for row in cand for c in row]
        flat += [ninf] * (-len(flat) % PEER_TOPK)
        top = _top16_sorted(flat)
        theta = top[PEER_TOPK - 1]
        nb = []
        for row in cand:
            cnt = jnp.zeros(slab, F32)
            for c in row:
                cnt = cnt + jnp.where(c >= theta, 1.0, 0.0)
            nb.append(cnt)
        z = jnp.zeros(slab, F32)
        for k in range(PEER_TOPK):
            z = z + jnp.exp(top[k] - top[0])
        inv_z = 1.0 / z
        total = nb[0]
        for a in range(1, PEER_TOPK):
            total = total + nb[a]
        tied = total != float(PEER_TOPK)
        for k in range(PEER_TOPK - 1):
            tied = tied | ((v1[k] == v1[k + 1]) & (nb[k] != nb[k + 1])) | ((v2[k] == v2[k + 1]) & (nb[0] > float(k)))
        in1 = jnp.zeros(slab, F32)
        in2 = jnp.zeros(slab, F32)
        for k in range(PEER_NKEYS):
            rows = slice(k * PEER_HEADS, (k + 1) * PEER_HEADS)
            bits, out1 = _rank_bits(s1[k], v1)
            n1_st[rows, :] = jnp.where(out1, 0.0, _select16(bits, nb))
            w1_st[rows, :] = jnp.exp(s1[k] - v1[0]) * inv_z
            in1 = in1 + jnp.where(out1, 0.0, 1.0)
            (b3, b2, b1, b0), out2 = _rank_bits(s2[k], v2)
            r = (jnp.where(b3, 8.0, 0.0) + jnp.where(b2, 4.0, 0.0)) + (jnp.where(b1, 2.0, 0.0) + jnp.where(b0, 1.0, 0.0))
            r2_st[rows, :] = jnp.where(out2, float(PEER_TOPK), r)
            w2_st[rows, :] = jnp.exp(s2[k] - v2[0])
            in2 = in2 + jnp.where(out2, 0.0, 1.0)
        tied = tied | ((in1 != float(PEER_TOPK)) & (nb[PEER_TOPK - 1] != 0.0))
        tied = tied | ((in2 != float(PEER_TOPK)) & (nb[0] == float(PEER_TOPK)))
        n_tied = jnp.sum(jnp.where(tied, 1.0, 0.0))

        @pl.when(n_tied > 0.0)
        def _():
            shape3 = (PEER_NKEYS, PEER_HEADS, ROUTE_LANES)
            sa_scr[...] = s1_all[:, lanes].reshape(shape3)
            _exact_top16(sa_scr, ra_scr, ta_scr)
            sb_scr[...] = s2_all[:, lanes].reshape(shape3)
            _exact_top16(sb_scr, rb_scr, tb_scr)
            r1 = ra_scr[...]
            r2_st[...] = rb_scr[...].reshape(NKH, ROUTE_LANES)
            e1 = jnp.exp(s1_all[:, lanes].reshape(shape3) - ta_scr[0][None])
            w2_st[...] = jnp.exp(s2_all[:, lanes] - jnp.tile(tb_scr[0], (PEER_NKEYS, 1)))
            n_cand = len(flat)
            pos = 0
            for a in range(PEER_TOPK):
                width = PEER_TOPK // (a + 1)
                sa_scr[pos:pos + width] = ta_scr[a][None] + tb_scr[0:width]
                pos += width
            sa_scr[pos:n_cand] = jnp.full((n_cand - pos,) + slab, -jnp.inf, F32)
            _exact_top16(sa_scr.at[0:n_cand], sb_scr.at[0:n_cand], ta_scr)
            picked = jnp.where(sb_scr[0:n_cand] < float(PEER_TOPK), 1.0, 0.0)
            zz = jnp.zeros(slab, F32)
            for k in range(PEER_TOPK):
                zz = zz + jnp.exp(ta_scr[k] - ta_scr[0])
            n1 = jnp.zeros(shape3, F32)
            pos = 0
            for a in range(PEER_TOPK):
                width = PEER_TOPK // (a + 1)
                cnt = jnp.sum(picked[pos:pos + width], axis=0)
                n1 = jnp.where(r1 == float(a), cnt[None], n1)
                pos += width
            n1_st[...] = n1.reshape(NKH, ROUTE_LANES)
            w1_st[...] = (e1 / zz[None]).reshape(NKH, ROUTE_LANES)

        n1_ref[:, :, lanes] = n1_st[...].reshape(PEER_NKEYS, PEER_HEADS, ROUTE_LANES)
        w1_ref[:, :, lanes] = w1_st[...].reshape(PEER_NKEYS, PEER_HEADS, ROUTE_LANES)
        for h in range(PEER_HEADS):
            r2_ref[h, :, lanes] = r2_st[pl.ds(h, PEER_NKEYS, stride=PEER_HEADS), :].astype(BF16)
            w2_ref[h, :, lanes] = w2_st[pl.ds(h, PEER_NKEYS, stride=PEER_HEADS), :].astype(BF16)


def _route_weights(w_q, keys):
    w = w_q.reshape(D_MODEL, PEER_HEADS, 2, PEER_HALF)
    w = jnp.transpose(w, (0, 2, 1, 3)).reshape(D_MODEL, PEER_HEADS * PEER_KEY_DIM)
    eye = jnp.eye(PEER_HEADS, dtype=keys.dtype)
    big = jnp.einsum('phkd,hg->pkhgd', keys, eye).reshape(2, NKH, PEER_HEADS * PEER_HALF)
    return w.astype(BF16), big.astype(BF16)


def _peer_route(q, key_blocks, n_tokens):
    half = PEER_HEADS * PEER_HALF
    shape_kh = jax.ShapeDtypeStruct((PEER_NKEYS, PEER_HEADS, n_tokens), F32)
    shape_hk = jax.ShapeDtypeStruct((PEER_HEADS, PEER_NKEYS, n_tokens), BF16)
    spec_kh = pl.BlockSpec((PEER_NKEYS, PEER_HEADS, ROUTE_TILE), lambda i: (0, 0, i))
    spec_hk = pl.BlockSpec((PEER_HEADS, PEER_NKEYS, ROUTE_TILE), lambda i: (0, 0, i))
    stage = pltpu.VMEM((NKH, ROUTE_LANES), F32)
    cand3 = pltpu.VMEM((PEER_NKEYS, PEER_HEADS, ROUTE_LANES), F32)
    top3 = pltpu.VMEM((PEER_TOPK, PEER_HEADS, ROUTE_LANES), F32)
    return pl.pallas_call(
        _peer_route_kernel,
        grid=(n_tokens // ROUTE_TILE,),
        in_specs=[
            pl.BlockSpec((ROUTE_TILE, 2 * half), lambda i: (i, 0)),
            pl.BlockSpec((None, NKH, half), lambda i: (0, 0, 0)),
            pl.BlockSpec((None, NKH, half), lambda i: (1, 0, 0)),
        ],
        out_specs=[spec_hk, spec_hk, spec_kh, spec_kh],
        out_shape=[shape_hk, shape_hk, shape_kh, shape_kh],
        scratch_shapes=[stage, stage, stage, stage, cand3, cand3, cand3, cand3, top3, top3],
        compiler_params=_cparams("parallel"),
        name="peer_route",
    )(q, key_blocks, key_blocks)


def _tables_kernel(u_ref, v_ref, ub_ref, vt_ref):
    ub_ref[...] = u_ref[...].astype(BF16)
    vt_ref[...] = v_ref[...].T.astype(BF16)


def _prepare_tables(u_tab, v_tab):
    n_exp = u_tab.shape[0]
    row = pl.BlockSpec((TABLE_BLOCK, D_MODEL), lambda i: (i, 0))
    return pl.pallas_call(
        _tables_kernel,
        grid=(n_exp // TABLE_BLOCK,),
        in_specs=[row, row],
        out_specs=[row, pl.BlockSpec((D_MODEL, TABLE_BLOCK), lambda i: (0, i))],
        out_shape=[jax.ShapeDtypeStruct((n_exp, D_MODEL), BF16), jax.ShapeDtypeStruct((D_MODEL, n_exp), BF16)],
        compiler_params=_cparams("parallel"),
        name="peer_tables",
    )(u_tab, v_tab)


def _peer_dense_kernel(ht_ref, u_ref, vt_ref, r2_ref, w2_ref, n1_ref, w1_ref, o_ref):
    c = pl.program_id(1)
    sub = BF16_SUBLANES
    zero = jnp.zeros((), BF16)
    keys_per_sub = SUB_EXPERTS // PEER_NKEYS
    parts = []
    for sb in range(DENSE_EXPERTS // SUB_EXPERTS):
        s = jnp.dot(u_ref[sb * SUB_EXPERTS:(sb + 1) * SUB_EXPERTS, :], ht_ref[...],
                    preferred_element_type=F32)
        act = _gelu(s.astype(BF16))
        for cc in range(keys_per_sub):
            key1 = c * (DENSE_EXPERTS // PEER_NKEYS) + sb * keys_per_sub + cc
            gate = jnp.zeros((PEER_NKEYS // sub, sub, DENSE_TILE), BF16)
            for h in range(PEER_HEADS):
                n1 = jnp.broadcast_to(n1_ref[key1, pl.ds(h, 1), :], (sub, DENSE_TILE)).astype(BF16)
                w1 = jnp.broadcast_to(w1_ref[key1, pl.ds(h, 1), :], (sub, DENSE_TILE)).astype(BF16)
                r2 = r2_ref[h].reshape(PEER_NKEYS // sub, sub, DENSE_TILE)
                w2 = w2_ref[h].reshape(PEER_NKEYS // sub, sub, DENSE_TILE)
                gate = gate + jnp.where(r2 < n1[None], w2, zero) * w1[None]
            parts.append(act[cc * PEER_NKEYS:(cc + 1) * PEER_NKEYS] * gate.reshape(PEER_NKEYS, DENSE_TILE))
    a = jnp.concatenate(parts, axis=0)
    upd = jnp.dot(vt_ref[...], a, preferred_element_type=F32)
    o_ref[...] = jnp.where(c == 0, 0.0, o_ref[...]) + upd


def _peer_dense(tok, h_t, u_tab, vt_tab, route):
    r2, w2, n1, w1 = route
    once = dict(pipeline_mode=pl.Buffered(1))
    spec_hk = pl.BlockSpec((PEER_HEADS, PEER_NKEYS, DENSE_TILE), lambda i, c: (0, 0, i))
    spec_kh = pl.BlockSpec((PEER_NKEYS, PEER_HEADS, DENSE_TILE), lambda i, c: (0, 0, i), **once)
    col_spec = pl.BlockSpec((D_MODEL, DENSE_TILE), lambda i, c: (0, i))
    return pl.pallas_call(
        _peer_dense_kernel,
        grid=(tok.total // DENSE_TILE, PEER_EXPERTS // DENSE_EXPERTS),
        in_specs=[
            col_spec,
            pl.BlockSpec((DENSE_EXPERTS, D_MODEL), lambda i, c: (c, 0)),
            pl.BlockSpec((D_MODEL, DENSE_EXPERTS), lambda i, c: (0, c)),
            spec_hk, spec_hk, spec_kh, spec_kh,
        ],
        out_specs=col_spec,
        out_shape=jax.ShapeDtypeStruct((D_MODEL, tok.total), F32),
        compiler_params=_cparams("parallel", "arbitrary"),
        name="peer_dense",
    )(h_t, u_tab, vt_tab, r2, w2, n1, w1)


def kernel(x_prompt, x_sample, state_ssm_re, state_ssm_im, cache_k, cache_v, c, c_ctx, w_mod, b_mod, ln_g, ln_b, ssm_w_in, ssm_a_re, ssm_a_im, ssm_log_dt, ssm_b_re, ssm_b_im, ssm_c_re, ssm_c_im, ssm_d, ssm_w_glu, ssm_w_out, attn_w_qkv, attn_sink, attn_w_out, peer_w_q, peer_keys, peer_u, peer_v):
    depth = w_mod.shape[0]
    n_batch, seq, _ = x_prompt.shape
    dec_batch, dec_seq, _ = x_sample.shape
    tok = _Tokens(n_batch * seq, dec_batch, dec_seq)
    alpha = (2 * depth) ** 0.25

    cond = jnp.concatenate([c, c_ctx[None, :],
                            jnp.zeros((MOD_ROWS - dec_batch - 1, D_MODEL), F32)], axis=0)
    mods_all = _modulation_all(cond, w_mod, b_mod)

    x = jnp.concatenate([x_prompt.reshape(-1, D_MODEL), x_sample.reshape(-1, D_MODEL)], axis=0)
    st_re, st_im, st_k, st_v = [], [], [], []
    for i in range(depth):
        j = i // 2
        mods = mods_all[i]
        if i % 2 == 0:
            u = _mod_linear(tok, x, mods, ssm_w_in[j].astype(BF16), 0, "s5_in")
            ops = _s5_operators(ssm_a_re[j], ssm_a_im[j], ssm_log_dt[j], ssm_b_re[j], ssm_b_im[j],
                                ssm_c_re[j], ssm_c_im[j])
            zeros = jnp.zeros((SSM_GROUPS, n_batch, 4 * SSM_STATE), F32)
            y_c, s_c = _s5_core(_to_chunks(u[:tok.n_ctx], n_batch, seq), zeros, ops,
                                seq // SSM_CHUNK, n_batch, "s5_core_context")
            h0 = _state_to_cols(state_ssm_re[:, j], state_ssm_im[:, j])
            y_s, _ = _s5_core(_to_chunks(u[tok.n_ctx:], dec_batch, dec_seq), h0, ops,
                              dec_seq // SSM_CHUNK, dec_batch, "s5_core_latent")
            y = jnp.concatenate([_from_chunks(y_c, n_batch, seq), _from_chunks(y_s, dec_batch, dec_seq)], axis=0)
            s_re, s_im = _cols_to_state(s_c)
            st_re.append(s_re)
            st_im.append(s_im)
            x = _s5_tail(tok, y, u, x, mods, ssm_d[j], ssm_w_glu[j].astype(BF16), ssm_w_out[j].astype(BF16),
                         ln_g[i, 0], ln_b[i, 0], alpha)
        else:
            qkv = _mod_linear(tok, x, mods, _attn_weights(attn_w_qkv[j]), 0, "attn_qkv")
            o_c = _attn_context(qkv, attn_sink[j], n_batch, seq)
            past = cache_k.shape[2]
            o_s = _attn_latent(qkv, _rope_tables(dec_seq), cache_k[:, j].reshape(dec_batch, past, KV_DIM),
                               cache_v[:, j].reshape(dec_batch, past, KV_DIM), attn_sink[j], tok)
            kv_c = qkv[:tok.n_ctx, 2 * D_MODEL:2 * D_MODEL + 2 * KV_DIM]
            st_k.append(kv_c[:, :KV_DIM].reshape(n_batch, seq, N_KV_HEADS, HEAD_DIM))
            st_v.append(kv_c[:, KV_DIM:].reshape(n_batch, seq, N_KV_HEADS, HEAD_DIM))
            x = _mm_postnorm(tok, jnp.concatenate([o_c, o_s], axis=0), x, mods, attn_w_out[j].astype(BF16),
                             ln_g[i, 0], ln_b[i, 0], 2, alpha, "attn_out")
        w_q, key_blocks = _route_weights(peer_w_q[i], peer_keys[i])
        q, h_t = _mod_linear_t(tok, x, mods, w_q, 3, "peer_query")
        route = _peer_route(q, key_blocks, tok.total)
        u_tab, vt_tab = _prepare_tables(peer_u[i], peer_v[i])
        o_t = _peer_dense(tok, h_t, u_tab, vt_tab, route)
        x = _postnorm(tok, o_t, x, mods, ln_g[i, 1], ln_b[i, 1], 5, alpha, "peer_norm")

    y_prompt = x[:tok.n_ctx].reshape(n_batch, seq, D_MODEL)
    y_sample = x[tok.n_ctx:].reshape(dec_batch, dec_seq, D_MODEL)
    return (y_prompt, y_sample, jnp.stack(st_re, axis=1), jnp.stack(st_im, axis=1),
            jnp.stack(st_k, axis=1), jnp.stack(st_v, axis=1))
```

```python
import functools
import math

import jax
import jax.numpy as jnp
from jax import lax
from jax.experimental import pallas as pl
from jax.experimental.pallas import tpu as pltpu

F32 = jnp.float32
BF16 = jnp.bfloat16

D_MODEL = 1024
GRID_W = 64
SSM_GROUP = 16
SSM_GROUPS = D_MODEL // SSM_GROUP
SSM_STATE = 64
SSM_CHUNK = 16
SSM_ROW = SSM_CHUNK * SSM_GROUP
HEAD_DIM = 64
N_HEADS = D_MODEL // HEAD_DIM
N_KV_HEADS = 4
KV_REP = N_HEADS // N_KV_HEADS
KV_DIM = N_KV_HEADS * HEAD_DIM
WINDOW = 128
ATTN_BLOCK = 128
ROPE_BASE = 10000.0
ROPE_AXIS_DIM = HEAD_DIM // 2
NEG_INF = -1e30
PEER_HEADS = 8
PEER_NKEYS = 128
PEER_EXPERTS = PEER_NKEYS * PEER_NKEYS
PEER_TOPK = 16
PEER_KEY_DIM = 256
PEER_HALF = PEER_KEY_DIM // 2
LN_EPS = 1e-5

TOKEN_TILE = 512
DENSE_TILE = 512
DENSE_EXPERTS = 2048
TABLE_BLOCK = 1024
SUB_EXPERTS = 512
MOD_ROWS = 16
BF16_SUBLANES = 16
VMEM_LIMIT = 56 * 1024 * 1024

_NT = (((1,), (1,)), ((), ()))


def _cparams(*sem):
    return pltpu.CompilerParams(dimension_semantics=sem, vmem_limit_bytes=VMEM_LIMIT)


def _gelu(x):
    return 0.5 * x * (1.0 + jnp.tanh(0.7978845608028654 * (x + 0.044715 * (x * x * x))))


def _layer_norm_rows(z, g, b):
    mu = jnp.mean(z, axis=-1, keepdims=True)
    zc = z - mu
    var = jnp.mean(zc * zc, axis=-1, keepdims=True)
    return zc * lax.rsqrt(var + LN_EPS) * g + b


def _mod_kernel(c_ref, w_ref, b_ref, o_ref):
    c = c_ref[...]
    s = c * jax.nn.sigmoid(c)
    o_ref[...] = jnp.dot(s.astype(BF16), w_ref[...].astype(BF16),
                         preferred_element_type=F32) + b_ref[...]


def _modulation_all(cond, w_mod, b_mod):
    depth = w_mod.shape[0]
    n_out = w_mod.shape[2]
    nb = n_out // D_MODEL
    out = pl.pallas_call(
        _mod_kernel,
        grid=(depth, nb),
        in_specs=[
            pl.BlockSpec((MOD_ROWS, D_MODEL), lambda i, j: (0, 0)),
            pl.BlockSpec((None, D_MODEL, D_MODEL), lambda i, j: (i, 0, j)),
            pl.BlockSpec((None, 1, D_MODEL), lambda i, j: (i, 0, j)),
        ],
        out_specs=pl.BlockSpec((None, MOD_ROWS, D_MODEL), lambda i, j: (i, 0, j)),
        out_shape=jax.ShapeDtypeStruct((depth, MOD_ROWS, n_out), F32),
        compiler_params=_cparams("arbitrary", "arbitrary"),
        name="modulation",
    )(cond, w_mod, b_mod.reshape(depth, 1, n_out))
    return out.reshape(depth, MOD_ROWS, nb, D_MODEL)


class _Tokens:
    def __init__(self, n_ctx_tokens, dec_batch, dec_seq):
        self.n_ctx = n_ctx_tokens
        self.dec_batch = dec_batch
        self.dec_seq = dec_seq
        self.total = n_ctx_tokens + dec_batch * dec_seq
        assert n_ctx_tokens % TOKEN_TILE == 0 and dec_seq % TOKEN_TILE == 0
        self.ctx_tiles = n_ctx_tokens // TOKEN_TILE
        self.tiles = self.total // TOKEN_TILE
        self.tiles_per_seq = dec_seq // TOKEN_TILE

    def mod_row(self, i):
        return jnp.where(i < self.ctx_tiles, self.dec_batch, (i - self.ctx_tiles) // self.tiles_per_seq)


def _mod_spec(tok):
    return pl.BlockSpec((1, 6, D_MODEL), lambda i: (tok.mod_row(i), 0, 0))


def _row_spec(width):
    return pl.BlockSpec((TOKEN_TILE, width), lambda i: (i, 0))


def _full_spec(shape):
    nd = len(shape)
    return pl.BlockSpec(shape, lambda i: (0,) * nd)


def _linear_kernel(x_ref, m_ref, w_ref, o_ref, *, shift_idx):
    m = m_ref[0]
    h = x_ref[...] * (1.0 + m[shift_idx + 1:shift_idx + 2]) + m[shift_idx:shift_idx + 1]
    o_ref[...] = jnp.dot(h.astype(BF16), w_ref[...], preferred_element_type=F32)


def _mod_linear(tok, x, mods, w, shift_idx, name):
    n = w.shape[1]
    return pl.pallas_call(
        functools.partial(_linear_kernel, shift_idx=shift_idx),
        grid=(tok.tiles,),
        in_specs=[_row_spec(D_MODEL), _mod_spec(tok), _full_spec((D_MODEL, n))],
        out_specs=_row_spec(n),
        out_shape=jax.ShapeDtypeStruct((tok.total, n), F32),
        compiler_params=_cparams("parallel"),
        name=name,
    )(x, mods, w)


def _col_spec():
    return pl.BlockSpec((D_MODEL, TOKEN_TILE), lambda i: (0, i))


def _linear_t_kernel(x_ref, m_ref, w_ref, o_ref, ht_ref, *, shift_idx):
    m = m_ref[0]
    h = x_ref[...] * (1.0 + m[shift_idx + 1:shift_idx + 2]) + m[shift_idx:shift_idx + 1]
    o_ref[...] = jnp.dot(h.astype(BF16), w_ref[...], preferred_element_type=F32)
    ht_ref[...] = h.T.astype(BF16)


def _mod_linear_t(tok, x, mods, w, shift_idx, name):
    n = w.shape[1]
    return pl.pallas_call(
        functools.partial(_linear_t_kernel, shift_idx=shift_idx),
        grid=(tok.tiles,),
        in_specs=[_row_spec(D_MODEL), _mod_spec(tok), _full_spec((D_MODEL, n))],
        out_specs=[_row_spec(n), _col_spec()],
        out_shape=[jax.ShapeDtypeStruct((tok.total, n), F32),
                   jax.ShapeDtypeStruct((D_MODEL, tok.total), BF16)],
        compiler_params=_cparams("parallel"),
        name=name,
    )(x, mods, w)


def _post_norm_rows(x, out, m, gate_idx, g_ref, b_ref, alpha):
    z = alpha * x + (1.0 + m[gate_idx:gate_idx + 1]) * out
    return _layer_norm_rows(z, g_ref[...], b_ref[...])


def _mm_postnorm_kernel(a_ref, x_ref, m_ref, w_ref, g_ref, b_ref, o_ref, *, gate_idx, alpha):
    y = jnp.dot(a_ref[...].astype(BF16), w_ref[...], preferred_element_type=F32)
    o_ref[...] = _post_norm_rows(x_ref[...], y, m_ref[0], gate_idx, g_ref, b_ref, alpha)


def _mm_postnorm(tok, a, x, mods, w, g, b, gate_idx, alpha, name):
    return pl.pallas_call(
        functools.partial(_mm_postnorm_kernel, gate_idx=gate_idx, alpha=alpha),
        grid=(tok.tiles,),
        in_specs=[_row_spec(D_MODEL), _row_spec(D_MODEL), _mod_spec(tok),
                  _full_spec((D_MODEL, D_MODEL)), _full_spec((1, D_MODEL)), _full_spec((1, D_MODEL))],
        out_specs=_row_spec(D_MODEL),
        out_shape=jax.ShapeDtypeStruct((tok.total, D_MODEL), F32),
        compiler_params=_cparams("parallel"),
        name=name,
    )(a, x, mods, w, g.reshape(1, D_MODEL), b.reshape(1, D_MODEL))


def _postnorm_kernel(ot_ref, x_ref, m_ref, g_ref, b_ref, o_ref, *, gate_idx, alpha):
    o_ref[...] = _post_norm_rows(x_ref[...], ot_ref[...].T, m_ref[0], gate_idx, g_ref, b_ref, alpha)


def _postnorm(tok, out_t, x, mods, g, b, gate_idx, alpha, name):
    return pl.pallas_call(
        functools.partial(_postnorm_kernel, gate_idx=gate_idx, alpha=alpha),
        grid=(tok.tiles,),
        in_specs=[_col_spec(), _row_spec(D_MODEL), _mod_spec(tok),
                  _full_spec((1, D_MODEL)), _full_spec((1, D_MODEL))],
        out_specs=_row_spec(D_MODEL),
        out_shape=jax.ShapeDtypeStruct((tok.total, D_MODEL), F32),
        compiler_params=_cparams("parallel"),
        name=name,
    )(out_t, x, mods, g.reshape(1, D_MODEL), b.reshape(1, D_MODEL))


def _s5_operators(a_re, a_im, log_dt, b_re, b_im, c_re, c_im):
    hi = lax.Precision.HIGHEST
    dt = jnp.exp(log_dt)[..., None]
    ar, ai = a_re * dt, a_im * dt
    k = jnp.arange(SSM_CHUNK + 1, dtype=F32)[:, None, None, None]
    mag = jnp.exp(k * ar[None])
    pw_re, pw_im = mag * jnp.cos(k * ai[None]), mag * jnp.sin(k * ai[None])
    l_re, l_im = pw_re[1], pw_im[1]
    den = a_re * a_re + a_im * a_im
    q_re = ((l_re - 1.0) * a_re + l_im * a_im) / den
    q_im = (l_im * a_re - (l_re - 1.0) * a_im) / den
    bt_re = q_re[..., None] * b_re - q_im[..., None] * b_im
    bt_im = q_re[..., None] * b_im + q_im[..., None] * b_re
    d_re = pw_re[..., None] * bt_re[None] - pw_im[..., None] * bt_im[None]
    d_im = pw_re[..., None] * bt_im[None] + pw_im[..., None] * bt_re[None]
    pw_re_c, pw_im_c = pw_re[:, :, :, None, :], pw_im[:, :, :, None, :]
    e_re = c_re[None] * pw_re_c - c_im[None] * pw_im_c
    e_im = c_re[None] * pw_im_c + c_im[None] * pw_re_c
    kern = (jnp.einsum('dgip,kdgpj->kdgij', c_re, d_re, precision=hi)
            - jnp.einsum('dgip,kdgpj->kdgij', c_im, d_im, precision=hi))
    s = jnp.arange(SSM_CHUNK)
    lag = s[None, :] - s[:, None]
    kf = kern[jnp.clip(lag, 0, SSM_CHUNK), 0]
    kb = kern[jnp.clip(-lag, 0, SSM_CHUNK), 1]
    mix = (jnp.where((lag >= 0)[:, :, None, None, None], kf, 0.0)
           + jnp.where((lag <= 0)[:, :, None, None, None], kb, 0.0))
    g = SSM_GROUPS
    m_op = jnp.transpose(mix, (2, 0, 4, 1, 3)).reshape(g, SSM_ROW, SSM_ROW)
    pf = (SSM_CHUNK - 1) - s
    win = jnp.stack([d_re[pf, 0], d_re[s, 1], d_im[pf, 0], d_im[s, 1]], axis=0)
    win_op = jnp.transpose(win, (2, 1, 4, 0, 3)).reshape(g, SSM_ROW, 4 * SSM_STATE)
    wout = jnp.stack([e_re[s + 1, 0], e_re[SSM_CHUNK - s, 1], -e_im[s + 1, 0], -e_im[SSM_CHUNK - s, 1]],
                     axis=0)
    wout_op = jnp.transpose(wout, (2, 0, 4, 1, 3)).reshape(g, 4 * SSM_STATE, SSM_ROW)
    lam = jnp.stack([jnp.concatenate([pw_re[SSM_CHUNK, 0], pw_re[SSM_CHUNK, 1]], axis=-1),
                     jnp.concatenate([pw_im[SSM_CHUNK, 0], pw_im[SSM_CHUNK, 1]], axis=-1)], axis=1)
    return m_op, win_op, wout_op, lam


def _s5_core_kernel(u_ref, h0_ref, m_ref, win_ref, wout_ref, lam_ref, y_ref, s_ref, z_scr, p_scr,
                    *, n_chunks, batch):
    hi = lax.Precision.HIGHEST
    p2 = 2 * SSM_STATE
    u = u_ref[...]
    z_scr[...] = jnp.dot(u, win_ref[...], preferred_element_type=F32, precision=hi)
    l_re = lam_ref[0:1, :]
    l_im = lam_ref[1:2, :]
    is_fwd = lax.broadcasted_iota(jnp.int32, (batch, p2), 1) < SSM_STATE

    def step(k, carry):
        s_re, s_im = carry
        rf = pl.multiple_of(k * batch, batch)
        rb = pl.multiple_of((n_chunks - 1 - k) * batch, batch)
        zf = z_scr[pl.ds(rf, batch), :]
        zb = z_scr[pl.ds(rb, batch), :]
        z_re = jnp.where(is_fwd, zf[:, :p2], zb[:, :p2])
        z_im = jnp.where(is_fwd, zf[:, p2:], zb[:, p2:])
        p_scr[pl.ds(rf, batch), 0:SSM_STATE] = s_re[:, :SSM_STATE]
        p_scr[pl.ds(rf, batch), p2:p2 + SSM_STATE] = s_im[:, :SSM_STATE]
        p_scr[pl.ds(rb, batch), SSM_STATE:p2] = s_re[:, SSM_STATE:]
        p_scr[pl.ds(rb, batch), p2 + SSM_STATE:] = s_im[:, SSM_STATE:]
        n_re = l_re * s_re - l_im * s_im + z_re
        n_im = l_re * s_im + l_im * s_re + z_im
        return n_re, n_im

    h0 = h0_ref[...]
    s_re, s_im = lax.fori_loop(0, n_chunks, step, (h0[:, :p2], h0[:, p2:]))
    s_ref[:, :p2] = s_re
    s_ref[:, p2:] = s_im
    y_ref[...] = (jnp.dot(u, m_ref[...], preferred_element_type=F32, precision=hi)
                  + jnp.dot(p_scr[...], wout_ref[...], preferred_element_type=F32, precision=hi))


def _s5_core(ug, h0, ops, n_chunks, batch, name):
    m_op, win_op, wout_op, lam = ops
    g, rows, _ = ug.shape
    st = 4 * SSM_STATE
    return pl.pallas_call(
        functools.partial(_s5_core_kernel, n_chunks=n_chunks, batch=batch),
        grid=(g,),
        in_specs=[
            pl.BlockSpec((None, rows, SSM_ROW), lambda i: (i, 0, 0)),
            pl.BlockSpec((None, batch, st), lambda i: (i, 0, 0)),
            pl.BlockSpec((None, SSM_ROW, SSM_ROW), lambda i: (i, 0, 0)),
            pl.BlockSpec((None, SSM_ROW, st), lambda i: (i, 0, 0)),
            pl.BlockSpec((None, st, SSM_ROW), lambda i: (i, 0, 0)),
            pl.BlockSpec((None, 2, 2 * SSM_STATE), lambda i: (i, 0, 0)),
        ],
        out_specs=[
            pl.BlockSpec((None, rows, SSM_ROW), lambda i: (i, 0, 0)),
            pl.BlockSpec((None, batch, st), lambda i: (i, 0, 0)),
        ],
        out_shape=[jax.ShapeDtypeStruct((g, rows, SSM_ROW), F32),
                   jax.ShapeDtypeStruct((g, batch, st), F32)],
        scratch_shapes=[pltpu.VMEM((rows, st), F32), pltpu.VMEM((rows, st), F32)],
        compiler_params=_cparams("parallel"),
        name=name,
    )(ug, h0, m_op, win_op, wout_op, lam)


def _to_chunks(u, batch, seq):
    n_chunks = seq // SSM_CHUNK
    t = u.reshape(batch, n_chunks, SSM_CHUNK, SSM_GROUPS, SSM_GROUP)
    return jnp.transpose(t, (3, 1, 0, 2, 4)).reshape(SSM_GROUPS, n_chunks * batch, SSM_ROW)


def _from_chunks(y, batch, seq):
    n_chunks = seq // SSM_CHUNK
    t = y.reshape(SSM_GROUPS, n_chunks, batch, SSM_CHUNK, SSM_GROUP)
    return jnp.transpose(t, (2, 1, 3, 0, 4)).reshape(batch * seq, D_MODEL)


def _state_to_cols(h_re, h_im):
    cols = jnp.concatenate([h_re[:, 0], h_re[:, 1], h_im[:, 0], h_im[:, 1]], axis=-1)
    return jnp.transpose(cols, (1, 0, 2))


def _cols_to_state(s):
    t = jnp.transpose(s, (1, 0, 2)).reshape(s.shape[1], SSM_GROUPS, 2, 2, SSM_STATE)
    t = jnp.transpose(t, (2, 0, 3, 1, 4))
    return t[0], t[1]


def _s5_tail_kernel(y_ref, u_ref, x_ref, m_ref, d_ref, wg_ref, wo_ref, g_ref, b_ref, o_ref, *, alpha):
    v = _gelu(y_ref[...] + d_ref[...] * u_ref[...])
    gl = jnp.dot(v.astype(BF16), wg_ref[...], preferred_element_type=F32)
    a = gl[:, :D_MODEL] * jax.nn.sigmoid(gl[:, D_MODEL:])
    out = jnp.dot(a.astype(BF16), wo_ref[...], preferred_element_type=F32)
    o_ref[...] = _post_norm_rows(x_ref[...], out, m_ref[0], 2, g_ref, b_ref, alpha)


def _s5_tail(tok, y, u, x, mods, d, w_glu, w_out, g, b, alpha):
    return pl.pallas_call(
        functools.partial(_s5_tail_kernel, alpha=alpha),
        grid=(tok.tiles,),
        in_specs=[_row_spec(D_MODEL), _row_spec(D_MODEL), _row_spec(D_MODEL), _mod_spec(tok),
                  _full_spec((1, D_MODEL)), _full_spec((D_MODEL, 2 * D_MODEL)),
                  _full_spec((D_MODEL, D_MODEL)), _full_spec((1, D_MODEL)), _full_spec((1, D_MODEL))],
        out_specs=_row_spec(D_MODEL),
        out_shape=jax.ShapeDtypeStruct((tok.total, D_MODEL), F32),
        compiler_params=_cparams("parallel"),
        name="s5_tail",
    )(y, u, x, mods, d.reshape(1, D_MODEL), w_glu, w_out, g.reshape(1, D_MODEL), b.reshape(1, D_MODEL))


QKV_COLS = 3 * D_MODEL


def _attn_weights(w_qkv):
    wq, wk, wv = w_qkv[:, :D_MODEL], w_qkv[:, D_MODEL:D_MODEL + KV_DIM], w_qkv[:, D_MODEL + KV_DIM:]
    half = ROPE_AXIS_DIM // 2

    def partner(w):
        n = w.shape[1]
        t = w.reshape(D_MODEL, n // ROPE_AXIS_DIM, 2, half)
        return jnp.stack([-t[:, :, 1], t[:, :, 0]], axis=2).reshape(D_MODEL, n)

    pad = jnp.zeros((D_MODEL, QKV_COLS - 2 * D_MODEL - 3 * KV_DIM), w_qkv.dtype)
    return jnp.concatenate([wq, partner(wq), wk, wv, partner(wk), pad], axis=1).astype(BF16)


def _rope_tables(seq):
    half = ROPE_AXIS_DIM // 2
    rows = seq // GRID_W
    row = jnp.repeat(jnp.arange(rows, dtype=F32), GRID_W)
    col = jnp.tile(jnp.arange(GRID_W, dtype=F32), rows)
    inv = ROPE_BASE ** (-jnp.arange(half, dtype=F32) / half)
    ang_r = row[:, None] * inv[None, :]
    ang_c = col[:, None] * inv[None, :]
    ang = jnp.concatenate([ang_r, ang_r, ang_c, ang_c], axis=-1)
    cos = jnp.tile(jnp.cos(ang), (1, N_KV_HEADS))
    sin = jnp.tile(jnp.sin(ang), (1, N_KV_HEADS))
    return jnp.concatenate([cos, sin], axis=-1)


def _softmax_heads(q, k_list, v_list, mask_list, sink_ref, o_ref, heads_per_pass):
    scale = HEAD_DIM ** -0.5
    tq = q.shape[0]
    masks = [None if m is None else jnp.concatenate([m] * heads_per_pass, axis=0) for m in mask_list]
    for first in range(0, N_HEADS, heads_per_pass):
        heads = range(first, first + heads_per_pass)
        g = first // KV_REP
        qg = jnp.concatenate([q[:, h * HEAD_DIM:(h + 1) * HEAD_DIM] for h in heads], axis=0).astype(BF16)
        sink = jnp.concatenate([jnp.full((tq, 1), sink_ref[h], F32) for h in heads], axis=0)
        cols = slice(g * HEAD_DIM, (g + 1) * HEAD_DIM)
        logits = []
        for k, mask in zip(k_list, masks):
            s = lax.dot_general(qg, k[:, cols], _NT, preferred_element_type=F32) * scale
            if mask is not None:
                s = jnp.where(mask, s, NEG_INF)
            logits.append(s)
        m = sink
        for s in logits:
            m = jnp.maximum(m, jnp.max(s, axis=-1, keepdims=True))
        denom = jnp.exp(sink - m)
        acc = jnp.zeros((heads_per_pass * tq, HEAD_DIM), F32)
        for s, v in zip(logits, v_list):
            p = jnp.exp(s - m)
            denom = denom + jnp.sum(p, axis=-1, keepdims=True)
            acc = acc + jnp.dot(p.astype(BF16), v[:, cols], preferred_element_type=F32)
        out = acc / denom
        for r, h in enumerate(heads):
            o_ref[:, h * HEAD_DIM:(h + 1) * HEAD_DIM] = out[r * tq:(r + 1) * tq]


def _attn_ctx_kernel(sink_ref, q_ref, kv_ref, o_ref, k_ref, v_ref):
    kv = kv_ref[...]
    k_ref[...] = kv[:, :KV_DIM]
    v_ref[...] = kv[:, KV_DIM:2 * KV_DIM]
    k = kv[:, :KV_DIM].astype(BF16)
    v = kv[:, KV_DIM:2 * KV_DIM].astype(BF16)
    _softmax_heads(q_ref[...], [k], [v], [None], sink_ref, o_ref, 1)


def _attn_context(qkv, sink, n_batch, seq):
    kv_spec = pl.BlockSpec((seq, KV_DIM), lambda b: (b, 0))
    kv_shape = jax.ShapeDtypeStruct((n_batch * seq, KV_DIM), F32)
    return pl.pallas_call(
        _attn_ctx_kernel,
        grid=(n_batch,),
        in_specs=[
            pl.BlockSpec(memory_space=pltpu.SMEM),
            pl.BlockSpec((seq, D_MODEL), lambda b: (b, 0)),
            pl.BlockSpec((seq, D_MODEL), lambda b: (b, 2)),
        ],
        out_specs=[pl.BlockSpec((seq, D_MODEL), lambda b: (b, 0)), kv_spec, kv_spec],
        out_shape=[jax.ShapeDtypeStruct((n_batch * seq, D_MODEL), F32), kv_shape, kv_shape],
        compiler_params=_cparams("parallel"),
        name="attn_context",
    )(sink, qkv, qkv)


def _attn_lat_kernel(sink_ref, q_ref, qp_ref, kv0_ref, kv1_ref, kv2_ref, t0_ref, t1_ref, t2_ref,
                     kc_ref, vc_ref, o_ref, *, n_qblocks):
    qb = pl.program_id(1)
    t1 = t1_ref[...]
    cos_q = jnp.concatenate([t1[:, :KV_DIM]] * KV_REP, axis=-1)
    sin_q = jnp.concatenate([t1[:, KV_DIM:]] * KV_REP, axis=-1)
    q = q_ref[...] * cos_q + qp_ref[...] * sin_q

    def rope_k(kv_ref, t_ref):
        kv = kv_ref[...]
        t = t_ref[...]
        k = kv[:, :KV_DIM] * t[:, :KV_DIM] + kv[:, 2 * KV_DIM:3 * KV_DIM] * t[:, KV_DIM:]
        return k.astype(BF16), kv[:, KV_DIM:2 * KV_DIM].astype(BF16)

    k0, v0 = rope_k(kv0_ref, t0_ref)
    k1, v1 = rope_k(kv1_ref, t1_ref)
    k2, v2 = rope_k(kv2_ref, t2_ref)
    k_loc = jnp.concatenate([k0, k1, k2], axis=0)
    v_loc = jnp.concatenate([v0, v1, v2], axis=0)
    span = 3 * ATTN_BLOCK
    qi = lax.broadcasted_iota(jnp.int32, (ATTN_BLOCK, span), 0)
    kj = lax.broadcasted_iota(jnp.int32, (ATTN_BLOCK, span), 1) - ATTN_BLOCK
    ok = (jnp.abs(kj - qi) <= WINDOW)
    ok = ok & ((kj >= 0) | (qb > 0)) & ((kj < ATTN_BLOCK) | (qb < n_qblocks - 1))
    kc = kc_ref[0].astype(BF16)
    vc = vc_ref[0].astype(BF16)
    _softmax_heads(q, [kc, k_loc], [vc, v_loc], [None, ok], sink_ref, o_ref, KV_REP)


def _attn_latent(qkv, tables, k_ctx, v_ctx, sink, tok):
    nqb = tok.dec_seq // ATTN_BLOCK
    base = tok.n_ctx // ATTN_BLOCK
    past = k_ctx.shape[1]

    def rows(b, i, off):
        return base + b * nqb + jnp.clip(i + off, 0, nqb - 1)

    def kv_spec(off):
        return pl.BlockSpec((ATTN_BLOCK, D_MODEL), lambda b, i: (rows(b, i, off), 2))

    def tab_spec(off):
        return pl.BlockSpec((ATTN_BLOCK, 2 * KV_DIM), lambda b, i: (jnp.clip(i + off, 0, nqb - 1), 0))

    return pl.pallas_call(
        functools.partial(_attn_lat_kernel, n_qblocks=nqb),
        grid=(tok.dec_batch, nqb),
        in_specs=[
            pl.BlockSpec(memory_space=pltpu.SMEM),
            pl.BlockSpec((ATTN_BLOCK, D_MODEL), lambda b, i: (rows(b, i, 0), 0)),
            pl.BlockSpec((ATTN_BLOCK, D_MODEL), lambda b, i: (rows(b, i, 0), 1)),
            kv_spec(-1), kv_spec(0), kv_spec(1),
            tab_spec(-1), tab_spec(0), tab_spec(1),
            pl.BlockSpec((1, past, KV_DIM), lambda b, i: (b, 0, 0)),
            pl.BlockSpec((1, past, KV_DIM), lambda b, i: (b, 0, 0)),
        ],
        out_specs=pl.BlockSpec((ATTN_BLOCK, D_MODEL), lambda b, i: (b * nqb + i, 0)),
        out_shape=jax.ShapeDtypeStruct((tok.dec_batch * tok.dec_seq, D_MODEL), F32),
        compiler_params=_cparams("parallel", "arbitrary"),
        name="attn_latent",
    )(sink, qkv, qkv, qkv, qkv, qkv, tables, tables, tables, k_ctx, v_ctx)


ROUTE_TILE = 256
ROUTE_LANES = 128
NKH = PEER_NKEYS * PEER_HEADS


def _oddeven_merge_sort_pairs(n):
    pairs = []

    def merge(lo, hi, r):
        step = r * 2
        if step < hi - lo:
            merge(lo, hi, step)
            merge(lo + r, hi, step)
            pairs.extend((i, i + r) for i in range(lo + r, hi - r, step))
        else:
            pairs.append((lo, lo + r))

    def sort(lo, hi):
        if hi - lo >= 1:
            mid = lo + (hi - lo) // 2
            sort(lo, mid)
            sort(mid + 1, hi)
            merge(lo, hi, 1)

    sort(0, n - 1)
    return pairs


_SORT16 = _oddeven_merge_sort_pairs(PEER_TOPK)


def _exchange(x, i, j):
    x[i], x[j] = jnp.maximum(x[i], x[j]), jnp.minimum(x[i], x[j])


def _top16_sorted(vals):
    groups = []
    for g in range(0, len(vals), PEER_TOPK):
        x = list(vals[g:g + PEER_TOPK])
        for i, j in _SORT16:
            _exchange(x, i, j)
        groups.append(x)
    while len(groups) > 1:
        merged = []
        for a, b in zip(groups[0::2], groups[1::2]):
            x = [jnp.maximum(a[i], b[PEER_TOPK - 1 - i]) for i in range(PEER_TOPK)]
            for d in (8, 4, 2, 1):
                for i in range(PEER_TOPK):
                    if not i & d:
                        _exchange(x, i, i + d)
            merged.append(x)
        groups = merged
    return groups[0]


def _rank_bits(s, v):
    b3 = v[7] > s
    b2 = jnp.where(b3, v[11], v[3]) > s
    b1 = jnp.where(b3, jnp.where(b2, v[13], v[9]), jnp.where(b2, v[5], v[1])) > s
    hi = jnp.where(b2, jnp.where(b1, v[14], v[12]), jnp.where(b1, v[10], v[8]))
    lo = jnp.where(b2, jnp.where(b1, v[6], v[4]), jnp.where(b1, v[2], v[0]))
    b0 = jnp.where(b3, hi, lo) > s
    return (b3, b2, b1, b0), v[15] > s


def _select16(bits, leaves):
    b3, b2, b1, b0 = bits
    lvl = [jnp.where(b0, leaves[2 * i + 1], leaves[2 * i]) for i in range(8)]
    lvl = [jnp.where(b1, lvl[2 * i + 1], lvl[2 * i]) for i in range(4)]
    lvl = [jnp.where(b2, lvl[2 * i + 1], lvl[2 * i]) for i in range(2)]
    return jnp.where(b3, lvl[1], lvl[0])


def _exact_top16(s_ref, rank_ref, top_ref):
    n = s_ref.shape[0]
    rank_ref[...] = jnp.full(rank_ref.shape, float(PEER_TOPK), F32)

    def body(k, carry):
        s = s_ref[...]
        iota = lax.broadcasted_iota(jnp.int32, s.shape, 0).astype(F32)
        m = jnp.max(s, axis=0)
        idx = jnp.min(jnp.where(s == m[None], iota, float(n)), axis=0)
        hit = iota == idx[None]
        rank_ref[...] = jnp.where(hit, lax.convert_element_type(k, F32), rank_ref[...])
        s_ref[...] = jnp.where(hit, -jnp.inf, s)
        top_ref[k] = m
        return carry

    lax.fori_loop(0, PEER_TOPK, body, 0)


def _peer_route_kernel(q_ref, k1_ref, k2_ref, r2_ref, w2_ref, n1_ref, w1_ref,
                       r2_st, w2_st, n1_st, w1_st, sa_scr, ra_scr, sb_scr, rb_scr, ta_scr, tb_scr):
    q = q_ref[...].astype(BF16)
    half = PEER_HEADS * PEER_HALF
    s1_all = lax.dot_general(k1_ref[...], q[:, :half], _NT, preferred_element_type=F32)
    s2_all = lax.dot_general(k2_ref[...], q[:, half:], _NT, preferred_element_type=F32)
    slab = (PEER_HEADS, ROUTE_LANES)
    ninf = jnp.full(slab, -jnp.inf, F32)

    for j in range(ROUTE_TILE // ROUTE_LANES):
        lanes = slice(j * ROUTE_LANES, (j + 1) * ROUTE_LANES)
        s1 = [s1_all[k * PEER_HEADS:(k + 1) * PEER_HEADS, lanes] for k in range(PEER_NKEYS)]
        s2 = [s2_all[k * PEER_HEADS:(k + 1) * PEER_HEADS, lanes] for k in range(PEER_NKEYS)]
        v1 = _top16_sorted(s1)
        v2 = _top16_sorted(s2)
        cand = [[v1[a] + v2[b] for b in range(PEER_TOPK // (a + 1))] for a in range(PEER_TOPK)]
        flat = [c for row in cand for c in row]
        flat += [ninf] * (-len(flat) % PEER_TOPK)
        top = _top16_sorted(flat)
        theta = top[PEER_TOPK - 1]
        nb = []
        for row in cand:
            cnt = jnp.zeros(slab, F32)
            for c in row:
                cnt = cnt + jnp.where(c >= theta, 1.0, 0.0)
            nb.append(cnt)
        z = jnp.zeros(slab, F32)
        for k in range(PEER_TOPK):
            z = z + jnp.exp(top[k] - top[0])
        inv_z = 1.0 / z
        total = nb[0]
        for a in range(1, PEER_TOPK):
            total = total + nb[a]
        tied = total != float(PEER_TOPK)
        for k in range(PEER_TOPK - 1):
            tied = tied | ((v1[k] == v1[k + 1]) & (nb[k] != nb[k + 1])) | ((v2[k] == v2[k + 1]) & (nb[0] > float(k)))
        in1 = jnp.zeros(slab, F32)
        in2 = jnp.zeros(slab, F32)
        for k in range(PEER_NKEYS):
            rows = slice(k * PEER_HEADS, (k + 1) * PEER_HEADS)
            bits, out1 = _rank_bits(s1[k], v1)
            n1_st[rows, :] = jnp.where(out1, 0.0, _select16(bits, nb))
            w1_st[rows, :] = jnp.exp(s1[k] - v1[0]) * inv_z
            in1 = in1 + jnp.where(out1, 0.0, 1.0)
            (b3, b2, b1, b0), out2 = _rank_bits(s2[k], v2)
            r = (jnp.where(b3, 8.0, 0.0) + jnp.where(b2, 4.0, 0.0)) + (jnp.where(b1, 2.0, 0.0) + jnp.where(b0, 1.0, 0.0))
            r2_st[rows, :] = jnp.where(out2, float(PEER_TOPK), r)
            w2_st[rows, :] = jnp.exp(s2[k] - v2[0])
            in2 = in2 + jnp.where(out2, 0.0, 1.0)
        tied = tied | ((in1 != float(PEER_TOPK)) & (nb[PEER_TOPK - 1] != 0.0))
        tied = tied | ((in2 != float(PEER_TOPK)) & (nb[0] == float(PEER_TOPK)))
        n_tied = jnp.sum(jnp.where(tied, 1.0, 0.0))

        @pl.when(n_tied > 0.0)
        def _():
            shape3 = (PEER_NKEYS, PEER_HEADS, ROUTE_LANES)
            sa_scr[...] = s1_all[:, lanes].reshape(shape3)
            _exact_top16(sa_scr, ra_scr, ta_scr)
            sb_scr[...] = s2_all[:, lanes].reshape(shape3)
            _exact_top16(sb_scr, rb_scr, tb_scr)
            r1 = ra_scr[...]
            r2_st[...] = rb_scr[...].reshape(NKH, ROUTE_LANES)
            e1 = jnp.exp(s1_all[:, lanes].reshape(shape3) - ta_scr[0][None])
            w2_st[...] = jnp.exp(s2_all[:, lanes] - jnp.tile(tb_scr[0], (PEER_NKEYS, 1)))
            n_cand = len(flat)
            pos = 0
            for a in range(PEER_TOPK):
                width = PEER_TOPK // (a + 1)
                sa_scr[pos:pos + width] = ta_scr[a][None] + tb_scr[0:width]
                pos += width
            sa_scr[pos:n_cand] = jnp.full((n_cand - pos,) + slab, -jnp.inf, F32)
            _exact_top16(sa_scr.at[0:n_cand], sb_scr.at[0:n_cand], ta_scr)
            picked = jnp.where(sb_scr[0:n_cand] < float(PEER_TOPK), 1.0, 0.0)
            zz = jnp.zeros(slab, F32)
            for k in range(PEER_TOPK):
                zz = zz + jnp.exp(ta_scr[k] - ta_scr[0])
            n1 = jnp.zeros(shape3, F32)
            pos = 0
            for a in range(PEER_TOPK):
                width = PEER_TOPK // (a + 1)
                cnt = jnp.sum(picked[pos:pos + width], axis=0)
                n1 = jnp.where(r1 == float(a), cnt[None], n1)
                pos += width
            n1_st[...] = n1.reshape(NKH, ROUTE_LANES)
            w1_st[...] = (e1 / zz[None]).reshape(NKH, ROUTE_LANES)

        n1_ref[:, :, lanes] = n1_st[...].reshape(PEER_NKEYS, PEER_HEADS, ROUTE_LANES)
        w1_ref[:, :, lanes] = w1_st[...].reshape(PEER_NKEYS, PEER_HEADS, ROUTE_LANES)
        for h in range(PEER_HEADS):
            r2_ref[h, :, lanes] = r2_st[pl.ds(h, PEER_NKEYS, stride=PEER_HEADS), :].astype(BF16)
            w2_ref[h, :, lanes] = w2_st[pl.ds(h, PEER_NKEYS, stride=PEER_HEADS), :].astype(BF16)


def _route_weights(w_q, keys):
    w = w_q.reshape(D_MODEL, PEER_HEADS, 2, PEER_HALF)
    w = jnp.transpose(w, (0, 2, 1, 3)).reshape(D_MODEL, PEER_HEADS * PEER_KEY_DIM)
    eye = jnp.eye(PEER_HEADS, dtype=keys.dtype)
    big = jnp.einsum('phkd,hg->pkhgd', keys, eye).reshape(2, NKH, PEER_HEADS * PEER_HALF)
    return w.astype(BF16), big.astype(BF16)


def _peer_route(q, key_blocks, n_tokens):
    half = PEER_HEADS * PEER_HALF
    shape_kh = jax.ShapeDtypeStruct((PEER_NKEYS, PEER_HEADS, n_tokens), F32)
    shape_hk = jax.ShapeDtypeStruct((PEER_HEADS, PEER_NKEYS, n_tokens), BF16)
    spec_kh = pl.BlockSpec((PEER_NKEYS, PEER_HEADS, ROUTE_TILE), lambda i: (0, 0, i))
    spec_hk = pl.BlockSpec((PEER_HEADS, PEER_NKEYS, ROUTE_TILE), lambda i: (0, 0, i))
    stage = pltpu.VMEM((NKH, ROUTE_LANES), F32)
    cand3 = pltpu.VMEM((PEER_NKEYS, PEER_HEADS, ROUTE_LANES), F32)
    top3 = pltpu.VMEM((PEER_TOPK, PEER_HEADS, ROUTE_LANES), F32)
    return pl.pallas_call(
        _peer_route_kernel,
        grid=(n_tokens // ROUTE_TILE,),
        in_specs=[
            pl.BlockSpec((ROUTE_TILE, 2 * half), lambda i: (i, 0)),
            pl.BlockSpec((None, NKH, half), lambda i: (0, 0, 0)),
            pl.BlockSpec((None, NKH, half), lambda i: (1, 0, 0)),
        ],
        out_specs=[spec_hk, spec_hk, spec_kh, spec_kh],
        out_shape=[shape_hk, shape_hk, shape_kh, shape_kh],
        scratch_shapes=[stage, stage, stage, stage, cand3, cand3, cand3, cand3, top3, top3],
        compiler_params=_cparams("parallel"),
        name="peer_route",
    )(q, key_blocks, key_blocks)


def _tables_kernel(u_ref, v_ref, ub_ref, vt_ref):
    ub_ref[...] = u_ref[...].astype(BF16)
    vt_ref[...] = v_ref[...].T.astype(BF16)


def _prepare_tables(u_tab, v_tab):
    n_exp = u_tab.shape[0]
    row = pl.BlockSpec((TABLE_BLOCK, D_MODEL), lambda i: (i, 0))
    return pl.pallas_call(
        _tables_kernel,
        grid=(n_exp // TABLE_BLOCK,),
        in_specs=[row, row],
        out_specs=[row, pl.BlockSpec((D_MODEL, TABLE_BLOCK), lambda i: (0, i))],
        out_shape=[jax.ShapeDtypeStruct((n_exp, D_MODEL), BF16), jax.ShapeDtypeStruct((D_MODEL, n_exp), BF16)],
        compiler_params=_cparams("parallel"),
        name="peer_tables",
    )(u_tab, v_tab)


def _peer_dense_kernel(ht_ref, u_ref, vt_ref, r2_ref, w2_ref, n1_ref, w1_ref, o_ref):
    c = pl.program_id(1)
    sub = BF16_SUBLANES
    zero = jnp.zeros((), BF16)
    keys_per_sub = SUB_EXPERTS // PEER_NKEYS
    parts = []
    for sb in range(DENSE_EXPERTS // SUB_EXPERTS):
        s = jnp.dot(u_ref[sb * SUB_EXPERTS:(sb + 1) * SUB_EXPERTS, :], ht_ref[...],
                    preferred_element_type=F32)
        act = _gelu(s.astype(BF16))
        for cc in range(keys_per_sub):
            key1 = c * (DENSE_EXPERTS // PEER_NKEYS) + sb * keys_per_sub + cc
            gate = jnp.zeros((PEER_NKEYS // sub, sub, DENSE_TILE), BF16)
            for h in range(PEER_HEADS):
                n1 = jnp.broadcast_to(n1_ref[key1, pl.ds(h, 1), :], (sub, DENSE_TILE)).astype(BF16)
                w1 = jnp.broadcast_to(w1_ref[key1, pl.ds(h, 1), :], (sub, DENSE_TILE)).astype(BF16)
                r2 = r2_ref[h].reshape(PEER_NKEYS // sub, sub, DENSE_TILE)
                w2 = w2_ref[h].reshape(PEER_NKEYS // sub, sub, DENSE_TILE)
                gate = gate + jnp.where(r2 < n1[None], w2, zero) * w1[None]
            parts.append(act[cc * PEER_NKEYS:(cc + 1) * PEER_NKEYS] * gate.reshape(PEER_NKEYS, DENSE_TILE))
    a = jnp.concatenate(parts, axis=0)
    upd = jnp.dot(vt_ref[...], a, preferred_element_type=F32)
    o_ref[...] = jnp.where(c == 0, 0.0, o_ref[...]) + upd


def _peer_dense(tok, h_t, u_tab, vt_tab, route):
    r2, w2, n1, w1 = route
    spec_hk = pl.BlockSpec((PEER_HEADS, PEER_NKEYS, DENSE_TILE), lambda i, c: (0, 0, i))
    spec_kh = pl.BlockSpec((PEER_NKEYS, PEER_HEADS, DENSE_TILE), lambda i, c: (0, 0, i))
    col_spec = pl.BlockSpec((D_MODEL, DENSE_TILE), lambda i, c: (0, i))
    return pl.pallas_call(
        _peer_dense_kernel,
        grid=(tok.total // DENSE_TILE, PEER_EXPERTS // DENSE_EXPERTS),
        in_specs=[
            col_spec,
            pl.BlockSpec((DENSE_EXPERTS, D_MODEL), lambda i, c: (c, 0)),
            pl.BlockSpec((D_MODEL, DENSE_EXPERTS), lambda i, c: (0, c)),
            spec_hk, spec_hk, spec_kh, spec_kh,
        ],
        out_specs=col_spec,
        out_shape=jax.ShapeDtypeStruct((D_MODEL, tok.total), F32),
        compiler_params=_cparams("parallel", "arbitrary"),
        name="peer_dense",
    )(h_t, u_tab, vt_tab, r2, w2, n1, w1)


def kernel(x_prompt, x_sample, state_ssm_re, state_ssm_im, cache_k, cache_v, c, c_ctx, w_mod, b_mod, ln_g, ln_b, ssm_w_in, ssm_a_re, ssm_a_im, ssm_log_dt, ssm_b_re, ssm_b_im, ssm_c_re, ssm_c_im, ssm_d, ssm_w_glu, ssm_w_out, attn_w_qkv, attn_sink, attn_w_out, peer_w_q, peer_keys, peer_u, peer_v):
    depth = w_mod.shape[0]
    n_batch, seq, _ = x_prompt.shape
    dec_batch, dec_seq, _ = x_sample.shape
    tok = _Tokens(n_batch * seq, dec_batch, dec_seq)
    alpha = (2 * depth) ** 0.25

    cond = jnp.concatenate([c, c_ctx[None, :],
                            jnp.zeros((MOD_ROWS - dec_batch - 1, D_MODEL), F32)], axis=0)
    mods_all = _modulation_all(cond, w_mod, b_mod)

    x = jnp.concatenate([x_prompt.reshape(-1, D_MODEL), x_sample.reshape(-1, D_MODEL)], axis=0)
    st_re, st_im, st_k, st_v = [], [], [], []
    for i in range(depth):
        j = i // 2
        mods = mods_all[i]
        if i % 2 == 0:
            u = _mod_linear(tok, x, mods, ssm_w_in[j].astype(BF16), 0, "s5_in")
            ops = _s5_operators(ssm_a_re[j], ssm_a_im[j], ssm_log_dt[j], ssm_b_re[j], ssm_b_im[j],
                                ssm_c_re[j], ssm_c_im[j])
            zeros = jnp.zeros((SSM_GROUPS, n_batch, 4 * SSM_STATE), F32)
            y_c, s_c = _s5_core(_to_chunks(u[:tok.n_ctx], n_batch, seq), zeros, ops,
                                seq // SSM_CHUNK, n_batch, "s5_core_context")
            h0 = _state_to_cols(state_ssm_re[:, j], state_ssm_im[:, j])
            y_s, _ = _s5_core(_to_chunks(u[tok.n_ctx:], dec_batch, dec_seq), h0, ops,
                              dec_seq // SSM_CHUNK, dec_batch, "s5_core_latent")
            y = jnp.concatenate([_from_chunks(y_c, n_batch, seq), _from_chunks(y_s, dec_batch, dec_seq)], axis=0)
            s_re, s_im = _cols_to_state(s_c)
            st_re.append(s_re)
            st_im.append(s_im)
            x = _s5_tail(tok, y, u, x, mods, ssm_d[j], ssm_w_glu[j].astype(BF16), ssm_w_out[j].astype(BF16),
                         ln_g[i, 0], ln_b[i, 0], alpha)
        else:
            qkv = _mod_linear(tok, x, mods, _attn_weights(attn_w_qkv[j]), 0, "attn_qkv")
            o_c, k_c, v_c = _attn_context(qkv, attn_sink[j], n_batch, seq)
            past = cache_k.shape[2]
            o_s = _attn_latent(qkv, _rope_tables(dec_seq), cache_k[:, j].reshape(dec_batch, past, KV_DIM),
                               cache_v[:, j].reshape(dec_batch, past, KV_DIM), attn_sink[j], tok)
            st_k.append(k_c.reshape(n_batch, seq, N_KV_HEADS, HEAD_DIM))
            st_v.append(v_c.reshape(n_batch, seq, N_KV_HEADS, HEAD_DIM))
            x = _mm_postnorm(tok, jnp.concatenate([o_c, o_s], axis=0), x, mods, attn_w_out[j].astype(BF16),
                             ln_g[i, 0], ln_b[i, 0], 2, alpha, "attn_out")
        w_q, key_blocks = _route_weights(peer_w_q[i], peer_keys[i])
        q, h_t = _mod_linear_t(tok, x, mods, w_q, 3, "peer_query")
        route = _peer_route(q, key_blocks, tok.total)
        u_tab, vt_tab = _prepare_tables(peer_u[i], peer_v[i])
        o_t = _peer_dense(tok, h_t, u_tab, vt_tab, route)
        x = _postnorm(tok, o_t, x, mods, ln_g[i, 1], ln_b[i, 1], 5, alpha, "peer_norm")

    y_prompt = x[:tok.n_ctx].reshape(n_batch, seq, D_MODEL)
    y_sample = x[tok.n_ctx:].reshape(dec_batch, dec_seq, D_MODEL)
    return (y_prompt, y_sample, jnp.stack(st_re, axis=1), jnp.stack(st_im, axis=1),
            jnp.stack(st_k, axis=1), jnp.stack(st_v, axis=1))
```

```python
import functools
import math

import jax
import jax.numpy as jnp
from jax import lax
from jax.experimental import pallas as pl
from jax.experimental.pallas import tpu as pltpu

F32 = jnp.float32
BF16 = jnp.bfloat16

D_MODEL = 1024
GRID_W = 64
SSM_GROUP = 16
SSM_GROUPS = D_MODEL // SSM_GROUP
SSM_STATE = 64
SSM_CHUNK = 16
SSM_ROW = SSM_CHUNK * SSM_GROUP
HEAD_DIM = 64
N_HEADS = D_MODEL // HEAD_DIM
N_KV_HEADS = 4
KV_REP = N_HEADS // N_KV_HEADS
KV_DIM = N_KV_HEADS * HEAD_DIM
WINDOW = 128
ATTN_BLOCK = 128
ROPE_BASE = 10000.0
ROPE_AXIS_DIM = HEAD_DIM // 2
NEG_INF = -1e30
PEER_HEADS = 8
PEER_NKEYS = 128
PEER_EXPERTS = PEER_NKEYS * PEER_NKEYS
PEER_TOPK = 16
PEER_KEY_DIM = 256
PEER_HALF = PEER_KEY_DIM // 2
LN_EPS = 1e-5

TOKEN_TILE = 512
DENSE_TILE = 512
DENSE_EXPERTS = 2048
TABLE_BLOCK = 1024
SUB_EXPERTS = 512
MOD_ROWS = 16
BF16_SUBLANES = 16
VMEM_LIMIT = 56 * 1024 * 1024

_NT = (((1,), (1,)), ((), ()))


def _cparams(*sem):
    return pltpu.CompilerParams(dimension_semantics=sem, vmem_limit_bytes=VMEM_LIMIT)


def _gelu(x):
    return 0.5 * x * (1.0 + jnp.tanh(0.7978845608028654 * (x + 0.044715 * (x * x * x))))


def _layer_norm_rows(z, g, b):
    mu = jnp.mean(z, axis=-1, keepdims=True)
    zc = z - mu
    var = jnp.mean(zc * zc, axis=-1, keepdims=True)
    return zc * lax.rsqrt(var + LN_EPS) * g + b


def _mod_kernel(c_ref, w_ref, b_ref, o_ref):
    c = c_ref[...]
    s = c * jax.nn.sigmoid(c)
    o_ref[...] = jnp.dot(s.astype(BF16), w_ref[...].astype(BF16),
                         preferred_element_type=F32) + b_ref[...]


def _modulation_all(cond, w_mod, b_mod):
    depth = w_mod.shape[0]
    n_out = w_mod.shape[2]
    nb = n_out // D_MODEL
    out = pl.pallas_call(
        _mod_kernel,
        grid=(depth, nb),
        in_specs=[
            pl.BlockSpec((MOD_ROWS, D_MODEL), lambda i, j: (0, 0)),
            pl.BlockSpec((None, D_MODEL, D_MODEL), lambda i, j: (i, 0, j)),
            pl.BlockSpec((None, 1, D_MODEL), lambda i, j: (i, 0, j)),
        ],
        out_specs=pl.BlockSpec((None, MOD_ROWS, D_MODEL), lambda i, j: (i, 0, j)),
        out_shape=jax.ShapeDtypeStruct((depth, MOD_ROWS, n_out), F32),
        compiler_params=_cparams("arbitrary", "arbitrary"),
        name="modulation",
    )(cond, w_mod, b_mod.reshape(depth, 1, n_out))
    return out.reshape(depth, MOD_ROWS, nb, D_MODEL)


class _Tokens:
    def __init__(self, n_ctx_tokens, dec_batch, dec_seq):
        self.n_ctx = n_ctx_tokens
        self.dec_batch = dec_batch
        self.dec_seq = dec_seq
        self.total = n_ctx_tokens + dec_batch * dec_seq
        assert n_ctx_tokens % TOKEN_TILE == 0 and dec_seq % TOKEN_TILE == 0
        self.ctx_tiles = n_ctx_tokens // TOKEN_TILE
        self.tiles = self.total // TOKEN_TILE
        self.tiles_per_seq = dec_seq // TOKEN_TILE

    def mod_row(self, i):
        return jnp.where(i < self.ctx_tiles, self.dec_batch, (i - self.ctx_tiles) // self.tiles_per_seq)


def _mod_spec(tok):
    return pl.BlockSpec((1, 6, D_MODEL), lambda i: (tok.mod_row(i), 0, 0))


def _row_spec(width):
    return pl.BlockSpec((TOKEN_TILE, width), lambda i: (i, 0))


def _full_spec(shape):
    nd = len(shape)
    return pl.BlockSpec(shape, lambda i: (0,) * nd)


def _linear_kernel(x_ref, m_ref, w_ref, o_ref, *, shift_idx):
    m = m_ref[0]
    h = x_ref[...] * (1.0 + m[shift_idx + 1:shift_idx + 2]) + m[shift_idx:shift_idx + 1]
    o_ref[...] = jnp.dot(h.astype(BF16), w_ref[...], preferred_element_type=F32)


def _mod_linear(tok, x, mods, w, shift_idx, name):
    n = w.shape[1]
    return pl.pallas_call(
        functools.partial(_linear_kernel, shift_idx=shift_idx),
        grid=(tok.tiles,),
        in_specs=[_row_spec(D_MODEL), _mod_spec(tok), _full_spec((D_MODEL, n))],
        out_specs=_row_spec(n),
        out_shape=jax.ShapeDtypeStruct((tok.total, n), F32),
        compiler_params=_cparams("parallel"),
        name=name,
    )(x, mods, w)


def _col_spec():
    return pl.BlockSpec((D_MODEL, TOKEN_TILE), lambda i: (0, i))


def _linear_t_kernel(x_ref, m_ref, w_ref, o_ref, ht_ref, *, shift_idx):
    m = m_ref[0]
    h = x_ref[...] * (1.0 + m[shift_idx + 1:shift_idx + 2]) + m[shift_idx:shift_idx + 1]
    o_ref[...] = jnp.dot(h.astype(BF16), w_ref[...], preferred_element_type=F32)
    ht_ref[...] = h.T.astype(BF16)


def _mod_linear_t(tok, x, mods, w, shift_idx, name):
    n = w.shape[1]
    return pl.pallas_call(
        functools.partial(_linear_t_kernel, shift_idx=shift_idx),
        grid=(tok.tiles,),
        in_specs=[_row_spec(D_MODEL), _mod_spec(tok), _full_spec((D_MODEL, n))],
        out_specs=[_row_spec(n), _col_spec()],
        out_shape=[jax.ShapeDtypeStruct((tok.total, n), F32),
                   jax.ShapeDtypeStruct((D_MODEL, tok.total), BF16)],
        compiler_params=_cparams("parallel"),
        name=name,
    )(x, mods, w)


def _post_norm_rows(x, out, m, gate_idx, g_ref, b_ref, alpha):
    z = alpha * x + (1.0 + m[gate_idx:gate_idx + 1]) * out
    return _layer_norm_rows(z, g_ref[...], b_ref[...])


def _mm_postnorm_kernel(a_ref, x_ref, m_ref, w_ref, g_ref, b_ref, o_ref, *, gate_idx, alpha):
    y = jnp.dot(a_ref[...].astype(BF16), w_ref[...], preferred_element_type=F32)
    o_ref[...] = _post_norm_rows(x_ref[...], y, m_ref[0], gate_idx, g_ref, b_ref, alpha)


def _mm_postnorm(tok, a, x, mods, w, g, b, gate_idx, alpha, name):
    return pl.pallas_call(
        functools.partial(_mm_postnorm_kernel, gate_idx=gate_idx, alpha=alpha),
        grid=(tok.tiles,),
        in_specs=[_row_spec(D_MODEL), _row_spec(D_MODEL), _mod_spec(tok),
                  _full_spec((D_MODEL, D_MODEL)), _full_spec((1, D_MODEL)), _full_spec((1, D_MODEL))],
        out_specs=_row_spec(D_MODEL),
        out_shape=jax.ShapeDtypeStruct((tok.total, D_MODEL), F32),
        compiler_params=_cparams("parallel"),
        name=name,
    )(a, x, mods, w, g.reshape(1, D_MODEL), b.reshape(1, D_MODEL))


def _postnorm_kernel(ot_ref, x_ref, m_ref, g_ref, b_ref, o_ref, *, gate_idx, alpha):
    o_ref[...] = _post_norm_rows(x_ref[...], ot_ref[...].T, m_ref[0], gate_idx, g_ref, b_ref, alpha)


def _postnorm(tok, out_t, x, mods, g, b, gate_idx, alpha, name):
    return pl.pallas_call(
        functools.partial(_postnorm_kernel, gate_idx=gate_idx, alpha=alpha),
        grid=(tok.tiles,),
        in_specs=[_col_spec(), _row_spec(D_MODEL), _mod_spec(tok),
                  _full_spec((1, D_MODEL)), _full_spec((1, D_MODEL))],
        out_specs=_row_spec(D_MODEL),
        out_shape=jax.ShapeDtypeStruct((tok.total, D_MODEL), F32),
        compiler_params=_cparams("parallel"),
        name=name,
    )(out_t, x, mods, g.reshape(1, D_MODEL), b.reshape(1, D_MODEL))


def _s5_operators(a_re, a_im, log_dt, b_re, b_im, c_re, c_im):
    hi = lax.Precision.HIGHEST
    dt = jnp.exp(log_dt)[..., None]
    ar, ai = a_re * dt, a_im * dt
    k = jnp.arange(SSM_CHUNK + 1, dtype=F32)[:, None, None, None]
    mag = jnp.exp(k * ar[None])
    pw_re, pw_im = mag * jnp.cos(k * ai[None]), mag * jnp.sin(k * ai[None])
    l_re, l_im = pw_re[1], pw_im[1]
    den = a_re * a_re + a_im * a_im
    q_re = ((l_re - 1.0) * a_re + l_im * a_im) / den
    q_im = (l_im * a_re - (l_re - 1.0) * a_im) / den
    bt_re = q_re[..., None] * b_re - q_im[..., None] * b_im
    bt_im = q_re[..., None] * b_im + q_im[..., None] * b_re
    d_re = pw_re[..., None] * bt_re[None] - pw_im[..., None] * bt_im[None]
    d_im = pw_re[..., None] * bt_im[None] + pw_im[..., None] * bt_re[None]
    pw_re_c, pw_im_c = pw_re[:, :, :, None, :], pw_im[:, :, :, None, :]
    e_re = c_re[None] * pw_re_c - c_im[None] * pw_im_c
    e_im = c_re[None] * pw_im_c + c_im[None] * pw_re_c
    kern = (jnp.einsum('dgip,kdgpj->kdgij', c_re, d_re, precision=hi)
            - jnp.einsum('dgip,kdgpj->kdgij', c_im, d_im, precision=hi))
    s = jnp.arange(SSM_CHUNK)
    lag = s[None, :] - s[:, None]
    kf = kern[jnp.clip(lag, 0, SSM_CHUNK), 0]
    kb = kern[jnp.clip(-lag, 0, SSM_CHUNK), 1]
    mix = (jnp.where((lag >= 0)[:, :, None, None, None], kf, 0.0)
           + jnp.where((lag <= 0)[:, :, None, None, None], kb, 0.0))
    g = SSM_GROUPS
    m_op = jnp.transpose(mix, (2, 0, 4, 1, 3)).reshape(g, SSM_ROW, SSM_ROW)
    pf = (SSM_CHUNK - 1) - s
    win = jnp.stack([d_re[pf, 0], d_re[s, 1], d_im[pf, 0], d_im[s, 1]], axis=0)
    win_op = jnp.transpose(win, (2, 1, 4, 0, 3)).reshape(g, SSM_ROW, 4 * SSM_STATE)
    wout = jnp.stack([e_re[s + 1, 0], e_re[SSM_CHUNK - s, 1], -e_im[s + 1, 0], -e_im[SSM_CHUNK - s, 1]],
                     axis=0)
    wout_op = jnp.transpose(wout, (2, 0, 4, 1, 3)).reshape(g, 4 * SSM_STATE, SSM_ROW)
    lam = jnp.stack([jnp.concatenate([pw_re[SSM_CHUNK, 0], pw_re[SSM_CHUNK, 1]], axis=-1),
                     jnp.concatenate([pw_im[SSM_CHUNK, 0], pw_im[SSM_CHUNK, 1]], axis=-1)], axis=1)
    return m_op, win_op, wout_op, lam


def _s5_core_kernel(u_ref, h0_ref, m_ref, win_ref, wout_ref, lam_ref, y_ref, s_ref, z_scr, p_scr,
                    *, n_chunks, batch):
    hi = lax.Precision.HIGHEST
    p2 = 2 * SSM_STATE
    u = u_ref[...]
    z_scr[...] = jnp.dot(u, win_ref[...], preferred_element_type=F32, precision=hi)
    l_re = lam_ref[0:1, :]
    l_im = lam_ref[1:2, :]
    is_fwd = lax.broadcasted_iota(jnp.int32, (batch, p2), 1) < SSM_STATE

    def step(k, carry):
        s_re, s_im = carry
        rf = pl.multiple_of(k * batch, batch)
        rb = pl.multiple_of((n_chunks - 1 - k) * batch, batch)
        zf = z_scr[pl.ds(rf, batch), :]
        zb = z_scr[pl.ds(rb, batch), :]
        z_re = jnp.where(is_fwd, zf[:, :p2], zb[:, :p2])
        z_im = jnp.where(is_fwd, zf[:, p2:], zb[:, p2:])
        p_scr[pl.ds(rf, batch), 0:SSM_STATE] = s_re[:, :SSM_STATE]
        p_scr[pl.ds(rf, batch), p2:p2 + SSM_STATE] = s_im[:, :SSM_STATE]
        p_scr[pl.ds(rb, batch), SSM_STATE:p2] = s_re[:, SSM_STATE:]
        p_scr[pl.ds(rb, batch), p2 + SSM_STATE:] = s_im[:, SSM_STATE:]
        n_re = l_re * s_re - l_im * s_im + z_re
        n_im = l_re * s_im + l_im * s_re + z_im
        return n_re, n_im

    h0 = h0_ref[...]
    s_re, s_im = lax.fori_loop(0, n_chunks, step, (h0[:, :p2], h0[:, p2:]))
    s_ref[:, :p2] = s_re
    s_ref[:, p2:] = s_im
    y_ref[...] = (jnp.dot(u.astype(BF16), m_ref[...].astype(BF16), preferred_element_type=F32)
                  + jnp.dot(p_scr[...].astype(BF16), wout_ref[...].astype(BF16), preferred_element_type=F32))


def _s5_core(ug, h0, ops, n_chunks, batch, name):
    m_op, win_op, wout_op, lam = ops
    g, rows, _ = ug.shape
    st = 4 * SSM_STATE
    return pl.pallas_call(
        functools.partial(_s5_core_kernel, n_chunks=n_chunks, batch=batch),
        grid=(g,),
        in_specs=[
            pl.BlockSpec((None, rows, SSM_ROW), lambda i: (i, 0, 0)),
            pl.BlockSpec((None, batch, st), lambda i: (i, 0, 0)),
            pl.BlockSpec((None, SSM_ROW, SSM_ROW), lambda i: (i, 0, 0)),
            pl.BlockSpec((None, SSM_ROW, st), lambda i: (i, 0, 0)),
            pl.BlockSpec((None, st, SSM_ROW), lambda i: (i, 0, 0)),
            pl.BlockSpec((None, 2, 2 * SSM_STATE), lambda i: (i, 0, 0)),
        ],
        out_specs=[
            pl.BlockSpec((None, rows, SSM_ROW), lambda i: (i, 0, 0)),
            pl.BlockSpec((None, batch, st), lambda i: (i, 0, 0)),
        ],
        out_shape=[jax.ShapeDtypeStruct((g, rows, SSM_ROW), F32),
                   jax.ShapeDtypeStruct((g, batch, st), F32)],
        scratch_shapes=[pltpu.VMEM((rows, st), F32), pltpu.VMEM((rows, st), F32)],
        compiler_params=_cparams("parallel"),
        name=name,
    )(ug, h0, m_op, win_op, wout_op, lam)


def _to_chunks(u, batch, seq):
    n_chunks = seq // SSM_CHUNK
    t = u.reshape(batch, n_chunks, SSM_CHUNK, SSM_GROUPS, SSM_GROUP)
    return jnp.transpose(t, (3, 1, 0, 2, 4)).reshape(SSM_GROUPS, n_chunks * batch, SSM_ROW)


def _from_chunks(y, batch, seq):
    n_chunks = seq // SSM_CHUNK
    t = y.reshape(SSM_GROUPS, n_chunks, batch, SSM_CHUNK, SSM_GROUP)
    return jnp.transpose(t, (2, 1, 3, 0, 4)).reshape(batch * seq, D_MODEL)


def _state_to_cols(h_re, h_im):
    cols = jnp.concatenate([h_re[:, 0], h_re[:, 1], h_im[:, 0], h_im[:, 1]], axis=-1)
    return jnp.transpose(cols, (1, 0, 2))


def _cols_to_state(s):
    t = jnp.transpose(s, (1, 0, 2)).reshape(s.shape[1], SSM_GROUPS, 2, 2, SSM_STATE)
    t = jnp.transpose(t, (2, 0, 3, 1, 4))
    return t[0], t[1]


def _s5_tail_kernel(y_ref, u_ref, x_ref, m_ref, d_ref, wg_ref, wo_ref, g_ref, b_ref, o_ref, *, alpha):
    v = _gelu(y_ref[...] + d_ref[...] * u_ref[...])
    gl = jnp.dot(v.astype(BF16), wg_ref[...], preferred_element_type=F32)
    a = gl[:, :D_MODEL] * jax.nn.sigmoid(gl[:, D_MODEL:])
    out = jnp.dot(a.astype(BF16), wo_ref[...], preferred_element_type=F32)
    o_ref[...] = _post_norm_rows(x_ref[...], out, m_ref[0], 2, g_ref, b_ref, alpha)


def _s5_tail(tok, y, u, x, mods, d, w_glu, w_out, g, b, alpha):
    return pl.pallas_call(
        functools.partial(_s5_tail_kernel, alpha=alpha),
        grid=(tok.tiles,),
        in_specs=[_row_spec(D_MODEL), _row_spec(D_MODEL), _row_spec(D_MODEL), _mod_spec(tok),
                  _full_spec((1, D_MODEL)), _full_spec((D_MODEL, 2 * D_MODEL)),
                  _full_spec((D_MODEL, D_MODEL)), _full_spec((1, D_MODEL)), _full_spec((1, D_MODEL))],
        out_specs=_row_spec(D_MODEL),
        out_shape=jax.ShapeDtypeStruct((tok.total, D_MODEL), F32),
        compiler_params=_cparams("parallel"),
        name="s5_tail",
    )(y, u, x, mods, d.reshape(1, D_MODEL), w_glu, w_out, g.reshape(1, D_MODEL), b.reshape(1, D_MODEL))


QKV_COLS = 3 * D_MODEL


def _attn_weights(w_qkv):
    wq, wk, wv = w_qkv[:, :D_MODEL], w_qkv[:, D_MODEL:D_MODEL + KV_DIM], w_qkv[:, D_MODEL + KV_DIM:]
    half = ROPE_AXIS_DIM // 2

    def partner(w):
        n = w.shape[1]
        t = w.reshape(D_MODEL, n // ROPE_AXIS_DIM, 2, half)
        return jnp.stack([-t[:, :, 1], t[:, :, 0]], axis=2).reshape(D_MODEL, n)

    pad = jnp.zeros((D_MODEL, QKV_COLS - 2 * D_MODEL - 3 * KV_DIM), w_qkv.dtype)
    return jnp.concatenate([wq, partner(wq), wk, wv, partner(wk), pad], axis=1).astype(BF16)


def _rope_tables(seq):
    half = ROPE_AXIS_DIM // 2
    rows = seq // GRID_W
    row = jnp.repeat(jnp.arange(rows, dtype=F32), GRID_W)
    col = jnp.tile(jnp.arange(GRID_W, dtype=F32), rows)
    inv = ROPE_BASE ** (-jnp.arange(half, dtype=F32) / half)
    ang_r = row[:, None] * inv[None, :]
    ang_c = col[:, None] * inv[None, :]
    ang = jnp.concatenate([ang_r, ang_r, ang_c, ang_c], axis=-1)
    cos = jnp.tile(jnp.cos(ang), (1, N_KV_HEADS))
    sin = jnp.tile(jnp.sin(ang), (1, N_KV_HEADS))
    return jnp.concatenate([cos, sin], axis=-1)


def _softmax_heads(q, k_list, v_list, mask_list, sink_ref, o_ref, heads_per_pass):
    scale = HEAD_DIM ** -0.5
    tq = q.shape[0]
    masks = [None if m is None else jnp.concatenate([m] * heads_per_pass, axis=0) for m in mask_list]
    for first in range(0, N_HEADS, heads_per_pass):
        heads = range(first, first + heads_per_pass)
        g = first // KV_REP
        qg = jnp.concatenate([q[:, h * HEAD_DIM:(h + 1) * HEAD_DIM] for h in heads], axis=0).astype(BF16)
        sink = jnp.concatenate([jnp.full((tq, 1), sink_ref[h], F32) for h in heads], axis=0)
        cols = slice(g * HEAD_DIM, (g + 1) * HEAD_DIM)
        logits = []
        for k, mask in zip(k_list, masks):
            s = lax.dot_general(qg, k[:, cols], _NT, preferred_element_type=F32) * scale
            if mask is not None:
                s = jnp.where(mask, s, NEG_INF)
            logits.append(s)
        m = sink
        for s in logits:
            m = jnp.maximum(m, jnp.max(s, axis=-1, keepdims=True))
        denom = jnp.exp(sink - m)
        acc = jnp.zeros((heads_per_pass * tq, HEAD_DIM), F32)
        for s, v in zip(logits, v_list):
            p = jnp.exp(s - m)
            denom = denom + jnp.sum(p, axis=-1, keepdims=True)
            acc = acc + jnp.dot(p.astype(BF16), v[:, cols], preferred_element_type=F32)
        out = acc / denom
        for r, h in enumerate(heads):
            o_ref[:, h * HEAD_DIM:(h + 1) * HEAD_DIM] = out[r * tq:(r + 1) * tq]


def _attn_ctx_kernel(sink_ref, q_ref, kv_ref, o_ref, k_ref, v_ref):
    kv = kv_ref[...]
    k_ref[...] = kv[:, :KV_DIM]
    v_ref[...] = kv[:, KV_DIM:2 * KV_DIM]
    k = kv[:, :KV_DIM].astype(BF16)
    v = kv[:, KV_DIM:2 * KV_DIM].astype(BF16)
    _softmax_heads(q_ref[...], [k], [v], [None], sink_ref, o_ref, 1)


def _attn_context(qkv, sink, n_batch, seq):
    kv_spec = pl.BlockSpec((seq, KV_DIM), lambda b: (b, 0))
    kv_shape = jax.ShapeDtypeStruct((n_batch * seq, KV_DIM), F32)
    return pl.pallas_call(
        _attn_ctx_kernel,
        grid=(n_batch,),
        in_specs=[
            pl.BlockSpec(memory_space=pltpu.SMEM),
            pl.BlockSpec((seq, D_MODEL), lambda b: (b, 0)),
            pl.BlockSpec((seq, D_MODEL), lambda b: (b, 2)),
        ],
        out_specs=[pl.BlockSpec((seq, D_MODEL), lambda b: (b, 0)), kv_spec, kv_spec],
        out_shape=[jax.ShapeDtypeStruct((n_batch * seq, D_MODEL), F32), kv_shape, kv_shape],
        compiler_params=_cparams("parallel"),
        name="attn_context",
    )(sink, qkv, qkv)


def _attn_lat_kernel(sink_ref, q_ref, qp_ref, kv0_ref, kv1_ref, kv2_ref, t0_ref, t1_ref, t2_ref,
                     kc_ref, vc_ref, o_ref, *, n_qblocks):
    qb = pl.program_id(1)
    t1 = t1_ref[...]
    cos_q = jnp.concatenate([t1[:, :KV_DIM]] * KV_REP, axis=-1)
    sin_q = jnp.concatenate([t1[:, KV_DIM:]] * KV_REP, axis=-1)
    q = q_ref[...] * cos_q + qp_ref[...] * sin_q

    def rope_k(kv_ref, t_ref):
        kv = kv_ref[...]
        t = t_ref[...]
        k = kv[:, :KV_DIM] * t[:, :KV_DIM] + kv[:, 2 * KV_DIM:3 * KV_DIM] * t[:, KV_DIM:]
        return k.astype(BF16), kv[:, KV_DIM:2 * KV_DIM].astype(BF16)

    k0, v0 = rope_k(kv0_ref, t0_ref)
    k1, v1 = rope_k(kv1_ref, t1_ref)
    k2, v2 = rope_k(kv2_ref, t2_ref)
    k_loc = jnp.concatenate([k0, k1, k2], axis=0)
    v_loc = jnp.concatenate([v0, v1, v2], axis=0)
    span = 3 * ATTN_BLOCK
    qi = lax.broadcasted_iota(jnp.int32, (ATTN_BLOCK, span), 0)
    kj = lax.broadcasted_iota(jnp.int32, (ATTN_BLOCK, span), 1) - ATTN_BLOCK
    ok = (jnp.abs(kj - qi) <= WINDOW)
    ok = ok & ((kj >= 0) | (qb > 0)) & ((kj < ATTN_BLOCK) | (qb < n_qblocks - 1))
    kc = kc_ref[0].astype(BF16)
    vc = vc_ref[0].astype(BF16)
    _softmax_heads(q, [kc, k_loc], [vc, v_loc], [None, ok], sink_ref, o_ref, KV_REP)


def _attn_latent(qkv, tables, k_ctx, v_ctx, sink, tok):
    nqb = tok.dec_seq // ATTN_BLOCK
    base = tok.n_ctx // ATTN_BLOCK
    past = k_ctx.shape[1]

    def rows(b, i, off):
        return base + b * nqb + jnp.clip(i + off, 0, nqb - 1)

    def kv_spec(off):
        return pl.BlockSpec((ATTN_BLOCK, D_MODEL), lambda b, i: (rows(b, i, off), 2))

    def tab_spec(off):
        return pl.BlockSpec((ATTN_BLOCK, 2 * KV_DIM), lambda b, i: (jnp.clip(i + off, 0, nqb - 1), 0))

    return pl.pallas_call(
        functools.partial(_attn_lat_kernel, n_qblocks=nqb),
        grid=(tok.dec_batch, nqb),
        in_specs=[
            pl.BlockSpec(memory_space=pltpu.SMEM),
            pl.BlockSpec((ATTN_BLOCK, D_MODEL), lambda b, i: (rows(b, i, 0), 0)),
            pl.BlockSpec((ATTN_BLOCK, D_MODEL), lambda b, i: (rows(b, i, 0), 1)),
            kv_spec(-1), kv_spec(0), kv_spec(1),
            tab_spec(-1), tab_spec(0), tab_spec(1),
            pl.BlockSpec((1, past, KV_DIM), lambda b, i: (b, 0, 0)),
            pl.BlockSpec((1, past, KV_DIM), lambda b, i: (b, 0, 0)),
        ],
        out_specs=pl.BlockSpec((ATTN_BLOCK, D_MODEL), lambda b, i: (b * nqb + i, 0)),
        out_shape=jax.ShapeDtypeStruct((tok.dec_batch * tok.dec_seq, D_MODEL), F32),
        compiler_params=_cparams("parallel", "arbitrary"),
        name="attn_latent",
    )(sink, qkv, qkv, qkv, qkv, qkv, tables, tables, tables, k_ctx, v_ctx)


ROUTE_TILE = 256
ROUTE_LANES = 128
NKH = PEER_NKEYS * PEER_HEADS


def _oddeven_merge_sort_pairs(n):
    pairs = []

    def merge(lo, hi, r):
        step = r * 2
        if step < hi - lo:
            merge(lo, hi, step)
            merge(lo + r, hi, step)
            pairs.extend((i, i + r) for i in range(lo + r, hi - r, step))
        else:
            pairs.append((lo, lo + r))

    def sort(lo, hi):
        if hi - lo >= 1:
            mid = lo + (hi - lo) // 2
            sort(lo, mid)
            sort(mid + 1, hi)
            merge(lo, hi, 1)

    sort(0, n - 1)
    return pairs


_SORT16 = _oddeven_merge_sort_pairs(PEER_TOPK)


def _exchange(x, i, j):
    x[i], x[j] = jnp.maximum(x[i], x[j]), jnp.minimum(x[i], x[j])


def _top16_sorted(vals):
    groups = []
    for g in range(0, len(vals), PEER_TOPK):
        x = list(vals[g:g + PEER_TOPK])
        for i, j in _SORT16:
            _exchange(x, i, j)
        groups.append(x)
    while len(groups) > 1:
        merged = []
        for a, b in zip(groups[0::2], groups[1::2]):
            x = [jnp.maximum(a[i], b[PEER_TOPK - 1 - i]) for i in range(PEER_TOPK)]
            for d in (8, 4, 2, 1):
                for i in range(PEER_TOPK):
                    if not i & d:
                        _exchange(x, i, i + d)
            merged.append(x)
        groups = merged
    return groups[0]


def _rank_bits(s, v):
    b3 = v[7] > s
    b2 = jnp.where(b3, v[11], v[3]) > s
    b1 = jnp.where(b3, jnp.where(b2, v[13], v[9]), jnp.where(b2, v[5], v[1])) > s
    hi = jnp.where(b2, jnp.where(b1, v[14], v[12]), jnp.where(b1, v[10], v[8]))
    lo = jnp.where(b2, jnp.where(b1, v[6], v[4]), jnp.where(b1, v[2], v[0]))
    b0 = jnp.where(b3, hi, lo) > s
    return (b3, b2, b1, b0), v[15] > s


def _select16(bits, leaves):
    b3, b2, b1, b0 = bits
    lvl = [jnp.where(b0, leaves[2 * i + 1], leaves[2 * i]) for i in range(8)]
    lvl = [jnp.where(b1, lvl[2 * i + 1], lvl[2 * i]) for i in range(4)]
    lvl = [jnp.where(b2, lvl[2 * i + 1], lvl[2 * i]) for i in range(2)]
    return jnp.where(b3, lvl[1], lvl[0])


def _exact_top16(s_ref, rank_ref, top_ref):
    n = s_ref.shape[0]
    rank_ref[...] = jnp.full(rank_ref.shape, float(PEER_TOPK), F32)

    def body(k, carry):
        s = s_ref[...]
        iota = lax.broadcasted_iota(jnp.int32, s.shape, 0).astype(F32)
        m = jnp.max(s, axis=0)
        idx = jnp.min(jnp.where(s == m[None], iota, float(n)), axis=0)
        hit = iota == idx[None]
        rank_ref[...] = jnp.where(hit, lax.convert_element_type(k, F32), rank_ref[...])
        s_ref[...] = jnp.where(hit, -jnp.inf, s)
        top_ref[k] = m
        return carry

    lax.fori_loop(0, PEER_TOPK, body, 0)


def _peer_route_kernel(q_ref, k1_ref, k2_ref, r2_ref, w2_ref, n1_ref, w1_ref,
                       r2_st, w2_st, n1_st, w1_st, sa_scr, ra_scr, sb_scr, rb_scr, ta_scr, tb_scr):
    q = q_ref[...].astype(BF16)
    half = PEER_HEADS * PEER_HALF
    s1_all = lax.dot_general(k1_ref[...], q[:, :half], _NT, preferred_element_type=F32)
    s2_all = lax.dot_general(k2_ref[...], q[:, half:], _NT, preferred_element_type=F32)
    slab = (PEER_HEADS, ROUTE_LANES)
    ninf = jnp.full(slab, -jnp.inf, F32)

    for j in range(ROUTE_TILE // ROUTE_LANES):
        lanes = slice(j * ROUTE_LANES, (j + 1) * ROUTE_LANES)
        s1 = [s1_all[k * PEER_HEADS:(k + 1) * PEER_HEADS, lanes] for k in range(PEER_NKEYS)]
        s2 = [s2_all[k * PEER_HEADS:(k + 1) * PEER_HEADS, lanes] for k in range(PEER_NKEYS)]
        v1 = _top16_sorted(s1)
        v2 = _top16_sorted(s2)
        cand = [[v1[a] + v2[b] for b in range(PEER_TOPK // (a + 1))] for a in range(PEER_TOPK)]
        flat = [c for row in cand for c in row]
        flat += [ninf] * (-len(flat) % PEER_TOPK)
        top = _top16_sorted(flat)
        theta = top[PEER_TOPK - 1]
        nb = []
        for row in cand:
            cnt = jnp.zeros(slab, F32)
            for c in row:
                cnt = cnt + jnp.where(c >= theta, 1.0, 0.0)
            nb.append(cnt)
        z = jnp.zeros(slab, F32)
        for k in range(PEER_TOPK):
            z = z + jnp.exp(top[k] - top[0])
        inv_z = 1.0 / z
        total = nb[0]
        for a in range(1, PEER_TOPK):
            total = total + nb[a]
        tied = total != float(PEER_TOPK)
        for k in range(PEER_TOPK - 1):
            tied = tied | ((v1[k] == v1[k + 1]) & (nb[k] != nb[k + 1])) | ((v2[k] == v2[k + 1]) & (nb[0] > float(k)))
        in1 = jnp.zeros(slab, F32)
        in2 = jnp.zeros(slab, F32)
        for k in range(PEER_NKEYS):
            rows = slice(k * PEER_HEADS, (k + 1) * PEER_HEADS)
            bits, out1 = _rank_bits(s1[k], v1)
            n1_st[rows, :] = jnp.where(out1, 0.0, _select16(bits, nb))
            w1_st[rows, :] = jnp.exp(s1[k] - v1[0]) * inv_z
            in1 = in1 + jnp.where(out1, 0.0, 1.0)
            (b3, b2, b1, b0), out2 = _rank_bits(s2[k], v2)
            r = (jnp.where(b3, 8.0, 0.0) + jnp.where(b2, 4.0, 0.0)) + (jnp.where(b1, 2.0, 0.0) + jnp.where(b0, 1.0, 0.0))
            r2_st[rows, :] = jnp.where(out2, float(PEER_TOPK), r)
            w2_st[rows, :] = jnp.exp(s2[k] - v2[0])
            in2 = in2 + jnp.where(out2, 0.0, 1.0)
        tied = tied | ((in1 != float(PEER_TOPK)) & (nb[PEER_TOPK - 1] != 0.0))
        tied = tied | ((in2 != float(PEER_TOPK)) & (nb[0] == float(PEER_TOPK)))
        n_tied = jnp.sum(jnp.where(tied, 1.0, 0.0))

        @pl.when(n_tied > 0.0)
        def _():
            shape3 = (PEER_NKEYS, PEER_HEADS, ROUTE_LANES)
            sa_scr[...] = s1_all[:, lanes].reshape(shape3)
            _exact_top16(sa_scr, ra_scr, ta_scr)
            sb_scr[...] = s2_all[:, lanes].reshape(shape3)
            _exact_top16(sb_scr, rb_scr, tb_scr)
            r1 = ra_scr[...]
            r2_st[...] = rb_scr[...].reshape(NKH, ROUTE_LANES)
            e1 = jnp.exp(s1_all[:, lanes].reshape(shape3) - ta_scr[0][None])
            w2_st[...] = jnp.exp(s2_all[:, lanes] - jnp.tile(tb_scr[0], (PEER_NKEYS, 1)))
            n_cand = len(flat)
            pos = 0
            for a in range(PEER_TOPK):
                width = PEER_TOPK // (a + 1)
                sa_scr[pos:pos + width] = ta_scr[a][None] + tb_scr[0:width]
                pos += width
            sa_scr[pos:n_cand] = jnp.full((n_cand - pos,) + slab, -jnp.inf, F32)
            _exact_top16(sa_scr.at[0:n_cand], sb_scr.at[0:n_cand], ta_scr)
            picked = jnp.where(sb_scr[0:n_cand] < float(PEER_TOPK), 1.0, 0.0)
            zz = jnp.zeros(slab, F32)
            for k in range(PEER_TOPK):
                zz = zz + jnp.exp(ta_scr[k] - ta_scr[0])
            n1 = jnp.zeros(shape3, F32)
            pos = 0
            for a in range(PEER_TOPK):
                width = PEER_TOPK // (a + 1)
                cnt = jnp.sum(picked[pos:pos + width], axis=0)
                n1 = jnp.where(r1 == float(a), cnt[None], n1)
                pos += width
            n1_st[...] = n1.reshape(NKH, ROUTE_LANES)
            w1_st[...] = (e1 / zz[None]).reshape(NKH, ROUTE_LANES)

        n1_ref[:, :, lanes] = n1_st[...].reshape(PEER_NKEYS, PEER_HEADS, ROUTE_LANES)
        w1_ref[:, :, lanes] = w1_st[...].reshape(PEER_NKEYS, PEER_HEADS, ROUTE_LANES)
        for h in range(PEER_HEADS):
            r2_ref[h, :, lanes] = r2_st[pl.ds(h, PEER_NKEYS, stride=PEER_HEADS), :].astype(BF16)
            w2_ref[h, :, lanes] = w2_st[pl.ds(h, PEER_NKEYS, stride=PEER_HEADS), :].astype(BF16)


def _route_weights(w_q, keys):
    w = w_q.reshape(D_MODEL, PEER_HEADS, 2, PEER_HALF)
    w = jnp.transpose(w, (0, 2, 1, 3)).reshape(D_MODEL, PEER_HEADS * PEER_KEY_DIM)
    eye = jnp.eye(PEER_HEADS, dtype=keys.dtype)
    big = jnp.einsum('phkd,hg->pkhgd', keys, eye).reshape(2, NKH, PEER_HEADS * PEER_HALF)
    return w.astype(BF16), big.astype(BF16)


def _peer_route(q, key_blocks, n_tokens):
    half = PEER_HEADS * PEER_HALF
    shape_kh = jax.ShapeDtypeStruct((PEER_NKEYS, PEER_HEADS, n_tokens), F32)
    shape_hk = jax.ShapeDtypeStruct((PEER_HEADS, PEER_NKEYS, n_tokens), BF16)
    spec_kh = pl.BlockSpec((PEER_NKEYS, PEER_HEADS, ROUTE_TILE), lambda i: (0, 0, i))
    spec_hk = pl.BlockSpec((PEER_HEADS, PEER_NKEYS, ROUTE_TILE), lambda i: (0, 0, i))
    stage = pltpu.VMEM((NKH, ROUTE_LANES), F32)
    cand3 = pltpu.VMEM((PEER_NKEYS, PEER_HEADS, ROUTE_LANES), F32)
    top3 = pltpu.VMEM((PEER_TOPK, PEER_HEADS, ROUTE_LANES), F32)
    return pl.pallas_call(
        _peer_route_kernel,
        grid=(n_tokens // ROUTE_TILE,),
        in_specs=[
            pl.BlockSpec((ROUTE_TILE, 2 * half), lambda i: (i, 0)),
            pl.BlockSpec((None, NKH, half), lambda i: (0, 0, 0)),
            pl.BlockSpec((None, NKH, half), lambda i: (1, 0, 0)),
        ],
        out_specs=[spec_hk, spec_hk, spec_kh, spec_kh],
        out_shape=[shape_hk, shape_hk, shape_kh, shape_kh],
        scratch_shapes=[stage, stage, stage, stage, cand3, cand3, cand3, cand3, top3, top3],
        compiler_params=_cparams("parallel"),
        name="peer_route",
    )(q, key_blocks, key_blocks)


def _tables_kernel(u_ref, v_ref, ub_ref, vt_ref):
    ub_ref[...] = u_ref[...].astype(BF16)
    vt_ref[...] = v_ref[...].T.astype(BF16)


def _prepare_tables(u_tab, v_tab):
    n_exp = u_tab.shape[0]
    row = pl.BlockSpec((TABLE_BLOCK, D_MODEL), lambda i: (i, 0))
    return pl.pallas_call(
        _tables_kernel,
        grid=(n_exp // TABLE_BLOCK,),
        in_specs=[row, row],
        out_specs=[row, pl.BlockSpec((D_MODEL, TABLE_BLOCK), lambda i: (0, i))],
        out_shape=[jax.ShapeDtypeStruct((n_exp, D_MODEL), BF16), jax.ShapeDtypeStruct((D_MODEL, n_exp), BF16)],
        compiler_params=_cparams("parallel"),
        name="peer_tables",
    )(u_tab, v_tab)


def _peer_dense_kernel(ht_ref, u_ref, vt_ref, r2_ref, w2_ref, n1_ref, w1_ref, o_ref):
    c = pl.program_id(1)
    sub = BF16_SUBLANES
    zero = jnp.zeros((), BF16)
    keys_per_sub = SUB_EXPERTS // PEER_NKEYS
    parts = []
    for sb in range(DENSE_EXPERTS // SUB_EXPERTS):
        s = jnp.dot(u_ref[sb * SUB_EXPERTS:(sb + 1) * SUB_EXPERTS, :], ht_ref[...],
                    preferred_element_type=F32)
        act = _gelu(s.astype(BF16))
        for cc in range(keys_per_sub):
            key1 = c * (DENSE_EXPERTS // PEER_NKEYS) + sb * keys_per_sub + cc
            gate = jnp.zeros((PEER_NKEYS // sub, sub, DENSE_TILE), BF16)
            for h in range(PEER_HEADS):
                n1 = jnp.broadcast_to(n1_ref[key1, pl.ds(h, 1), :], (sub, DENSE_TILE)).astype(BF16)
                w1 = jnp.broadcast_to(w1_ref[key1, pl.ds(h, 1), :], (sub, DENSE_TILE)).astype(BF16)
                r2 = r2_ref[h].reshape(PEER_NKEYS // sub, sub, DENSE_TILE)
                w2 = w2_ref[h].reshape(PEER_NKEYS // sub, sub, DENSE_TILE)
                gate = gate + jnp.where(r2 < n1[None], w2, zero) * w1[None]
            parts.append(act[cc * PEER_NKEYS:(cc + 1) * PEER_NKEYS] * gate.reshape(PEER_NKEYS, DENSE_TILE))
    a = jnp.concatenate(parts, axis=0)
    upd = jnp.dot(vt_ref[...], a, preferred_element_type=F32)
    o_ref[...] = jnp.where(c == 0, 0.0, o_ref[...]) + upd


def _peer_dense(tok, h_t, u_tab, vt_tab, route):
    r2, w2, n1, w1 = route
    spec_hk = pl.BlockSpec((PEER_HEADS, PEER_NKEYS, DENSE_TILE), lambda i, c: (0, 0, i))
    spec_kh = pl.BlockSpec((PEER_NKEYS, PEER_HEADS, DENSE_TILE), lambda i, c: (0, 0, i))
    col_spec = pl.BlockSpec((D_MODEL, DENSE_TILE), lambda i, c: (0, i))
    return pl.pallas_call(
        _peer_dense_kernel,
        grid=(tok.total // DENSE_TILE, PEER_EXPERTS // DENSE_EXPERTS),
        in_specs=[
            col_spec,
            pl.BlockSpec((DENSE_EXPERTS, D_MODEL), lambda i, c: (c, 0)),
            pl.BlockSpec((D_MODEL, DENSE_EXPERTS), lambda i, c: (0, c)),
            spec_hk, spec_hk, spec_kh, spec_kh,
        ],
        out_specs=col_spec,
        out_shape=jax.ShapeDtypeStruct((D_MODEL, tok.total), F32),
        compiler_params=_cparams("parallel", "arbitrary"),
        name="peer_dense",
    )(h_t, u_tab, vt_tab, r2, w2, n1, w1)


def kernel(x_prompt, x_sample, state_ssm_re, state_ssm_im, cache_k, cache_v, c, c_ctx, w_mod, b_mod, ln_g, ln_b, ssm_w_in, ssm_a_re, ssm_a_im, ssm_log_dt, ssm_b_re, ssm_b_im, ssm_c_re, ssm_c_im, ssm_d, ssm_w_glu, ssm_w_out, attn_w_qkv, attn_sink, attn_w_out, peer_w_q, peer_keys, peer_u, peer_v):
    depth = w_mod.shape[0]
    n_batch, seq, _ = x_prompt.shape
    dec_batch, dec_seq, _ = x_sample.shape
    tok = _Tokens(n_batch * seq, dec_batch, dec_seq)
    alpha = (2 * depth) ** 0.25

    cond = jnp.concatenate([c, c_ctx[None, :],
                            jnp.zeros((MOD_ROWS - dec_batch - 1, D_MODEL), F32)], axis=0)
    mods_all = _modulation_all(cond, w_mod, b_mod)

    x = jnp.concatenate([x_prompt.reshape(-1, D_MODEL), x_sample.reshape(-1, D_MODEL)], axis=0)
    st_re, st_im, st_k, st_v = [], [], [], []
    for i in range(depth):
        j = i // 2
        mods = mods_all[i]
        if i % 2 == 0:
            u = _mod_linear(tok, x, mods, ssm_w_in[j].astype(BF16), 0, "s5_in")
            ops = _s5_operators(ssm_a_re[j], ssm_a_im[j], ssm_log_dt[j], ssm_b_re[j], ssm_b_im[j],
                                ssm_c_re[j], ssm_c_im[j])
            zeros = jnp.zeros((SSM_GROUPS, n_batch, 4 * SSM_STATE), F32)
            y_c, s_c = _s5_core(_to_chunks(u[:tok.n_ctx], n_batch, seq), zeros, ops,
                                seq // SSM_CHUNK, n_batch, "s5_core_context")
            h0 = _state_to_cols(state_ssm_re[:, j], state_ssm_im[:, j])
            y_s, _ = _s5_core(_to_chunks(u[tok.n_ctx:], dec_batch, dec_seq), h0, ops,
                              dec_seq // SSM_CHUNK, dec_batch, "s5_core_latent")
            y = jnp.concatenate([_from_chunks(y_c, n_batch, seq), _from_chunks(y_s, dec_batch, dec_seq)], axis=0)
            s_re, s_im = _cols_to_state(s_c)
            st_re.append(s_re)
            st_im.append(s_im)
            x = _s5_tail(tok, y, u, x, mods, ssm_d[j], ssm_w_glu[j].astype(BF16), ssm_w_out[j].astype(BF16),
                         ln_g[i, 0], ln_b[i, 0], alpha)
        else:
            qkv = _mod_linear(tok, x, mods, _attn_weights(attn_w_qkv[j]), 0, "attn_qkv")
            o_c, k_c, v_c = _attn_context(qkv, attn_sink[j], n_batch, seq)
            past = cache_k.shape[2]
            o_s = _attn_latent(qkv, _rope_tables(dec_seq), cache_k[:, j].reshape(dec_batch, past, KV_DIM),
                               cache_v[:, j].reshape(dec_batch, past, KV_DIM), attn_sink[j], tok)
            st_k.append(k_c.reshape(n_batch, seq, N_KV_HEADS, HEAD_DIM))
            st_v.append(v_c.reshape(n_batch, seq, N_KV_HEADS, HEAD_DIM))
            x = _mm_postnorm(tok, jnp.concatenate([o_c, o_s], axis=0), x, mods, attn_w_out[j].astype(BF16),
                             ln_g[i, 0], ln_b[i, 0], 2, alpha, "attn_out")
        w_q, key_blocks = _route_weights(peer_w_q[i], peer_keys[i])
        q, h_t = _mod_linear_t(tok, x, mods, w_q, 3, "peer_query")
        route = _peer_route(q, key_blocks, tok.total)
        u_tab, vt_tab = _prepare_tables(peer_u[i], peer_v[i])
        o_t = _peer_dense(tok, h_t, u_tab, vt_tab, route)
        x = _postnorm(tok, o_t, x, mods, ln_g[i, 1], ln_b[i, 1], 5, alpha, "peer_norm")

    y_prompt = x[:tok.n_ctx].reshape(n_batch, seq, D_MODEL)
    y_sample = x[tok.n_ctx:].reshape(dec_batch, dec_seq, D_MODEL)
    return (y_prompt, y_sample, jnp.stack(st_re, axis=1), jnp.stack(st_im, axis=1),
            jnp.stack(st_k, axis=1), jnp.stack(st_v, axis=1))
```

```python
import functools
import math

import jax
import jax.numpy as jnp
from jax import lax
from jax.experimental import pallas as pl
from jax.experimental.pallas import tpu as pltpu

F32 = jnp.float32
BF16 = jnp.bfloat16

D_MODEL = 1024
GRID_W = 64
SSM_GROUP = 16
SSM_GROUPS = D_MODEL // SSM_GROUP
SSM_STATE = 64
SSM_CHUNK = 16
SSM_ROW = SSM_CHUNK * SSM_GROUP
HEAD_DIM = 64
N_HEADS = D_MODEL // HEAD_DIM
N_KV_HEADS = 4
KV_REP = N_HEADS // N_KV_HEADS
KV_DIM = N_KV_HEADS * HEAD_DIM
WINDOW = 128
ATTN_BLOCK = 128
ROPE_BASE = 10000.0
ROPE_AXIS_DIM = HEAD_DIM // 2
NEG_INF = -1e30
PEER_HEADS = 8
PEER_NKEYS = 128
PEER_EXPERTS = PEER_NKEYS * PEER_NKEYS
PEER_TOPK = 16
PEER_KEY_DIM = 256
PEER_HALF = PEER_KEY_DIM // 2
LN_EPS = 1e-5

TOKEN_TILE = 512
DENSE_TILE = 512
DENSE_EXPERTS = 2048
TABLE_BLOCK = 1024
SUB_EXPERTS = 512
MOD_ROWS = 16
BF16_SUBLANES = 16
VMEM_LIMIT = 56 * 1024 * 1024

_NT = (((1,), (1,)), ((), ()))


def _cparams(*sem):
    return pltpu.CompilerParams(dimension_semantics=sem, vmem_limit_bytes=VMEM_LIMIT)


def _gelu(x):
    return 0.5 * x * (1.0 + jnp.tanh(0.7978845608028654 * (x + 0.044715 * (x * x * x))))


def _layer_norm_rows(z, g, b):
    mu = jnp.mean(z, axis=-1, keepdims=True)
    zc = z - mu
    var = jnp.mean(zc * zc, axis=-1, keepdims=True)
    return zc * lax.rsqrt(var + LN_EPS) * g + b


def _mod_kernel(c_ref, w_ref, b_ref, o_ref):
    c = c_ref[...]
    s = c * jax.nn.sigmoid(c)
    o_ref[...] = jnp.dot(s.astype(BF16), w_ref[...].astype(BF16),
                         preferred_element_type=F32) + b_ref[...]


def _modulation_all(cond, w_mod, b_mod):
    depth = w_mod.shape[0]
    n_out = w_mod.shape[2]
    nb = n_out // D_MODEL
    out = pl.pallas_call(
        _mod_kernel,
        grid=(depth, nb),
        in_specs=[
            pl.BlockSpec((MOD_ROWS, D_MODEL), lambda i, j: (0, 0)),
            pl.BlockSpec((None, D_MODEL, D_MODEL), lambda i, j: (i, 0, j)),
            pl.BlockSpec((None, 1, D_MODEL), lambda i, j: (i, 0, j)),
        ],
        out_specs=pl.BlockSpec((None, MOD_ROWS, D_MODEL), lambda i, j: (i, 0, j)),
        out_shape=jax.ShapeDtypeStruct((depth, MOD_ROWS, n_out), F32),
        compiler_params=_cparams("arbitrary", "arbitrary"),
        name="modulation",
    )(cond, w_mod, b_mod.reshape(depth, 1, n_out))
    return out.reshape(depth, MOD_ROWS, nb, D_MODEL)


class _Tokens:
    def __init__(self, n_ctx_tokens, dec_batch, dec_seq):
        self.n_ctx = n_ctx_tokens
        self.dec_batch = dec_batch
        self.dec_seq = dec_seq
        self.total = n_ctx_tokens + dec_batch * dec_seq
        assert n_ctx_tokens % TOKEN_TILE == 0 and dec_seq % TOKEN_TILE == 0
        self.ctx_tiles = n_ctx_tokens // TOKEN_TILE
        self.tiles = self.total // TOKEN_TILE
        self.tiles_per_seq = dec_seq // TOKEN_TILE

    def mod_row(self, i):
        return jnp.where(i < self.ctx_tiles, self.dec_batch, (i - self.ctx_tiles) // self.tiles_per_seq)


def _mod_spec(tok):
    return pl.BlockSpec((1, 6, D_MODEL), lambda i: (tok.mod_row(i), 0, 0))


def _row_spec(width):
    return pl.BlockSpec((TOKEN_TILE, width), lambda i: (i, 0))


def _full_spec(shape):
    nd = len(shape)
    return pl.BlockSpec(shape, lambda i: (0,) * nd)


def _linear_kernel(x_ref, m_ref, w_ref, o_ref, *, shift_idx):
    m = m_ref[0]
    h = x_ref[...] * (1.0 + m[shift_idx + 1:shift_idx + 2]) + m[shift_idx:shift_idx + 1]
    o_ref[...] = jnp.dot(h.astype(BF16), w_ref[...], preferred_element_type=F32)


def _mod_linear(tok, x, mods, w, shift_idx, name):
    n = w.shape[1]
    return pl.pallas_call(
        functools.partial(_linear_kernel, shift_idx=shift_idx),
        grid=(tok.tiles,),
        in_specs=[_row_spec(D_MODEL), _mod_spec(tok), _full_spec((D_MODEL, n))],
        out_specs=_row_spec(n),
        out_shape=jax.ShapeDtypeStruct((tok.total, n), F32),
        compiler_params=_cparams("parallel"),
        name=name,
    )(x, mods, w)


def _col_spec():
    return pl.BlockSpec((D_MODEL, TOKEN_TILE), lambda i: (0, i))


def _linear_t_kernel(x_ref, m_ref, w_ref, o_ref, ht_ref, *, shift_idx):
    m = m_ref[0]
    h = x_ref[...] * (1.0 + m[shift_idx + 1:shift_idx + 2]) + m[shift_idx:shift_idx + 1]
    o_ref[...] = jnp.dot(h.astype(BF16), w_ref[...], preferred_element_type=F32)
    ht_ref[...] = h.T.astype(BF16)


def _mod_linear_t(tok, x, mods, w, shift_idx, name):
    n = w.shape[1]
    return pl.pallas_call(
        functools.partial(_linear_t_kernel, shift_idx=shift_idx),
        grid=(tok.tiles,),
        in_specs=[_row_spec(D_MODEL), _mod_spec(tok), _full_spec((D_MODEL, n))],
        out_specs=[_row_spec(n), _col_spec()],
        out_shape=[jax.ShapeDtypeStruct((tok.total, n), F32),
                   jax.ShapeDtypeStruct((D_MODEL, tok.total), BF16)],
        compiler_params=_cparams("parallel"),
        name=name,
    )(x, mods, w)


def _post_norm_rows(x, out, m, gate_idx, g_ref, b_ref, alpha):
    z = alpha * x + (1.0 + m[gate_idx:gate_idx + 1]) * out
    return _layer_norm_rows(z, g_ref[...], b_ref[...])


def _mm_postnorm_kernel(a_ref, x_ref, m_ref, w_ref, g_ref, b_ref, o_ref, *, gate_idx, alpha):
    y = jnp.dot(a_ref[...].astype(BF16), w_ref[...], preferred_element_type=F32)
    o_ref[...] = _post_norm_rows(x_ref[...], y, m_ref[0], gate_idx, g_ref, b_ref, alpha)


def _mm_postnorm(tok, a, x, mods, w, g, b, gate_idx, alpha, name):
    return pl.pallas_call(
        functools.partial(_mm_postnorm_kernel, gate_idx=gate_idx, alpha=alpha),
        grid=(tok.tiles,),
        in_specs=[_row_spec(D_MODEL), _row_spec(D_MODEL), _mod_spec(tok),
                  _full_spec((D_MODEL, D_MODEL)), _full_spec((1, D_MODEL)), _full_spec((1, D_MODEL))],
        out_specs=_row_spec(D_MODEL),
        out_shape=jax.ShapeDtypeStruct((tok.total, D_MODEL), F32),
        compiler_params=_cparams("parallel"),
        name=name,
    )(a, x, mods, w, g.reshape(1, D_MODEL), b.reshape(1, D_MODEL))


def _postnorm_kernel(ot_ref, x_ref, m_ref, g_ref, b_ref, o_ref, *, gate_idx, alpha):
    o_ref[...] = _post_norm_rows(x_ref[...], ot_ref[...].T, m_ref[0], gate_idx, g_ref, b_ref, alpha)


def _postnorm(tok, out_t, x, mods, g, b, gate_idx, alpha, name):
    return pl.pallas_call(
        functools.partial(_postnorm_kernel, gate_idx=gate_idx, alpha=alpha),
        grid=(tok.tiles,),
        in_specs=[_col_spec(), _row_spec(D_MODEL), _mod_spec(tok),
                  _full_spec((1, D_MODEL)), _full_spec((1, D_MODEL))],
        out_specs=_row_spec(D_MODEL),
        out_shape=jax.ShapeDtypeStruct((tok.total, D_MODEL), F32),
        compiler_params=_cparams("parallel"),
        name=name,
    )(out_t, x, mods, g.reshape(1, D_MODEL), b.reshape(1, D_MODEL))


def _s5_operators(a_re, a_im, log_dt, b_re, b_im, c_re, c_im):
    hi = lax.Precision.HIGHEST
    dt = jnp.exp(log_dt)[..., None]
    ar, ai = a_re * dt, a_im * dt
    k = jnp.arange(SSM_CHUNK + 1, dtype=F32)[:, None, None, None]
    mag = jnp.exp(k * ar[None])
    pw_re, pw_im = mag * jnp.cos(k * ai[None]), mag * jnp.sin(k * ai[None])
    l_re, l_im = pw_re[1], pw_im[1]
    den = a_re * a_re + a_im * a_im
    q_re = ((l_re - 1.0) * a_re + l_im * a_im) / den
    q_im = (l_im * a_re - (l_re - 1.0) * a_im) / den
    bt_re = q_re[..., None] * b_re - q_im[..., None] * b_im
    bt_im = q_re[..., None] * b_im + q_im[..., None] * b_re
    d_re = pw_re[..., None] * bt_re[None] - pw_im[..., None] * bt_im[None]
    d_im = pw_re[..., None] * bt_im[None] + pw_im[..., None] * bt_re[None]
    pw_re_c, pw_im_c = pw_re[:, :, :, None, :], pw_im[:, :, :, None, :]
    e_re = c_re[None] * pw_re_c - c_im[None] * pw_im_c
    e_im = c_re[None] * pw_im_c + c_im[None] * pw_re_c
    kern = (jnp.einsum('dgip,kdgpj->kdgij', c_re, d_re, precision=hi)
            - jnp.einsum('dgip,kdgpj->kdgij', c_im, d_im, precision=hi))
    s = jnp.arange(SSM_CHUNK)
    lag = s[None, :] - s[:, None]
    kf = kern[jnp.clip(lag, 0, SSM_CHUNK), 0]
    kb = kern[jnp.clip(-lag, 0, SSM_CHUNK), 1]
    mix = (jnp.where((lag >= 0)[:, :, None, None, None], kf, 0.0)
           + jnp.where((lag <= 0)[:, :, None, None, None], kb, 0.0))
    g = SSM_GROUPS
    m_op = jnp.transpose(mix, (2, 0, 4, 1, 3)).reshape(g, SSM_ROW, SSM_ROW)
    pf = (SSM_CHUNK - 1) - s
    win = jnp.stack([d_re[pf, 0], d_re[s, 1], d_im[pf, 0], d_im[s, 1]], axis=0)
    win_op = jnp.transpose(win, (2, 1, 4, 0, 3)).reshape(g, SSM_ROW, 4 * SSM_STATE)
    wout = jnp.stack([e_re[s + 1, 0], e_re[SSM_CHUNK - s, 1], -e_im[s + 1, 0], -e_im[SSM_CHUNK - s, 1]],
                     axis=0)
    wout_op = jnp.transpose(wout, (2, 0, 4, 1, 3)).reshape(g, 4 * SSM_STATE, SSM_ROW)
    lam = jnp.stack([jnp.concatenate([pw_re[SSM_CHUNK, 0], pw_re[SSM_CHUNK, 1]], axis=-1),
                     jnp.concatenate([pw_im[SSM_CHUNK, 0], pw_im[SSM_CHUNK, 1]], axis=-1)], axis=1)
    return m_op, win_op, wout_op, lam


def _s5_core_kernel(u_ref, h0_ref, m_ref, win_ref, wout_ref, lam_ref, y_ref, s_ref, z_scr, p_scr,
                    *, n_chunks, batch):
    hi = lax.Precision.HIGHEST
    p2 = 2 * SSM_STATE
    u = u_ref[...]
    z_scr[...] = jnp.dot(u.astype(BF16), win_ref[...].astype(BF16), preferred_element_type=F32)
    l_re = lam_ref[0:1, :]
    l_im = lam_ref[1:2, :]
    is_fwd = lax.broadcasted_iota(jnp.int32, (batch, p2), 1) < SSM_STATE

    def step(k, carry):
        s_re, s_im = carry
        rf = pl.multiple_of(k * batch, batch)
        rb = pl.multiple_of((n_chunks - 1 - k) * batch, batch)
        zf = z_scr[pl.ds(rf, batch), :]
        zb = z_scr[pl.ds(rb, batch), :]
        z_re = jnp.where(is_fwd, zf[:, :p2], zb[:, :p2])
        z_im = jnp.where(is_fwd, zf[:, p2:], zb[:, p2:])
        p_scr[pl.ds(rf, batch), 0:SSM_STATE] = s_re[:, :SSM_STATE]
        p_scr[pl.ds(rf, batch), p2:p2 + SSM_STATE] = s_im[:, :SSM_STATE]
        p_scr[pl.ds(rb, batch), SSM_STATE:p2] = s_re[:, SSM_STATE:]
        p_scr[pl.ds(rb, batch), p2 + SSM_STATE:] = s_im[:, SSM_STATE:]
        n_re = l_re * s_re - l_im * s_im + z_re
        n_im = l_re * s_im + l_im * s_re + z_im
        return n_re, n_im

    h0 = h0_ref[...]
    s_re, s_im = lax.fori_loop(0, n_chunks, step, (h0[:, :p2], h0[:, p2:]))
    s_ref[:, :p2] = s_re
    s_ref[:, p2:] = s_im
    y_ref[...] = (jnp.dot(u.astype(BF16), m_ref[...].astype(BF16), preferred_element_type=F32)
                  + jnp.dot(p_scr[...].astype(BF16), wout_ref[...].astype(BF16), preferred_element_type=F32))


def _s5_core(ug, h0, ops, n_chunks, batch, name):
    m_op, win_op, wout_op, lam = ops
    g, rows, _ = ug.shape
    st = 4 * SSM_STATE
    return pl.pallas_call(
        functools.partial(_s5_core_kernel, n_chunks=n_chunks, batch=batch),
        grid=(g,),
        in_specs=[
            pl.BlockSpec((None, rows, SSM_ROW), lambda i: (i, 0, 0)),
            pl.BlockSpec((None, batch, st), lambda i: (i, 0, 0)),
            pl.BlockSpec((None, SSM_ROW, SSM_ROW), lambda i: (i, 0, 0)),
            pl.BlockSpec((None, SSM_ROW, st), lambda i: (i, 0, 0)),
            pl.BlockSpec((None, st, SSM_ROW), lambda i: (i, 0, 0)),
            pl.BlockSpec((None, 2, 2 * SSM_STATE), lambda i: (i, 0, 0)),
        ],
        out_specs=[
            pl.BlockSpec((None, rows, SSM_ROW), lambda i: (i, 0, 0)),
            pl.BlockSpec((None, batch, st), lambda i: (i, 0, 0)),
        ],
        out_shape=[jax.ShapeDtypeStruct((g, rows, SSM_ROW), F32),
                   jax.ShapeDtypeStruct((g, batch, st), F32)],
        scratch_shapes=[pltpu.VMEM((rows, st), F32), pltpu.VMEM((rows, st), F32)],
        compiler_params=_cparams("parallel"),
        name=name,
    )(ug, h0, m_op, win_op, wout_op, lam)


def _to_chunks(u, batch, seq):
    n_chunks = seq // SSM_CHUNK
    t = u.reshape(batch, n_chunks, SSM_CHUNK, SSM_GROUPS, SSM_GROUP)
    return jnp.transpose(t, (3, 1, 0, 2, 4)).reshape(SSM_GROUPS, n_chunks * batch, SSM_ROW)


def _from_chunks(y, batch, seq):
    n_chunks = seq // SSM_CHUNK
    t = y.reshape(SSM_GROUPS, n_chunks, batch, SSM_CHUNK, SSM_GROUP)
    return jnp.transpose(t, (2, 1, 3, 0, 4)).reshape(batch * seq, D_MODEL)


def _state_to_cols(h_re, h_im):
    cols = jnp.concatenate([h_re[:, 0], h_re[:, 1], h_im[:, 0], h_im[:, 1]], axis=-1)
    return jnp.transpose(cols, (1, 0, 2))


def _cols_to_state(s):
    t = jnp.transpose(s, (1, 0, 2)).reshape(s.shape[1], SSM_GROUPS, 2, 2, SSM_STATE)
    t = jnp.transpose(t, (2, 0, 3, 1, 4))
    return t[0], t[1]


def _s5_tail_kernel(y_ref, u_ref, x_ref, m_ref, d_ref, wg_ref, wo_ref, g_ref, b_ref, o_ref, *, alpha):
    v = _gelu(y_ref[...] + d_ref[...] * u_ref[...])
    gl = jnp.dot(v.astype(BF16), wg_ref[...], preferred_element_type=F32)
    a = gl[:, :D_MODEL] * jax.nn.sigmoid(gl[:, D_MODEL:])
    out = jnp.dot(a.astype(BF16), wo_ref[...], preferred_element_type=F32)
    o_ref[...] = _post_norm_rows(x_ref[...], out, m_ref[0], 2, g_ref, b_ref, alpha)


def _s5_tail(tok, y, u, x, mods, d, w_glu, w_out, g, b, alpha):
    return pl.pallas_call(
        functools.partial(_s5_tail_kernel, alpha=alpha),
        grid=(tok.tiles,),
        in_specs=[_row_spec(D_MODEL), _row_spec(D_MODEL), _row_spec(D_MODEL), _mod_spec(tok),
                  _full_spec((1, D_MODEL)), _full_spec((D_MODEL, 2 * D_MODEL)),
                  _full_spec((D_MODEL, D_MODEL)), _full_spec((1, D_MODEL)), _full_spec((1, D_MODEL))],
        out_specs=_row_spec(D_MODEL),
        out_shape=jax.ShapeDtypeStruct((tok.total, D_MODEL), F32),
        compiler_params=_cparams("parallel"),
        name="s5_tail",
    )(y, u, x, mods, d.reshape(1, D_MODEL), w_glu, w_out, g.reshape(1, D_MODEL), b.reshape(1, D_MODEL))


QKV_COLS = 3 * D_MODEL


def _attn_weights(w_qkv):
    wq, wk, wv = w_qkv[:, :D_MODEL], w_qkv[:, D_MODEL:D_MODEL + KV_DIM], w_qkv[:, D_MODEL + KV_DIM:]
    half = ROPE_AXIS_DIM // 2

    def partner(w):
        n = w.shape[1]
        t = w.reshape(D_MODEL, n // ROPE_AXIS_DIM, 2, half)
        return jnp.stack([-t[:, :, 1], t[:, :, 0]], axis=2).reshape(D_MODEL, n)

    pad = jnp.zeros((D_MODEL, QKV_COLS - 2 * D_MODEL - 3 * KV_DIM), w_qkv.dtype)
    return jnp.concatenate([wq, partner(wq), wk, wv, partner(wk), pad], axis=1).astype(BF16)


def _rope_tables(seq):
    half = ROPE_AXIS_DIM // 2
    rows = seq // GRID_W
    row = jnp.repeat(jnp.arange(rows, dtype=F32), GRID_W)
    col = jnp.tile(jnp.arange(GRID_W, dtype=F32), rows)
    inv = ROPE_BASE ** (-jnp.arange(half, dtype=F32) / half)
    ang_r = row[:, None] * inv[None, :]
    ang_c = col[:, None] * inv[None, :]
    ang = jnp.concatenate([ang_r, ang_r, ang_c, ang_c], axis=-1)
    cos = jnp.tile(jnp.cos(ang), (1, N_KV_HEADS))
    sin = jnp.tile(jnp.sin(ang), (1, N_KV_HEADS))
    return jnp.concatenate([cos, sin], axis=-1)


def _softmax_heads(q, k_list, v_list, mask_list, sink_ref, o_ref, heads_per_pass):
    scale = HEAD_DIM ** -0.5
    tq = q.shape[0]
    masks = [None if m is None else jnp.concatenate([m] * heads_per_pass, axis=0) for m in mask_list]
    for first in range(0, N_HEADS, heads_per_pass):
        heads = range(first, first + heads_per_pass)
        g = first // KV_REP
        qg = jnp.concatenate([q[:, h * HEAD_DIM:(h + 1) * HEAD_DIM] for h in heads], axis=0).astype(BF16)
        sink = jnp.concatenate([jnp.full((tq, 1), sink_ref[h], F32) for h in heads], axis=0)
        cols = slice(g * HEAD_DIM, (g + 1) * HEAD_DIM)
        logits = []
        for k, mask in zip(k_list, masks):
            s = lax.dot_general(qg, k[:, cols], _NT, preferred_element_type=F32) * scale
            if mask is not None:
                s = jnp.where(mask, s, NEG_INF)
            logits.append(s)
        m = sink
        for s in logits:
            m = jnp.maximum(m, jnp.max(s, axis=-1, keepdims=True))
        denom = jnp.exp(sink - m)
        acc = jnp.zeros((heads_per_pass * tq, HEAD_DIM), F32)
        for s, v in zip(logits, v_list):
            p = jnp.exp(s - m)
            denom = denom + jnp.sum(p, axis=-1, keepdims=True)
            acc = acc + jnp.dot(p.astype(BF16), v[:, cols], preferred_element_type=F32)
        out = acc / denom
        for r, h in enumerate(heads):
            o_ref[:, h * HEAD_DIM:(h + 1) * HEAD_DIM] = out[r * tq:(r + 1) * tq]


def _attn_ctx_kernel(sink_ref, q_ref, kv_ref, o_ref, k_ref, v_ref):
    kv = kv_ref[...]
    k_ref[...] = kv[:, :KV_DIM]
    v_ref[...] = kv[:, KV_DIM:2 * KV_DIM]
    k = kv[:, :KV_DIM].astype(BF16)
    v = kv[:, KV_DIM:2 * KV_DIM].astype(BF16)
    _softmax_heads(q_ref[...], [k], [v], [None], sink_ref, o_ref, 1)


def _attn_context(qkv, sink, n_batch, seq):
    kv_spec = pl.BlockSpec((seq, KV_DIM), lambda b: (b, 0))
    kv_shape = jax.ShapeDtypeStruct((n_batch * seq, KV_DIM), F32)
    return pl.pallas_call(
        _attn_ctx_kernel,
        grid=(n_batch,),
        in_specs=[
            pl.BlockSpec(memory_space=pltpu.SMEM),
            pl.BlockSpec((seq, D_MODEL), lambda b: (b, 0)),
            pl.BlockSpec((seq, D_MODEL), lambda b: (b, 2)),
        ],
        out_specs=[pl.BlockSpec((seq, D_MODEL), lambda b: (b, 0)), kv_spec, kv_spec],
        out_shape=[jax.ShapeDtypeStruct((n_batch * seq, D_MODEL), F32), kv_shape, kv_shape],
        compiler_params=_cparams("parallel"),
        name="attn_context",
    )(sink, qkv, qkv)


def _attn_lat_kernel(sink_ref, q_ref, qp_ref, kv0_ref, kv1_ref, kv2_ref, t0_ref, t1_ref, t2_ref,
                     kc_ref, vc_ref, o_ref, *, n_qblocks):
    qb = pl.program_id(1)
    t1 = t1_ref[...]
    cos_q = jnp.concatenate([t1[:, :KV_DIM]] * KV_REP, axis=-1)
    sin_q = jnp.concatenate([t1[:, KV_DIM:]] * KV_REP, axis=-1)
    q = q_ref[...] * cos_q + qp_ref[...] * sin_q

    def rope_k(kv_ref, t_ref):
        kv = kv_ref[...]
        t = t_ref[...]
        k = kv[:, :KV_DIM] * t[:, :KV_DIM] + kv[:, 2 * KV_DIM:3 * KV_DIM] * t[:, KV_DIM:]
        return k.astype(BF16), kv[:, KV_DIM:2 * KV_DIM].astype(BF16)

    k0, v0 = rope_k(kv0_ref, t0_ref)
    k1, v1 = rope_k(kv1_ref, t1_ref)
    k2, v2 = rope_k(kv2_ref, t2_ref)
    k_loc = jnp.concatenate([k0, k1, k2], axis=0)
    v_loc = jnp.concatenate([v0, v1, v2], axis=0)
    span = 3 * ATTN_BLOCK
    qi = lax.broadcasted_iota(jnp.int32, (ATTN_BLOCK, span), 0)
    kj = lax.broadcasted_iota(jnp.int32, (ATTN_BLOCK, span), 1) - ATTN_BLOCK
    ok = (jnp.abs(kj - qi) <= WINDOW)
    ok = ok & ((kj >= 0) | (qb > 0)) & ((kj < ATTN_BLOCK) | (qb < n_qblocks - 1))
    kc = kc_ref[0].astype(BF16)
    vc = vc_ref[0].astype(BF16)
    _softmax_heads(q, [kc, k_loc], [vc, v_loc], [None, ok], sink_ref, o_ref, KV_REP)


def _attn_latent(qkv, tables, k_ctx, v_ctx, sink, tok):
    nqb = tok.dec_seq // ATTN_BLOCK
    base = tok.n_ctx // ATTN_BLOCK
    past = k_ctx.shape[1]

    def rows(b, i, off):
        return base + b * nqb + jnp.clip(i + off, 0, nqb - 1)

    def kv_spec(off):
        return pl.BlockSpec((ATTN_BLOCK, D_MODEL), lambda b, i: (rows(b, i, off), 2))

    def tab_spec(off):
        return pl.BlockSpec((ATTN_BLOCK, 2 * KV_DIM), lambda b, i: (jnp.clip(i + off, 0, nqb - 1), 0))

    return pl.pallas_call(
        functools.partial(_attn_lat_kernel, n_qblocks=nqb),
        grid=(tok.dec_batch, nqb),
        in_specs=[
            pl.BlockSpec(memory_space=pltpu.SMEM),
            pl.BlockSpec((ATTN_BLOCK, D_MODEL), lambda b, i: (rows(b, i, 0), 0)),
            pl.BlockSpec((ATTN_BLOCK, D_MODEL), lambda b, i: (rows(b, i, 0), 1)),
            kv_spec(-1), kv_spec(0), kv_spec(1),
            tab_spec(-1), tab_spec(0), tab_spec(1),
            pl.BlockSpec((1, past, KV_DIM), lambda b, i: (b, 0, 0)),
            pl.BlockSpec((1, past, KV_DIM), lambda b, i: (b, 0, 0)),
        ],
        out_specs=pl.BlockSpec((ATTN_BLOCK, D_MODEL), lambda b, i: (b * nqb + i, 0)),
        out_shape=jax.ShapeDtypeStruct((tok.dec_batch * tok.dec_seq, D_MODEL), F32),
        compiler_params=_cparams("parallel", "arbitrary"),
        name="attn_latent",
    )(sink, qkv, qkv, qkv, qkv, qkv, tables, tables, tables, k_ctx, v_ctx)


ROUTE_TILE = 256
ROUTE_LANES = 128
NKH = PEER_NKEYS * PEER_HEADS


def _oddeven_merge_sort_pairs(n):
    pairs = []

    def merge(lo, hi, r):
        step = r * 2
        if step < hi - lo:
            merge(lo, hi, step)
            merge(lo + r, hi, step)
            pairs.extend((i, i + r) for i in range(lo + r, hi - r, step))
        else:
            pairs.append((lo, lo + r))

    def sort(lo, hi):
        if hi - lo >= 1:
            mid = lo + (hi - lo) // 2
            sort(lo, mid)
            sort(mid + 1, hi)
            merge(lo, hi, 1)

    sort(0, n - 1)
    return pairs


_SORT16 = _oddeven_merge_sort_pairs(PEER_TOPK)


def _exchange(x, i, j):
    x[i], x[j] = jnp.maximum(x[i], x[j]), jnp.minimum(x[i], x[j])


def _top16_sorted(vals):
    groups = []
    for g in range(0, len(vals), PEER_TOPK):
        x = list(vals[g:g + PEER_TOPK])
        for i, j in _SORT16:
            _exchange(x, i, j)
        groups.append(x)
    while len(groups) > 1:
        merged = []
        for a, b in zip(groups[0::2], groups[1::2]):
            x = [jnp.maximum(a[i], b[PEER_TOPK - 1 - i]) for i in range(PEER_TOPK)]
            for d in (8, 4, 2, 1):
                for i in range(PEER_TOPK):
                    if not i & d:
                        _exchange(x, i, i + d)
            merged.append(x)
        groups = merged
    return groups[0]


def _rank_bits(s, v):
    b3 = v[7] > s
    b2 = jnp.where(b3, v[11], v[3]) > s
    b1 = jnp.where(b3, jnp.where(b2, v[13], v[9]), jnp.where(b2, v[5], v[1])) > s
    hi = jnp.where(b2, jnp.where(b1, v[14], v[12]), jnp.where(b1, v[10], v[8]))
    lo = jnp.where(b2, jnp.where(b1, v[6], v[4]), jnp.where(b1, v[2], v[0]))
    b0 = jnp.where(b3, hi, lo) > s
    return (b3, b2, b1, b0), v[15] > s


def _select16(bits, leaves):
    b3, b2, b1, b0 = bits
    lvl = [jnp.where(b0, leaves[2 * i + 1], leaves[2 * i]) for i in range(8)]
    lvl = [jnp.where(b1, lvl[2 * i + 1], lvl[2 * i]) for i in range(4)]
    lvl = [jnp.where(b2, lvl[2 * i + 1], lvl[2 * i]) for i in range(2)]
    return jnp.where(b3, lvl[1], lvl[0])


def _exact_top16(s_ref, rank_ref, top_ref):
    n = s_ref.shape[0]
    rank_ref[...] = jnp.full(rank_ref.shape, float(PEER_TOPK), F32)

    def body(k, carry):
        s = s_ref[...]
        iota = lax.broadcasted_iota(jnp.int32, s.shape, 0).astype(F32)
        m = jnp.max(s, axis=0)
        idx = jnp.min(jnp.where(s == m[None], iota, float(n)), axis=0)
        hit = iota == idx[None]
        rank_ref[...] = jnp.where(hit, lax.convert_element_type(k, F32), rank_ref[...])
        s_ref[...] = jnp.where(hit, -jnp.inf, s)
        top_ref[k] = m
        return carry

    lax.fori_loop(0, PEER_TOPK, body, 0)


def _peer_route_kernel(q_ref, k1_ref, k2_ref, r2_ref, w2_ref, n1_ref, w1_ref,
                       r2_st, w2_st, n1_st, w1_st, sa_scr, ra_scr, sb_scr, rb_scr, ta_scr, tb_scr):
    q = q_ref[...].astype(BF16)
    half = PEER_HEADS * PEER_HALF
    s1_all = lax.dot_general(k1_ref[...], q[:, :half], _NT, preferred_element_type=F32)
    s2_all = lax.dot_general(k2_ref[...], q[:, half:], _NT, preferred_element_type=F32)
    slab = (PEER_HEADS, ROUTE_LANES)
    ninf = jnp.full(slab, -jnp.inf, F32)

    for j in range(ROUTE_TILE // ROUTE_LANES):
        lanes = slice(j * ROUTE_LANES, (j + 1) * ROUTE_LANES)
        s1 = [s1_all[k * PEER_HEADS:(k + 1) * PEER_HEADS, lanes] for k in range(PEER_NKEYS)]
        s2 = [s2_all[k * PEER_HEADS:(k + 1) * PEER_HEADS, lanes] for k in range(PEER_NKEYS)]
        v1 = _top16_sorted(s1)
        v2 = _top16_sorted(s2)
        cand = [[v1[a] + v2[b] for b in range(PEER_TOPK // (a + 1))] for a in range(PEER_TOPK)]
        flat = [c for row in cand for c in row]
        flat += [ninf] * (-len(flat) % PEER_TOPK)
        top = _top16_sorted(flat)
        theta = top[PEER_TOPK - 1]
        nb = []
        for row in cand:
            cnt = jnp.zeros(slab, F32)
            for c in row:
                cnt = cnt + jnp.where(c >= theta, 1.0, 0.0)
            nb.append(cnt)
        z = jnp.zeros(slab, F32)
        for k in range(PEER_TOPK):
            z = z + jnp.exp(top[k] - top[0])
        inv_z = 1.0 / z
        total = nb[0]
        for a in range(1, PEER_TOPK):
            total = total + nb[a]
        tied = total != float(PEER_TOPK)
        for k in range(PEER_TOPK - 1):
            tied = tied | ((v1[k] == v1[k + 1]) & (nb[k] != nb[k + 1])) | ((v2[k] == v2[k + 1]) & (nb[0] > float(k)))
        in1 = jnp.zeros(slab, F32)
        in2 = jnp.zeros(slab, F32)
        for k in range(PEER_NKEYS):
            rows = slice(k * PEER_HEADS, (k + 1) * PEER_HEADS)
            bits, out1 = _rank_bits(s1[k], v1)
            n1_st[rows, :] = jnp.where(out1, 0.0, _select16(bits, nb))
            w1_st[rows, :] = jnp.exp(s1[k] - v1[0]) * inv_z
            in1 = in1 + jnp.where(out1, 0.0, 1.0)
            (b3, b2, b1, b0), out2 = _rank_bits(s2[k], v2)
            r = (jnp.where(b3, 8.0, 0.0) + jnp.where(b2, 4.0, 0.0)) + (jnp.where(b1, 2.0, 0.0) + jnp.where(b0, 1.0, 0.0))
            r2_st[rows, :] = jnp.where(out2, float(PEER_TOPK), r)
            w2_st[rows, :] = jnp.exp(s2[k] - v2[0])
            in2 = in2 + jnp.where(out2, 0.0, 1.0)
        tied = tied | ((in1 != float(PEER_TOPK)) & (nb[PEER_TOPK - 1] != 0.0))
        tied = tied | ((in2 != float(PEER_TOPK)) & (nb[0] == float(PEER_TOPK)))
        n_tied = jnp.sum(jnp.where(tied, 1.0, 0.0))

        @pl.when(n_tied > 0.0)
        def _():
            shape3 = (PEER_NKEYS, PEER_HEADS, ROUTE_LANES)
            sa_scr[...] = s1_all[:, lanes].reshape(shape3)
            _exact_top16(sa_scr, ra_scr, ta_scr)
            sb_scr[...] = s2_all[:, lanes].reshape(shape3)
            _exact_top16(sb_scr, rb_scr, tb_scr)
            r1 = ra_scr[...]
            r2_st[...] = rb_scr[...].reshape(NKH, ROUTE_LANES)
            e1 = jnp.exp(s1_all[:, lanes].reshape(shape3) - ta_scr[0][None])
            w2_st[...] = jnp.exp(s2_all[:, lanes] - jnp.tile(tb_scr[0], (PEER_NKEYS, 1)))
            n_cand = len(flat)
            pos = 0
            for a in range(PEER_TOPK):
                width = PEER_TOPK // (a + 1)
                sa_scr[pos:pos + width] = ta_scr[a][None] + tb_scr[0:width]
                pos += width
            sa_scr[pos:n_cand] = jnp.full((n_cand - pos,) + slab, -jnp.inf, F32)
            _exact_top16(sa_scr.at[0:n_cand], sb_scr.at[0:n_cand], ta_scr)
            picked = jnp.where(sb_scr[0:n_cand] < float(PEER_TOPK), 1.0, 0.0)
            zz = jnp.zeros(slab, F32)
            for k in range(PEER_TOPK):
                zz = zz + jnp.exp(ta_scr[k] - ta_scr[0])
            n1 = jnp.zeros(shape3, F32)
            pos = 0
            for a in range(PEER_TOPK):
                width = PEER_TOPK // (a + 1)
                cnt = jnp.sum(picked[pos:pos + width], axis=0)
                n1 = jnp.where(r1 == float(a), cnt[None], n1)
                pos += width
            n1_st[...] = n1.reshape(NKH, ROUTE_LANES)
            w1_st[...] = (e1 / zz[None]).reshape(NKH, ROUTE_LANES)

        n1_ref[:, :, lanes] = n1_st[...].reshape(PEER_NKEYS, PEER_HEADS, ROUTE_LANES)
        w1_ref[:, :, lanes] = w1_st[...].reshape(PEER_NKEYS, PEER_HEADS, ROUTE_LANES)
        for h in range(PEER_HEADS):
            r2_ref[h, :, lanes] = r2_st[pl.ds(h, PEER_NKEYS, stride=PEER_HEADS), :].astype(BF16)
            w2_ref[h, :, lanes] = w2_st[pl.ds(h, PEER_NKEYS, stride=PEER_HEADS), :].astype(BF16)


def _route_weights(w_q, keys):
    w = w_q.reshape(D_MODEL, PEER_HEADS, 2, PEER_HALF)
    w = jnp.transpose(w, (0, 2, 1, 3)).reshape(D_MODEL, PEER_HEADS * PEER_KEY_DIM)
    eye = jnp.eye(PEER_HEADS, dtype=keys.dtype)
    big = jnp.einsum('phkd,hg->pkhgd', keys, eye).reshape(2, NKH, PEER_HEADS * PEER_HALF)
    return w.astype(BF16), big.astype(BF16)


def _peer_route(q, key_blocks, n_tokens):
    half = PEER_HEADS * PEER_HALF
    shape_kh = jax.ShapeDtypeStruct((PEER_NKEYS, PEER_HEADS, n_tokens), F32)
    shape_hk = jax.ShapeDtypeStruct((PEER_HEADS, PEER_NKEYS, n_tokens), BF16)
    spec_kh = pl.BlockSpec((PEER_NKEYS, PEER_HEADS, ROUTE_TILE), lambda i: (0, 0, i))
    spec_hk = pl.BlockSpec((PEER_HEADS, PEER_NKEYS, ROUTE_TILE), lambda i: (0, 0, i))
    stage = pltpu.VMEM((NKH, ROUTE_LANES), F32)
    cand3 = pltpu.VMEM((PEER_NKEYS, PEER_HEADS, ROUTE_LANES), F32)
    top3 = pltpu.VMEM((PEER_TOPK, PEER_HEADS, ROUTE_LANES), F32)
    return pl.pallas_call(
        _peer_route_kernel,
        grid=(n_tokens // ROUTE_TILE,),
        in_specs=[
            pl.BlockSpec((ROUTE_TILE, 2 * half), lambda i: (i, 0)),
            pl.BlockSpec((None, NKH, half), lambda i: (0, 0, 0)),
            pl.BlockSpec((None, NKH, half), lambda i: (1, 0, 0)),
        ],
        out_specs=[spec_hk, spec_hk, spec_kh, spec_kh],
        out_shape=[shape_hk, shape_hk, shape_kh, shape_kh],
        scratch_shapes=[stage, stage, stage, stage, cand3, cand3, cand3, cand3, top3, top3],
        compiler_params=_cparams("parallel"),
        name="peer_route",
    )(q, key_blocks, key_blocks)


def _tables_kernel(u_ref, v_ref, ub_ref, vt_ref):
    ub_ref[...] = u_ref[...].astype(BF16)
    vt_ref[...] = v_ref[...].T.astype(BF16)


def _prepare_tables(u_tab, v_tab):
    n_exp = u_tab.shape[0]
    row = pl.BlockSpec((TABLE_BLOCK, D_MODEL), lambda i: (i, 0))
    return pl.pallas_call(
        _tables_kernel,
        grid=(n_exp // TABLE_BLOCK,),
        in_specs=[row, row],
        out_specs=[row, pl.BlockSpec((D_MODEL, TABLE_BLOCK), lambda i: (0, i))],
        out_shape=[jax.ShapeDtypeStruct((n_exp, D_MODEL), BF16), jax.ShapeDtypeStruct((D_MODEL, n_exp), BF16)],
        compiler_params=_cparams("parallel"),
        name="peer_tables",
    )(u_tab, v_tab)


def _peer_dense_kernel(ht_ref, u_ref, vt_ref, r2_ref, w2_ref, n1_ref, w1_ref, o_ref):
    c = pl.program_id(1)
    sub = BF16_SUBLANES
    zero = jnp.zeros((), BF16)
    keys_per_sub = SUB_EXPERTS // PEER_NKEYS
    parts = []
    for sb in range(DENSE_EXPERTS // SUB_EXPERTS):
        s = jnp.dot(u_ref[sb * SUB_EXPERTS:(sb + 1) * SUB_EXPERTS, :], ht_ref[...],
                    preferred_element_type=F32)
        act = _gelu(s.astype(BF16))
        for cc in range(keys_per_sub):
            key1 = c * (DENSE_EXPERTS // PEER_NKEYS) + sb * keys_per_sub + cc
            gate = jnp.zeros((PEER_NKEYS // sub, sub, DENSE_TILE), BF16)
            for h in range(PEER_HEADS):
                n1 = jnp.broadcast_to(n1_ref[key1, pl.ds(h, 1), :], (sub, DENSE_TILE)).astype(BF16)
                w1 = jnp.broadcast_to(w1_ref[key1, pl.ds(h, 1), :], (sub, DENSE_TILE)).astype(BF16)
                r2 = r2_ref[h].reshape(PEER_NKEYS // sub, sub, DENSE_TILE)
                w2 = w2_ref[h].reshape(PEER_NKEYS // sub, sub, DENSE_TILE)
                gate = gate + jnp.where(r2 < n1[None], w2, zero) * w1[None]
            parts.append(act[cc * PEER_NKEYS:(cc + 1) * PEER_NKEYS] * gate.reshape(PEER_NKEYS, DENSE_TILE))
    a = jnp.concatenate(parts, axis=0)
    upd = jnp.dot(vt_ref[...], a, preferred_element_type=F32)
    o_ref[...] = jnp.where(c == 0, 0.0, o_ref[...]) + upd


def _peer_dense(tok, h_t, u_tab, vt_tab, route):
    r2, w2, n1, w1 = route
    spec_hk = pl.BlockSpec((PEER_HEADS, PEER_NKEYS, DENSE_TILE), lambda i, c: (0, 0, i))
    spec_kh = pl.BlockSpec((PEER_NKEYS, PEER_HEADS, DENSE_TILE), lambda i, c: (0, 0, i))
    col_spec = pl.BlockSpec((D_MODEL, DENSE_TILE), lambda i, c: (0, i))
    return pl.pallas_call(
        _peer_dense_kernel,
        grid=(tok.total // DENSE_TILE, PEER_EXPERTS // DENSE_EXPERTS),
        in_specs=[
            col_spec,
            pl.BlockSpec((DENSE_EXPERTS, D_MODEL), lambda i, c: (c, 0)),
            pl.BlockSpec((D_MODEL, DENSE_EXPERTS), lambda i, c: (0, c)),
            spec_hk, spec_hk, spec_kh, spec_kh,
        ],
        out_specs=col_spec,
        out_shape=jax.ShapeDtypeStruct((D_MODEL, tok.total), F32),
        compiler_params=_cparams("parallel", "arbitrary"),
        name="peer_dense",
    )(h_t, u_tab, vt_tab, r2, w2, n1, w1)


def kernel(x_prompt, x_sample, state_ssm_re, state_ssm_im, cache_k, cache_v, c, c_ctx, w_mod, b_mod, ln_g, ln_b, ssm_w_in, ssm_a_re, ssm_a_im, ssm_log_dt, ssm_b_re, ssm_b_im, ssm_c_re, ssm_c_im, ssm_d, ssm_w_glu, ssm_w_out, attn_w_qkv, attn_sink, attn_w_out, peer_w_q, peer_keys, peer_u, peer_v):
    depth = w_mod.shape[0]
    n_batch, seq, _ = x_prompt.shape
    dec_batch, dec_seq, _ = x_sample.shape
    tok = _Tokens(n_batch * seq, dec_batch, dec_seq)
    alpha = (2 * depth) ** 0.25

    cond = jnp.concatenate([c, c_ctx[None, :],
                            jnp.zeros((MOD_ROWS - dec_batch - 1, D_MODEL), F32)], axis=0)
    mods_all = _modulation_all(cond, w_mod, b_mod)

    x = jnp.concatenate([x_prompt.reshape(-1, D_MODEL), x_sample.reshape(-1, D_MODEL)], axis=0)
    st_re, st_im, st_k, st_v = [], [], [], []
    for i in range(depth):
        j = i // 2
        mods = mods_all[i]
        if i % 2 == 0:
            u = _mod_linear(tok, x, mods, ssm_w_in[j].astype(BF16), 0, "s5_in")
            ops = _s5_operators(ssm_a_re[j], ssm_a_im[j], ssm_log_dt[j], ssm_b_re[j], ssm_b_im[j],
                                ssm_c_re[j], ssm_c_im[j])
            zeros = jnp.zeros((SSM_GROUPS, n_batch, 4 * SSM_STATE), F32)
            y_c, s_c = _s5_core(_to_chunks(u[:tok.n_ctx], n_batch, seq), zeros, ops,
                                seq // SSM_CHUNK, n_batch, "s5_core_context")
            h0 = _state_to_cols(state_ssm_re[:, j], state_ssm_im[:, j])
            y_s, _ = _s5_core(_to_chunks(u[tok.n_ctx:], dec_batch, dec_seq), h0, ops,
                              dec_seq // SSM_CHUNK, dec_batch, "s5_core_latent")
            y = jnp.concatenate([_from_chunks(y_c, n_batch, seq), _from_chunks(y_s, dec_batch, dec_seq)], axis=0)
            s_re, s_im = _cols_to_state(s_c)
            st_re.append(s_re)
            st_im.append(s_im)
            x = _s5_tail(tok, y, u, x, mods, ssm_d[j], ssm_w_glu[j].astype(BF16), ssm_w_out[j].astype(BF16),
                         ln_g[i, 0], ln_b[i, 0], alpha)
        else:
            qkv = _mod_linear(tok, x, mods, _attn_weights(attn_w_qkv[j]), 0, "attn_qkv")
            o_c, k_c, v_c = _attn_context(qkv, attn_sink[j], n_batch, seq)
            past = cache_k.shape[2]
            o_s = _attn_latent(qkv, _rope_tables(dec_seq), cache_k[:, j].reshape(dec_batch, past, KV_DIM),
                               cache_v[:, j].reshape(dec_batch, past, KV_DIM), attn_sink[j], tok)
            st_k.append(k_c.reshape(n_batch, seq, N_KV_HEADS, HEAD_DIM))
            st_v.append(v_c.reshape(n_batch, seq, N_KV_HEADS, HEAD_DIM))
            x = _mm_postnorm(tok, jnp.concatenate([o_c, o_s], axis=0), x, mods, attn_w_out[j].astype(BF16),
                             ln_g[i, 0], ln_b[i, 0], 2, alpha, "attn_out")
        w_q, key_blocks = _route_weights(peer_w_q[i], peer_keys[i])
        q, h_t = _mod_linear_t(tok, x, mods, w_q, 3, "peer_query")
        route = _peer_route(q, key_blocks, tok.total)
        u_tab, vt_tab = _prepare_tables(peer_u[i], peer_v[i])
        o_t = _peer_dense(tok, h_t, u_tab, vt_tab, route)
        x = _postnorm(tok, o_t, x, mods, ln_g[i, 1], ln_b[i, 1], 5, alpha, "peer_norm")

    y_prompt = x[:tok.n_ctx].reshape(n_batch, seq, D_MODEL)
    y_sample = x[tok.n_ctx:].reshape(dec_batch, dec_seq, D_MODEL)
    return (y_prompt, y_sample, jnp.stack(st_re, axis=1), jnp.stack(st_im, axis=1),
            jnp.stack(st_k, axis=1), jnp.stack(st_v, axis=1))
```
